```python
import math
import jax
import jax.numpy as jnp
from jax import lax
import numpy as np

D_MODEL = 1024
BATCH = 8
SEQ = 4096
DEPTH = 2

GRID_W = 64
CTX_LEN = 256

N_HEADS_ATTN = 4
HEAD_DIM_QK = 64
HEAD_DIM_V = 2 * HEAD_DIM_QK
QK_W = N_HEADS_ATTN * 2 * HEAD_DIM_QK
ATTN_W = N_HEADS_ATTN * HEAD_DIM_V
ROPE_BASE = 10000.0
ROPE_FREQS = HEAD_DIM_QK // 4
Q_BLOCK = 128

CONV_W = 512
CONV_K = 3

POOL_WINDOWS = (2, 4, 8, 16)
POOL_GROUPS = 4
POOL_W = 512
POOL_GW = POOL_W // POOL_GROUPS

N_BRANCH = 3
Q_END = QK_W
K_END = Q_END + QK_W
V_END = K_END + ATTN_W
CB_END = V_END + CONV_W
CC_END = CB_END + CONV_W
CX_END = CC_END + CONV_W
P_END = CX_END + POOL_W
IN_W = P_END + N_BRANCH * D_MODEL

D_FF = 2816
N_EXPERTS = 8
TOP_K = 2
N_DENSE = (DEPTH + 1) // 2
N_MOE = DEPTH // 2

EPS = 1e-6

kernel_name = "hybrid_diffattn_conv_pool_moe_dit"


def rmsnorm(t, g):
    tf = t.astype(jnp.float32)
    n = tf * lax.rsqrt(jnp.mean(tf * tf, axis=-1, keepdims=True) + EPS)
    return (n * g.astype(jnp.float32)).astype(t.dtype)


def modulate(t, shift, scale):
    return t * (1 + scale) + shift


def rope_tables(rows, cols):
    inv = ROPE_BASE ** (-jnp.arange(ROPE_FREQS, dtype=jnp.float32) / ROPE_FREQS)
    ang = jnp.stack([rows.astype(jnp.float32)[:, None] * inv,
                     cols.astype(jnp.float32)[:, None] * inv], axis=1)
    ang = jnp.broadcast_to(ang[:, :, None, :], (ang.shape[0], 2, 2, ROPE_FREQS))
    ang = ang.reshape(ang.shape[0], HEAD_DIM_QK)
    return jnp.cos(ang), jnp.sin(ang)


def apply_rope(t, cos, sin):
    tr = t.reshape(t.shape[:-1] + (2, 2, ROPE_FREQS))
    rot = jnp.concatenate([-tr[..., 1:2, :], tr[..., 0:1, :]], axis=-2).reshape(t.shape)
    c = cos[None, :, None, None, :]
    s = sin[None, :, None, None, :]
    return (t * c + rot * s).astype(t.dtype)


def heads_qk(t):
    return t.reshape(t.shape[:-1] + (N_HEADS_ATTN, 2, HEAD_DIM_QK))


def heads_v(t):
    return t.reshape(t.shape[:-1] + (N_HEADS_ATTN, HEAD_DIM_V))


def diff_attention(q, k, v, lam):
    B, L = q.shape[0], q.shape[1]
    nblk = L // Q_BLOCK
    qb = jnp.moveaxis(q.reshape((B, nblk, Q_BLOCK) + q.shape[2:]), 1, 0)
    scale = HEAD_DIM_QK ** -0.5

    def block(qi):
        s = jnp.einsum('bqhmd,bkhmd->bhmqk', qi, k).astype(jnp.float32) * scale
        p = jax.nn.softmax(s, axis=-1)
        a = p[:, :, 0] - lam * p[:, :, 1]
        return jnp.einsum('bhqk,bkhe->bqhe', a.astype(v.dtype), v)

    o = lax.map(block, qb)
    return jnp.moveaxis(o, 0, 1).reshape(B, L, N_HEADS_ATTN, HEAD_DIM_V)


def short_conv(u, w):
    p = jnp.pad(u, ((0, 0), (1, 1), (0, 0)))
    return w[0] * p[:, :-2] + w[1] * p[:, 1:-1] + w[2] * p[:, 2:]


def multiscale_pool(u, pool_w, pool_scale):
    B, L = u.shape[0], u.shape[1]
    ug = u.reshape(B, L, POOL_GROUPS, POOL_GW)
    cs = jnp.pad(jnp.cumsum(ug.astype(jnp.float32), axis=1), ((0, 0), (1, 0), (0, 0), (0, 0)))
    t = jnp.arange(L)
    outs = []
    for g, w in enumerate(POOL_WINDOWS):
        lo = jnp.clip(t - w // 2, 0, L)
        hi = jnp.clip(t - w // 2 + w, 0, L)
        csg = cs[:, :, g]
        mean = (csg[:, hi] - csg[:, lo]) / (hi - lo).astype(jnp.float32)[None, :, None]
        outs.append(mean - ug[:, :, g].astype(jnp.float32))
    p = jnp.stack(outs, axis=2).astype(u.dtype)
    y = jnp.einsum('blgc,gcd->blgd', p, pool_w).reshape(B, L, POOL_W)
    return y * pool_scale


def mixer_merge(attn_heads, cb, cc, cx, pin, gts, g_sub, lam_init, conv_w_i, pool_w_i,
                pool_scale_i, w_branch_i, w_out_i):
    B, L = cb.shape[0], cb.shape[1]
    attn_o = (rmsnorm(attn_heads, g_sub) * (1.0 - lam_init)).reshape(B, L, ATTN_W)
    conv_o = cb * short_conv(cc * cx, conv_w_i)
    pool_o = multiscale_pool(pin, pool_w_i, pool_scale_i)
    g = jax.nn.sigmoid(gts.reshape(B, L, N_BRANCH, D_MODEL))
    merged = (g[:, :, 0] * (attn_o @ w_branch_i[0])
              + g[:, :, 1] * (conv_o @ w_branch_i[1])
              + g[:, :, 2] * (pool_o @ w_branch_i[2]))
    return merged @ w_out_i


def swiglu(h, wg, wu, wd):
    return (jax.nn.silu(h @ wg) * (h @ wu)) @ wd


def moe_ffn(h, wr, wg, wu, wd):
    logits = (h @ wr).astype(jnp.float32)
    top_v, top_i = lax.top_k(logits, TOP_K)
    top_p = jax.nn.softmax(top_v, axis=-1)
    combine = jnp.sum(jax.nn.one_hot(top_i, N_EXPERTS, dtype=jnp.float32) * top_p[..., None],
                      axis=-2).astype(h.dtype)
    out = jnp.zeros_like(h)
    for e in range(N_EXPERTS):
        out = out + combine[..., e:e + 1] * swiglu(h, wg[e], wu[e], wd[e])
    return out


def setup_inputs(seed: int = 0) -> dict:
    key = jax.random.key(seed)
    ks = jax.random.split(key, 32)
    f32 = jnp.float32
    D = D_MODEL

    def nrm(k, shape, scale):
        return jax.random.normal(k, shape, f32) * scale

    return {
        "x": nrm(ks[0], (BATCH, SEQ, D), 1.0),
        "c": nrm(ks[1], (BATCH, D), 1.0),
        "ctx": nrm(ks[2], (BATCH, CTX_LEN, D), 1.0),
        "c_ctx": nrm(ks[3], (D,), 1.0),
        "w_mod": nrm(ks[4], (DEPTH, D, 6 * D), 0.5 * D ** -0.5),
        "b_mod": nrm(ks[5], (DEPTH, 6 * D), 0.01),
        "g_pre_mix": 1.0 + nrm(ks[6], (DEPTH, D), 0.05),
        "g_post_mix": 1.0 + nrm(ks[7], (DEPTH, D), 0.05),
        "g_pre_ffn": 1.0 + nrm(ks[8], (DEPTH, D), 0.05),
        "g_post_ffn": 1.0 + nrm(ks[9], (DEPTH, D), 0.05),
        "w_in": nrm(ks[10], (DEPTH, D, IN_W), D ** -0.5),
        "lambda_q1": nrm(ks[11], (DEPTH, HEAD_DIM_QK), 0.1),
        "lambda_k1": nrm(ks[12], (DEPTH, HEAD_DIM_QK), 0.1),
        "lambda_q2": nrm(ks[13], (DEPTH, HEAD_DIM_QK), 0.1),
        "lambda_k2": nrm(ks[14], (DEPTH, HEAD_DIM_QK), 0.1),
        "g_subln": 1.0 + nrm(ks[15], (DEPTH, HEAD_DIM_V), 0.05),
        "conv_w": nrm(ks[16], (DEPTH, CONV_K, CONV_W), CONV_K ** -0.5),
        "pool_w": nrm(ks[17], (DEPTH, POOL_GROUPS, POOL_GW, POOL_GW), POOL_GW ** -0.5),
        "pool_scale": 1.0 + nrm(ks[18], (DEPTH, POOL_W), 0.05),
        "w_branch": nrm(ks[19], (DEPTH, N_BRANCH, ATTN_W, D), ATTN_W ** -0.5),
        "w_out": nrm(ks[20], (DEPTH, D, D), D ** -0.5),
        "ffn_w_gate": nrm(ks[21], (N_DENSE, D, D_FF), D ** -0.5),
        "ffn_w_up": nrm(ks[22], (N_DENSE, D, D_FF), D ** -0.5),
        "ffn_w_down": nrm(ks[23], (N_DENSE, D_FF, D), D_FF ** -0.5),
        "router_w": nrm(ks[24], (N_MOE, D, N_EXPERTS), D ** -0.5),
        "moe_w_gate": nrm(ks[25], (N_MOE, N_EXPERTS, D, D_FF), D ** -0.5),
        "moe_w_up": nrm(ks[26], (N_MOE, N_EXPERTS, D, D_FF), D ** -0.5),
        "moe_w_down": nrm(ks[27], (N_MOE, N_EXPERTS, D_FF, D), D_FF ** -0.5),
    }


def reference(x, c, ctx, c_ctx, w_mod, b_mod, g_pre_mix, g_post_mix, g_pre_ffn, g_post_ffn,
              w_in, lambda_q1, lambda_k1, lambda_q2, lambda_k2, g_subln, conv_w, pool_w,
              pool_scale, w_branch, w_out, ffn_w_gate, ffn_w_up, ffn_w_down, router_w,
              moe_w_gate, moe_w_up, moe_w_down):
    S = x.shape[1]
    ROWS = S // GRID_W
    rows = jnp.repeat(jnp.arange(ROWS), GRID_W)
    cols = jnp.tile(jnp.arange(GRID_W), ROWS)
    cos, sin = rope_tables(rows, cols)
    y = ctx

    for i in range(DEPTH):
        last = i == DEPTH - 1
        mod_lat = (jax.nn.silu(c) @ w_mod[i] + b_mod[i])[:, None, :]
        mod_ctx = jax.nn.silu(c_ctx) @ w_mod[i] + b_mod[i]
        sh_m, sc_m, gt_m, sh_f, sc_f, gt_f = jnp.split(mod_lat, 6, axis=-1)
        csh_m, csc_m, cgt_m, csh_f, csc_f, cgt_f = jnp.split(mod_ctx, 6, axis=-1)

        lam_init = 0.8 - 0.6 * math.exp(-0.3 * i)
        lam = (jnp.exp(jnp.sum(lambda_q1[i].astype(jnp.float32) * lambda_k1[i].astype(jnp.float32)))
               - jnp.exp(jnp.sum(lambda_q2[i].astype(jnp.float32) * lambda_k2[i].astype(jnp.float32)))
               + lam_init)

        h_lat = modulate(rmsnorm(x, g_pre_mix[i]), sh_m, sc_m)
        h_ctx = modulate(rmsnorm(y, g_pre_mix[i]), csh_m, csc_m)

        z = h_lat @ w_in[i]
        q, k, v, cb, cc, cx, pin, gts = jnp.split(
            z, [Q_END, K_END, V_END, CB_END, CC_END, CX_END, P_END], axis=-1)
        q = apply_rope(heads_qk(q), cos, sin)
        k = apply_rope(heads_qk(k), cos, sin)
        v = heads_v(v)

        if last:
            kc, vc = jnp.split(h_ctx @ w_in[i][:, Q_END:V_END], [QK_W], axis=-1)
            kc, vc = heads_qk(kc), heads_v(vc)
        else:
            zc = h_ctx @ w_in[i]
            qc, kc, vc, cbc, ccc, cxc, pinc, gtsc = jnp.split(
                zc, [Q_END, K_END, V_END, CB_END, CC_END, CX_END, P_END], axis=-1)
            qc, kc, vc = heads_qk(qc), heads_qk(kc), heads_v(vc)
            attn_c = diff_attention(qc, kc, vc, lam)
            mix_c = mixer_merge(attn_c, cbc, ccc, cxc, pinc, gtsc, g_subln[i], lam_init, conv_w[i],
                                pool_w[i], pool_scale[i], w_branch[i], w_out[i])

        k_all = jnp.concatenate([kc, k], axis=1)
        v_all = jnp.concatenate([vc, v], axis=1)
        attn_l = diff_attention(q, k_all, v_all, lam)
        mix_l = mixer_merge(attn_l, cb, cc, cx, pin, gts, g_subln[i], lam_init, conv_w[i],
                            pool_w[i], pool_scale[i], w_branch[i], w_out[i])
        x = x + gt_m * rmsnorm(mix_l, g_post_mix[i])
        if not last:
            y = y + cgt_m * rmsnorm(mix_c, g_post_mix[i])

        j = i // 2
        h2 = modulate(rmsnorm(x, g_pre_ffn[i]), sh_f, sc_f)
        if i % 2 == 0:
            f = swiglu(h2, ffn_w_gate[j], ffn_w_up[j], ffn_w_down[j])
        else:
            f = moe_ffn(h2, router_w[j], moe_w_gate[j], moe_w_up[j], moe_w_down[j])
        x = x + gt_f * rmsnorm(f, g_post_ffn[i])
        if not last:
            h2c = modulate(rmsnorm(y, g_pre_ffn[i]), csh_f, csc_f)
            if i % 2 == 0:
                fc = swiglu(h2c, ffn_w_gate[j], ffn_w_up[j], ffn_w_down[j])
            else:
                fc = moe_ffn(h2c, router_w[j], moe_w_gate[j], moe_w_up[j], moe_w_down[j])
            y = y + cgt_f * rmsnorm(fc, g_post_ffn[i])

    return x
```

```python
import functools
import math

import jax
import jax.numpy as jnp
from jax import lax
from jax.experimental import pallas as pl
from jax.experimental.pallas import tpu as pltpu

F32 = jnp.float32
BF16 = jnp.bfloat16

D_MODEL = 1024
GRID_W = 64
N_HEADS = 4
HEAD_QK = 64
HEAD_V = 128
ROPE_BASE = 10000.0
ROPE_FREQS = HEAD_QK // 4
CONV_K = 3
POOL_WINDOWS = (2, 4, 8, 16)
POOL_GW = 128
N_BRANCH = 3
BRANCH_W = 512
GATES_W = N_BRANCH * D_MODEL
IN_W = GATES_W + 7 * BRANCH_W
D_FF = 2816
N_EXPERTS = 8
EPS = 1e-6

LANES = 128
BF16_SUBLANES = 16
COL_BLK = 512
Q_COL, K_COL, V_COL, CB_COL, CC_COL, CX_COL, PIN_COL = range(6, 13)
HEAD_BLKS = COL_BLK // 128
MOD_ROWS = 16
CTX_MOD_ROW = 8
HALO = BF16_SUBLANES
FF_CHUNK = 256
VMEM_LIMIT = 56 * 1024 * 1024


def _params(sem, vmem=VMEM_LIMIT):
    return pltpu.CompilerParams(dimension_semantics=sem, vmem_limit_bytes=vmem)


def _rms(t, g):
    return t * lax.rsqrt(jnp.mean(t * t, axis=-1, keepdims=True) + EPS) * g


def _mod_kernel(c_ref, w_ref, b_ref, o_ref):
    c = c_ref[...]
    s = c * jax.nn.sigmoid(c)
    o_ref[0] = jnp.dot(s, w_ref[0], preferred_element_type=F32,
                       precision=lax.Precision.HIGHEST) + b_ref[0]


def _modulation(cvec, w_mod, b_mod):
    depth = w_mod.shape[0]
    wcols = w_mod.shape[2]
    tn = 1536
    return pl.pallas_call(
        _mod_kernel,
        grid=(depth, wcols // tn),
        in_specs=[pl.BlockSpec((MOD_ROWS, D_MODEL), lambda l, j: (0, 0)),
                  pl.BlockSpec((1, D_MODEL, tn), lambda l, j: (l, 0, j)),
                  pl.BlockSpec((1, 1, tn), lambda l, j: (l, 0, j))],
        out_specs=pl.BlockSpec((1, MOD_ROWS, tn), lambda l, j: (l, 0, j)),
        out_shape=jax.ShapeDtypeStruct((depth, MOD_ROWS, wcols), F32),
        compiler_params=_params(("parallel", "parallel")),
        name="modulation",
    )(cvec, w_mod, b_mod.reshape(depth, 1, wcols))


def _mod_spec(layer, chunk, ctx, ngrid):
    def idx(*g):
        row = CTX_MOD_ROW if ctx else g[0]
        return (layer * MOD_ROWS + row, 0, chunk)
    del ngrid
    return pl.BlockSpec((1, 1, D_MODEL), idx)


def _inproj_kernel(x_ref, g_ref, sh_ref, sc_ref, w_ref, cos_ref, sa_ref, sb_ref, o_ref, h_scr,
                   *, rope, col_start):
    j = pl.program_id(2)

    @pl.when(j == 0)
    def _():
        n = _rms(x_ref[0], g_ref[...])
        h_scr[...] = (n * (1.0 + sc_ref[0]) + sh_ref[0]).astype(BF16)

    z = jnp.dot(h_scr[...], w_ref[...], preferred_element_type=F32)
    if not rope:
        o_ref[0] = z.astype(BF16)
        return

    is_qk = jnp.logical_or(j + col_start == Q_COL, j + col_start == K_COL)

    @pl.when(is_qk)
    def _():
        cos, sa, sb = cos_ref[...], sa_ref[...], sb_ref[...]
        for c in range(COL_BLK // LANES):
            t = z[:, c * LANES:(c + 1) * LANES]
            r = (t * cos + pltpu.roll(t, LANES - ROPE_FREQS, 1) * sa
                 + pltpu.roll(t, ROPE_FREQS, 1) * sb)
            o_ref[0, :, c * LANES:(c + 1) * LANES] = r.astype(BF16)

    @pl.when(jnp.logical_not(is_qk))
    def _():
        o_ref[0] = z.astype(BF16)


def _inproj(x, g, mod, w, tables, *, layer, ctx, rope, col_start, ncols, tm):
    bt, L, _ = x.shape
    cos, sa, sb = tables
    grid = (bt, L // tm, ncols)
    tab_spec = pl.BlockSpec((tm, LANES), lambda b, i, j: (i, 0))
    return pl.pallas_call(
        functools.partial(_inproj_kernel, rope=rope, col_start=col_start),
        grid=grid,
        in_specs=[pl.BlockSpec((1, tm, D_MODEL), lambda b, i, j: (b, i, 0)),
                  pl.BlockSpec((1, D_MODEL), lambda b, i, j: (0, 0)),
                  _mod_spec(layer, 0, ctx, 3),
                  _mod_spec(layer, 1, ctx, 3),
                  pl.BlockSpec((D_MODEL, COL_BLK), lambda b, i, j: (0, j + col_start)),
                  tab_spec, tab_spec, tab_spec],
        out_specs=pl.BlockSpec((1, tm, COL_BLK), lambda b, i, j: (b, i, j)),
        out_shape=jax.ShapeDtypeStruct((bt, L, ncols * COL_BLK), BF16),
        scratch_shapes=[pltpu.VMEM((tm, D_MODEL), BF16)],
        compiler_params=_params(("parallel", "parallel", "arbitrary")),
        name="inproj",
    )(x, g, mod, mod, w, cos, sa, sb)


def _attn_kernel(*refs, nseg, lam_init):
    q_ref, lam_ref, gs_ref = refs[0], refs[1], refs[2]
    kv_refs = refs[3:3 + 2 * nseg]
    o_ref = refs[3 + 2 * nseg]

    lv = lam_ref[...]
    lam = (jnp.exp(jnp.sum(lv[0:1] * lv[1:2], axis=-1, keepdims=True))
           - jnp.exp(jnp.sum(lv[2:3] * lv[3:4], axis=-1, keepdims=True)) + lam_init)

    q = q_ref[0]
    lane = lax.broadcasted_iota(jnp.int32, q.shape, 1)
    qs = (q.astype(F32) * (HEAD_QK ** -0.5)).astype(BF16)
    q1 = jnp.where(lane < HEAD_QK, qs, jnp.zeros_like(qs))
    q2 = jnp.where(lane >= HEAD_QK, qs, jnp.zeros_like(qs))

    dn = (((1,), (1,)), ((), ()))
    s1 = [lax.dot_general(q1, kv_refs[2 * s][0], dn, preferred_element_type=F32) for s in range(nseg)]
    s2 = [lax.dot_general(q2, kv_refs[2 * s][0], dn, preferred_element_type=F32) for s in range(nseg)]

    def softmax_parts(ss):
        m = functools.reduce(jnp.maximum, [jnp.max(s, axis=-1, keepdims=True) for s in ss])
        es = [jnp.exp(s - m) for s in ss]
        l = functools.reduce(jnp.add, [jnp.sum(e, axis=-1, keepdims=True) for e in es])
        return es, l

    e1, l1 = softmax_parts(s1)
    e2, l2 = softmax_parts(s2)
    r1 = 1.0 / l1
    r2 = lam / l2
    o = None
    for s in range(nseg):
        a = (e1[s] * r1 - e2[s] * r2).astype(BF16)
        part = jnp.dot(a, kv_refs[2 * s + 1][0], preferred_element_type=F32)
        o = part if o is None else o + part
    o_ref[0] = (_rms(o, gs_ref[...]) * (1.0 - lam_init)).astype(BF16)


def _attention(zq, segs, lam_vecs, g_sub, *, lam_init, tq):
    bt, lq, _ = zq.shape
    q0 = Q_COL * HEAD_BLKS
    in_specs = [pl.BlockSpec((1, tq, HEAD_V), lambda b, h, i: (b, i, q0 + h)),
                pl.BlockSpec((4, HEAD_QK), lambda b, h, i: (0, 0)),
                pl.BlockSpec((1, HEAD_V), lambda b, h, i: (0, 0))]
    args = [zq, lam_vecs, g_sub]
    for arr, kc, vc in segs:
        t = arr.shape[1]
        in_specs.append(pl.BlockSpec((1, t, HEAD_V), lambda b, h, i, kc=kc: (b, 0, kc + h)))
        in_specs.append(pl.BlockSpec((1, t, HEAD_V), lambda b, h, i, vc=vc: (b, 0, vc + h)))
        args += [arr, arr]
    return pl.pallas_call(
        functools.partial(_attn_kernel, nseg=len(segs), lam_init=lam_init),
        grid=(bt, N_HEADS, lq // tq),
        in_specs=in_specs,
        out_specs=pl.BlockSpec((1, tq, HEAD_V), lambda b, h, i: (b, i, h)),
        out_shape=jax.ShapeDtypeStruct((bt, lq, N_HEADS * HEAD_V), BF16),
        compiler_params=_params(("parallel", "parallel", "arbitrary")),
        name="diff_attention",
    )(*args)


def _merge_kernel(attn_ref, cb_ref, cc_ref, cx_ref, pin_ref, g0_ref, g1_ref, g2_ref,
                  ccp_ref, cxp_ref, pinp_ref, ccn_ref, cxn_ref, pinn_ref,
                  x_ref, gate_ref, gpost_ref, cw_ref, pw_ref, ps_ref, wb_ref, wo_ref,
                  o_ref, u_scr, p_scr, *, tm, seq_len):
    i = pl.program_id(1)
    first = i == 0
    last = i == pl.num_programs(1) - 1

    def f32(ref):
        return ref[0].astype(F32)

    u_scr[0:HALO] = jnp.where(first, 0.0, f32(ccp_ref) * f32(cxp_ref))
    u_scr[HALO:HALO + tm] = f32(cc_ref) * f32(cx_ref)
    u_scr[HALO + tm:2 * HALO + tm] = jnp.where(last, 0.0, f32(ccn_ref) * f32(cxn_ref))
    p_scr[0:HALO] = jnp.where(first, 0.0, f32(pinp_ref))
    p_scr[HALO:HALO + tm] = f32(pin_ref)
    p_scr[HALO + tm:2 * HALO + tm] = jnp.where(last, 0.0, f32(pinn_ref))

    cw = cw_ref[...]
    conv = (cw[0:1] * u_scr[HALO - 1:HALO - 1 + tm] + cw[1:2] * u_scr[HALO:HALO + tm]
            + cw[2:3] * u_scr[HALO + 1:HALO + 1 + tm])
    conv_o = (f32(cb_ref) * conv).astype(BF16)

    t = i * tm + lax.broadcasted_iota(jnp.int32, (tm, 1), 0)
    pool_parts = []
    for g, w in enumerate(POOL_WINDOWS):
        cs = slice(g * POOL_GW, (g + 1) * POOL_GW)
        acc = p_scr[HALO - w // 2:HALO - w // 2 + tm, cs]
        for d in range(-w // 2 + 1, w // 2):
            acc = acc + p_scr[HALO + d:HALO + d + tm, cs]
        lo = jnp.maximum(t - w // 2, 0)
        hi = jnp.minimum(t - w // 2 + w, seq_len)
        p = acc / (hi - lo).astype(F32) - p_scr[HALO:HALO + tm, cs]
        pool_parts.append(jnp.dot(p.astype(BF16), pw_ref[g], preferred_element_type=F32))
    pool_o = (jnp.concatenate(pool_parts, axis=-1) * ps_ref[...]).astype(BF16)

    merged = (jax.nn.sigmoid(f32(g0_ref)) * jnp.dot(attn_ref[0], wb_ref[0], preferred_element_type=F32)
              + jax.nn.sigmoid(f32(g1_ref)) * jnp.dot(conv_o, wb_ref[1], preferred_element_type=F32)
              + jax.nn.sigmoid(f32(g2_ref)) * jnp.dot(pool_o, wb_ref[2], preferred_element_type=F32))
    mix = jnp.dot(merged.astype(BF16), wo_ref[...], preferred_element_type=F32)
    o_ref[0] = x_ref[0] + gate_ref[0] * _rms(mix, gpost_ref[...])


def _merge(attn, z, x, mod, g_post, conv_w, pool_w, pool_scale, w_branch, w_out, *, layer, ctx, tm):
    bt, L, _ = x.shape
    hb = tm // HALO
    nhb = L // HALO

    def col(c, width=COL_BLK):
        return pl.BlockSpec((1, tm, width), lambda b, i, c=c: (b, i, c))

    def prev(c):
        return pl.BlockSpec((1, HALO, COL_BLK), lambda b, i, c=c: (b, jnp.maximum(i * hb - 1, 0), c))

    def nxt(c):
        return pl.BlockSpec((1, HALO, COL_BLK),
                            lambda b, i, c=c: (b, jnp.minimum((i + 1) * hb, nhb - 1), c))

    def full(a):
        return pl.BlockSpec(a.shape, lambda b, i, n=a.ndim: (0,) * n)

    in_specs = [pl.BlockSpec((1, tm, BRANCH_W), lambda b, i: (b, i, 0)),
                col(CB_COL), col(CC_COL), col(CX_COL), col(PIN_COL)]
    in_specs += [col(k, D_MODEL) for k in range(N_BRANCH)]
    in_specs += [prev(CC_COL), prev(CX_COL), prev(PIN_COL), nxt(CC_COL), nxt(CX_COL), nxt(PIN_COL)]
    in_specs += [pl.BlockSpec((1, tm, D_MODEL), lambda b, i: (b, i, 0)),
                 _mod_spec(layer, 2, ctx, 2),
                 full(g_post), full(conv_w), full(pool_w), full(pool_scale), full(w_branch),
                 full(w_out)]
    zz = z
    return pl.pallas_call(
        functools.partial(_merge_kernel, tm=tm, seq_len=L),
        grid=(bt, L // tm),
        in_specs=in_specs,
        out_specs=pl.BlockSpec((1, tm, D_MODEL), lambda b, i: (b, i, 0)),
        out_shape=jax.ShapeDtypeStruct((bt, L, D_MODEL), F32),
        scratch_shapes=[pltpu.VMEM((tm + 2 * HALO, BRANCH_W), F32),
                        pltpu.VMEM((tm + 2 * HALO, BRANCH_W), F32)],
        compiler_params=_params(("parallel", "arbitrary")),
        name="mixer_merge",
    )(attn, *([zz] * 13), x, mod, g_post, conv_w, pool_w, pool_scale, w_branch, w_out)


def _router_kernel(x_ref, g_ref, sh_ref, sc_ref, wr_ref, o_ref):
    h = _rms(x_ref[0], g_ref[...]) * (1.0 + sc_ref[0]) + sh_ref[0]
    logits = jnp.dot(h, wr_ref[...], preferred_element_type=F32, precision=lax.Precision.HIGHEST)
    lane = lax.broadcasted_iota(jnp.int32, logits.shape, 1)
    neg = jnp.float32(-jnp.inf)
    l1 = jnp.where(lane < N_EXPERTS, logits, neg)
    m1 = jnp.max(l1, axis=-1, keepdims=True)
    i1 = jnp.min(jnp.where(l1 == m1, lane, LANES), axis=-1, keepdims=True)
    l2 = jnp.where(lane == i1, neg, l1)
    m2 = jnp.max(l2, axis=-1, keepdims=True)
    i2 = jnp.min(jnp.where(l2 == m2, lane, LANES), axis=-1, keepdims=True)
    e2 = jnp.exp(m2 - m1)
    den = 1.0 + e2
    o_ref[0] = jnp.where(lane == i1, 1.0 / den, 0.0) + jnp.where(lane == i2, e2 / den, 0.0)


def _router(x, g, mod, wr_pad, *, layer, ctx, tm):
    bt, L, _ = x.shape
    return pl.pallas_call(
        _router_kernel,
        grid=(bt, L // tm),
        in_specs=[pl.BlockSpec((1, tm, D_MODEL), lambda b, i: (b, i, 0)),
                  pl.BlockSpec((1, D_MODEL), lambda b, i: (0, 0)),
                  _mod_spec(layer, 3, ctx, 2),
                  _mod_spec(layer, 4, ctx, 2),
                  pl.BlockSpec((D_MODEL, LANES), lambda b, i: (0, 0))],
        out_specs=pl.BlockSpec((1, tm, LANES), lambda b, i: (b, i, 0)),
        out_shape=jax.ShapeDtypeStruct((bt, L, LANES), F32),
        compiler_params=_params(("parallel", "parallel")),
        name="router",
    )(x, g, mod, mod, wr_pad)


def _ffn_kernel(x_ref, g_ref, sh_ref, sc_ref, gate_ref, gpost_ref, *rest, routed):
    comb_ref = rest[0] if routed else None
    wg_ref, wu_ref, wd_ref, o_ref, h_scr, acc_scr = rest[-6:]
    e = pl.program_id(2)

    @pl.when(e == 0)
    def _():
        n = _rms(x_ref[0], g_ref[...])
        h_scr[...] = (n * (1.0 + sc_ref[0]) + sh_ref[0]).astype(BF16)
        acc_scr[...] = jnp.zeros_like(acc_scr)

    h = h_scr[...]
    f = None
    for c in range(D_FF // FF_CHUNK):
        cs = slice(c * FF_CHUNK, (c + 1) * FF_CHUNK)
        gt = jnp.dot(h, wg_ref[0, :, cs], preferred_element_type=F32)
        up = jnp.dot(h, wu_ref[0, :, cs], preferred_element_type=F32)
        a = (gt * jax.nn.sigmoid(gt) * up).astype(BF16)
        part = jnp.dot(a, wd_ref[0, cs, :], preferred_element_type=F32)
        f = part if f is None else f + part
    if routed:
        comb = comb_ref[0]
        lane = lax.broadcasted_iota(jnp.int32, comb.shape, 1)
        f = f * jnp.sum(jnp.where(lane == e, comb, 0.0), axis=-1, keepdims=True)
    acc_scr[...] += f

    @pl.when(e == pl.num_programs(2) - 1)
    def _():
        o_ref[0] = x_ref[0] + gate_ref[0] * _rms(acc_scr[...], gpost_ref[...])


def _ffn(x, g_pre, g_post, mod, comb, wg, wu, wd, *, layer, ctx, routed, tm):
    bt, L, _ = x.shape
    ne = wg.shape[0]
    comb_specs = [pl.BlockSpec((1, tm, LANES), lambda b, i, e: (b, i, 0))] if routed else []
    comb_args = [comb] if routed else []
    return pl.pallas_call(
        functools.partial(_ffn_kernel, routed=routed),
        grid=(bt, L // tm, ne),
        in_specs=[pl.BlockSpec((1, tm, D_MODEL), lambda b, i, e: (b, i, 0)),
                  pl.BlockSpec((1, D_MODEL), lambda b, i, e: (0, 0)),
                  _mod_spec(layer, 3, ctx, 3),
                  _mod_spec(layer, 4, ctx, 3),
                  _mod_spec(layer, 5, ctx, 3),
                  pl.BlockSpec((1, D_MODEL), lambda b, i, e: (0, 0))] + comb_specs + [
                  pl.BlockSpec((1, D_MODEL, D_FF), lambda b, i, e: (e, 0, 0)),
                  pl.BlockSpec((1, D_MODEL, D_FF), lambda b, i, e: (e, 0, 0)),
                  pl.BlockSpec((1, D_FF, D_MODEL), lambda b, i, e: (e, 0, 0))],
        out_specs=pl.BlockSpec((1, tm, D_MODEL), lambda b, i, e: (b, i, 0)),
        out_shape=jax.ShapeDtypeStruct((bt, L, D_MODEL), F32),
        scratch_shapes=[pltpu.VMEM((tm, D_MODEL), BF16), pltpu.VMEM((tm, D_MODEL), F32)],
        compiler_params=_params(("parallel", "parallel", "arbitrary")),
        name="channel_mixer",
    )(x, g_pre, mod, mod, mod, g_post, *comb_args, wg, wu, wd)


def _rope_tables(seq_len):
    rows = jnp.repeat(jnp.arange(seq_len // GRID_W), GRID_W).astype(F32)
    cols = jnp.tile(jnp.arange(GRID_W), seq_len // GRID_W).astype(F32)
    inv = ROPE_BASE ** (-jnp.arange(ROPE_FREQS, dtype=F32) / ROPE_FREQS)
    ang = jnp.concatenate([rows[:, None] * inv, rows[:, None] * inv,
                           cols[:, None] * inv, cols[:, None] * inv], axis=1)
    ang = jnp.tile(ang, (1, LANES // HEAD_QK))
    cos, sin = jnp.cos(ang), jnp.sin(ang)
    low = (jnp.arange(LANES) % (2 * ROPE_FREQS)) < ROPE_FREQS
    return cos, jnp.where(low, -sin, 0.0), jnp.where(low, 0.0, sin)


def kernel(x, c, ctx, c_ctx, w_mod, b_mod, g_pre_mix, g_post_mix, g_pre_ffn, g_post_ffn,
           w_in, lambda_q1, lambda_k1, lambda_q2, lambda_k2, g_subln, conv_w, pool_w,
           pool_scale, w_branch, w_out, ffn_w_gate, ffn_w_up, ffn_w_down, router_w,
           moe_w_gate, moe_w_up, moe_w_down):
    depth = w_in.shape[0]
    bsz, seq, _ = x.shape
    ctx_len = ctx.shape[1]

    cvec = jnp.zeros((MOD_ROWS, D_MODEL), F32).at[:bsz].set(c).at[CTX_MOD_ROW].set(c_ctx)
    mod = _modulation(cvec, w_mod, b_mod).reshape(depth * MOD_ROWS, 1, 6 * D_MODEL)
    tables = _rope_tables(seq)
    no_tables = tuple(t[:ctx_len] for t in tables)

    y = ctx
    for i in range(depth):
        last = i == depth - 1
        lam_init = 0.8 - 0.6 * math.exp(-0.3 * i)
        split = w_in.shape[2] - GATES_W
        w_in_i = jnp.concatenate([w_in[i, :, split:], w_in[i, :, :split]], axis=1).astype(BF16)
        lam_vecs = jnp.stack([lambda_q1[i], lambda_k1[i], lambda_q2[i], lambda_k2[i]]).astype(F32)
        g_sub = g_subln[i].reshape(1, HEAD_V)
        g_pm, g_qm = g_pre_mix[i].reshape(1, D_MODEL), g_post_mix[i].reshape(1, D_MODEL)
        g_pf, g_qf = g_pre_ffn[i].reshape(1, D_MODEL), g_post_ffn[i].reshape(1, D_MODEL)
        mix_w = (conv_w[i], pool_w[i].astype(BF16), pool_scale[i].reshape(1, BRANCH_W),
                 w_branch[i].astype(BF16), w_out[i].astype(BF16))

        z = _inproj(x, g_pm, mod, w_in_i, tables, layer=i, ctx=False, rope=True,
                    col_start=0, ncols=IN_W // COL_BLK, tm=1024)
        kv_full = (K_COL * HEAD_BLKS, V_COL * HEAD_BLKS)
        if last:
            zc = _inproj(y, g_pm, mod, w_in_i, no_tables, layer=i, ctx=True, rope=False,
                         col_start=K_COL, ncols=2, tm=ctx_len)
            segs = [(z,) + kv_full, (zc, 0, HEAD_BLKS)]
        else:
            zc = _inproj(y, g_pm, mod, w_in_i, no_tables, layer=i, ctx=True, rope=False,
                         col_start=0, ncols=IN_W // COL_BLK, tm=ctx_len)
            segs = [(z,) + kv_full, (zc,) + kv_full]
            attn_c = _attention(zc, [(zc,) + kv_full], lam_vecs, g_sub, lam_init=lam_init,
                                tq=ctx_len)
            y = _merge(attn_c, zc, y, mod, g_qm, *mix_w, layer=i, ctx=True, tm=ctx_len)
        attn_l = _attention(z, segs, lam_vecs, g_sub, lam_init=lam_init, tq=256)
        x = _merge(attn_l, z, x, mod, g_qm, *mix_w, layer=i, ctx=False, tm=512)

        j = i // 2
        if i % 2 == 0:
            ffn_w = (ffn_w_gate[j:j + 1].astype(BF16), ffn_w_up[j:j + 1].astype(BF16),
                     ffn_w_down[j:j + 1].astype(BF16))
            routed = False
        else:
            ffn_w = (moe_w_gate[j].astype(BF16), moe_w_up[j].astype(BF16), moe_w_down[j].astype(BF16))
            wr_pad = jnp.zeros((D_MODEL, LANES), F32).at[:, :N_EXPERTS].set(router_w[j])
            routed = True
        streams = [("x", x, False, 512)] + ([] if last else [("y", y, True, ctx_len)])
        outs = {}
        for name, t, is_ctx, tm in streams:
            comb = _router(t, g_pf, mod, wr_pad, layer=i, ctx=is_ctx, tm=tm) if routed else None
            outs[name] = _ffn(t, g_pf, g_qf, mod, comb, *ffn_w, layer=i, ctx=is_ctx,
                              routed=routed, tm=tm)
        x = outs["x"]
        if not last:
            y = outs["y"]
    return x
```

```python
import functools
import math

import jax
import jax.numpy as jnp
from jax import lax
from jax.experimental import pallas as pl
from jax.experimental.pallas import tpu as pltpu

F32 = jnp.float32
BF16 = jnp.bfloat16

D_MODEL = 1024
GRID_W = 64
N_HEADS = 4
HEAD_QK = 64
HEAD_V = 128
ROPE_BASE = 10000.0
ROPE_FREQS = HEAD_QK // 4
CONV_K = 3
POOL_WINDOWS = (2, 4, 8, 16)
POOL_GW = 128
N_BRANCH = 3
BRANCH_W = 512
GATES_W = N_BRANCH * D_MODEL
IN_W = GATES_W + 7 * BRANCH_W
D_FF = 2816
N_EXPERTS = 8
EPS = 1e-6

LANES = 128
BF16_SUBLANES = 16
COL_BLK = 512
Q_COL, K_COL, V_COL, CB_COL, CC_COL, CX_COL, PIN_COL = range(6, 13)
HEAD_BLKS = COL_BLK // 128
MOD_ROWS = 16
CTX_MOD_ROW = 8
HALO = BF16_SUBLANES
FF_CHUNK = 256
VMEM_LIMIT = 56 * 1024 * 1024


def _params(sem, vmem=VMEM_LIMIT):
    return pltpu.CompilerParams(dimension_semantics=sem, vmem_limit_bytes=vmem)


def _rms(t, g):
    return t * lax.rsqrt(jnp.mean(t * t, axis=-1, keepdims=True) + EPS) * g


def _mod_kernel(c_ref, w_ref, b_ref, o_ref):
    c = c_ref[...]
    s = c * jax.nn.sigmoid(c)
    o_ref[0] = jnp.dot(s, w_ref[0], preferred_element_type=F32,
                       precision=lax.Precision.HIGHEST) + b_ref[0]


def _modulation(cvec, w_mod, b_mod):
    depth = w_mod.shape[0]
    wcols = w_mod.shape[2]
    tn = 1536
    return pl.pallas_call(
        _mod_kernel,
        grid=(depth, wcols // tn),
        in_specs=[pl.BlockSpec((MOD_ROWS, D_MODEL), lambda l, j: (0, 0)),
                  pl.BlockSpec((1, D_MODEL, tn), lambda l, j: (l, 0, j)),
                  pl.BlockSpec((1, 1, tn), lambda l, j: (l, 0, j))],
        out_specs=pl.BlockSpec((1, MOD_ROWS, tn), lambda l, j: (l, 0, j)),
        out_shape=jax.ShapeDtypeStruct((depth, MOD_ROWS, wcols), F32),
        compiler_params=_params(("parallel", "parallel")),
        name="modulation",
    )(cvec, w_mod, b_mod.reshape(depth, 1, wcols))


def _mod_spec(layer, chunk, ctx, ngrid):
    def idx(*g):
        row = CTX_MOD_ROW if ctx else g[0]
        return (layer * MOD_ROWS + row, 0, chunk)
    del ngrid
    return pl.BlockSpec((1, 1, D_MODEL), idx)


def _inproj_kernel(x_ref, g_ref, sh_ref, sc_ref, w_ref, cos_ref, sa_ref, sb_ref, o_ref, h_scr,
                   *, rope, col_start):
    j = pl.program_id(2)

    @pl.when(j == 0)
    def _():
        n = _rms(x_ref[0], g_ref[...])
        h_scr[...] = (n * (1.0 + sc_ref[0]) + sh_ref[0]).astype(BF16)

    z = jnp.dot(h_scr[...], w_ref[...], preferred_element_type=F32)
    if not rope:
        o_ref[0] = z.astype(BF16)
        return

    is_qk = jnp.logical_or(j + col_start == Q_COL, j + col_start == K_COL)

    @pl.when(is_qk)
    def _():
        cos, sa, sb = cos_ref[...], sa_ref[...], sb_ref[...]
        for c in range(COL_BLK // LANES):
            t = z[:, c * LANES:(c + 1) * LANES]
            r = (t * cos + pltpu.roll(t, LANES - ROPE_FREQS, 1) * sa
                 + pltpu.roll(t, ROPE_FREQS, 1) * sb)
            o_ref[0, :, c * LANES:(c + 1) * LANES] = r.astype(BF16)

    @pl.when(jnp.logical_not(is_qk))
    def _():
        o_ref[0] = z.astype(BF16)


def _inproj(x, g, mod, w, tables, *, layer, ctx, rope, col_start, ncols, tm):
    bt, L, _ = x.shape
    cos, sa, sb = tables
    grid = (bt, L // tm, ncols)
    tab_spec = pl.BlockSpec((tm, LANES), lambda b, i, j: (i, 0))
    return pl.pallas_call(
        functools.partial(_inproj_kernel, rope=rope, col_start=col_start),
        grid=grid,
        in_specs=[pl.BlockSpec((1, tm, D_MODEL), lambda b, i, j: (b, i, 0)),
                  pl.BlockSpec((1, D_MODEL), lambda b, i, j: (0, 0)),
                  _mod_spec(layer, 0, ctx, 3),
                  _mod_spec(layer, 1, ctx, 3),
                  pl.BlockSpec((D_MODEL, COL_BLK), lambda b, i, j: (0, j + col_start)),
                  tab_spec, tab_spec, tab_spec],
        out_specs=pl.BlockSpec((1, tm, COL_BLK), lambda b, i, j: (b, i, j)),
        out_shape=jax.ShapeDtypeStruct((bt, L, ncols * COL_BLK), BF16),
        scratch_shapes=[pltpu.VMEM((tm, D_MODEL), BF16)],
        compiler_params=_params(("parallel", "parallel", "arbitrary")),
        name="inproj",
    )(x, g, mod, mod, w, cos, sa, sb)


def _attn_kernel(*refs, nseg, lam_init):
    q_ref, lam_ref, gs_ref = refs[0], refs[1], refs[2]
    kv_refs = refs[3:3 + 2 * nseg]
    o_ref = refs[3 + 2 * nseg]

    lv = lam_ref[...]
    lam = (jnp.exp(jnp.sum(lv[0:1] * lv[1:2], axis=-1, keepdims=True))
           - jnp.exp(jnp.sum(lv[2:3] * lv[3:4], axis=-1, keepdims=True)) + lam_init)

    q = q_ref[0]
    lane = lax.broadcasted_iota(jnp.int32, q.shape, 1)
    qs = (q.astype(F32) * (HEAD_QK ** -0.5)).astype(BF16)
    q1 = jnp.where(lane < HEAD_QK, qs, jnp.zeros_like(qs))
    q2 = jnp.where(lane >= HEAD_QK, qs, jnp.zeros_like(qs))

    dn = (((1,), (1,)), ((), ()))
    s1 = [lax.dot_general(q1, kv_refs[2 * s][0], dn, preferred_element_type=F32) for s in range(nseg)]
    s2 = [lax.dot_general(q2, kv_refs[2 * s][0], dn, preferred_element_type=F32) for s in range(nseg)]

    def softmax_parts(ss):
        m = functools.reduce(jnp.maximum, [jnp.max(s, axis=-1, keepdims=True) for s in ss])
        es = [jnp.exp(s - m) for s in ss]
        l = functools.reduce(jnp.add, [jnp.sum(e, axis=-1, keepdims=True) for e in es])
        return es, l

    e1, l1 = softmax_parts(s1)
    e2, l2 = softmax_parts(s2)
    r1 = 1.0 / l1
    r2 = lam / l2
    o = None
    for s in range(nseg):
        a = (e1[s] * r1 - e2[s] * r2).astype(BF16)
        part = jnp.dot(a, kv_refs[2 * s + 1][0], preferred_element_type=F32)
        o = part if o is None else o + part
    o_ref[0] = (_rms(o, gs_ref[...]) * (1.0 - lam_init)).astype(BF16)


def _attention(zq, segs, lam_vecs, g_sub, *, lam_init, tq):
    bt, lq, _ = zq.shape
    q0 = Q_COL * HEAD_BLKS
    in_specs = [pl.BlockSpec((1, tq, HEAD_V), lambda b, h, i: (b, i, q0 + h)),
                pl.BlockSpec((4, HEAD_QK), lambda b, h, i: (0, 0)),
                pl.BlockSpec((1, HEAD_V), lambda b, h, i: (0, 0))]
    args = [zq, lam_vecs, g_sub]
    for arr, kc, vc in segs:
        t = arr.shape[1]
        in_specs.append(pl.BlockSpec((1, t, HEAD_V), lambda b, h, i, kc=kc: (b, 0, kc + h)))
        in_specs.append(pl.BlockSpec((1, t, HEAD_V), lambda b, h, i, vc=vc: (b, 0, vc + h)))
        args += [arr, arr]
    return pl.pallas_call(
        functools.partial(_attn_kernel, nseg=len(segs), lam_init=lam_init),
        grid=(bt, N_HEADS, lq // tq),
        in_specs=in_specs,
        out_specs=pl.BlockSpec((1, tq, HEAD_V), lambda b, h, i: (b, i, h)),
        out_shape=jax.ShapeDtypeStruct((bt, lq, N_HEADS * HEAD_V), BF16),
        compiler_params=_params(("parallel", "parallel", "arbitrary")),
        name="diff_attention",
    )(*args)


def _merge_kernel(attn_ref, cb_ref, cc_ref, cx_ref, pin_ref, g0_ref, g1_ref, g2_ref,
                  ccp_ref, cxp_ref, pinp_ref, ccn_ref, cxn_ref, pinn_ref,
                  x_ref, gate_ref, gpost_ref, cw_ref, pw_ref, ps_ref, wb_ref, wo_ref,
                  o_ref, u_scr, p_scr, *, tm, seq_len):
    i = pl.program_id(1)
    first = i == 0
    last = i == pl.num_programs(1) - 1

    def f32(ref):
        return ref[0].astype(F32)

    u_scr[0:HALO] = jnp.where(first, 0.0, f32(ccp_ref) * f32(cxp_ref))
    u_scr[HALO:HALO + tm] = f32(cc_ref) * f32(cx_ref)
    u_scr[HALO + tm:2 * HALO + tm] = jnp.where(last, 0.0, f32(ccn_ref) * f32(cxn_ref))
    p_scr[0:HALO] = jnp.where(first, 0.0, f32(pinp_ref))
    p_scr[HALO:HALO + tm] = f32(pin_ref)
    p_scr[HALO + tm:2 * HALO + tm] = jnp.where(last, 0.0, f32(pinn_ref))

    cw = cw_ref[...]
    conv = (cw[0:1] * u_scr[HALO - 1:HALO - 1 + tm] + cw[1:2] * u_scr[HALO:HALO + tm]
            + cw[2:3] * u_scr[HALO + 1:HALO + 1 + tm])
    conv_o = (f32(cb_ref) * conv).astype(BF16)

    t = i * tm + lax.broadcasted_iota(jnp.int32, (tm, 1), 0)
    pool_parts = []
    for g, w in enumerate(POOL_WINDOWS):
        cs = slice(g * POOL_GW, (g + 1) * POOL_GW)
        acc = p_scr[HALO - w // 2:HALO - w // 2 + tm, cs]
        for d in range(-w // 2 + 1, w // 2):
            acc = acc + p_scr[HALO + d:HALO + d + tm, cs]
        lo = jnp.maximum(t - w // 2, 0)
        hi = jnp.minimum(t - w // 2 + w, seq_len)
        p = acc / (hi - lo).astype(F32) - p_scr[HALO:HALO + tm, cs]
        pool_parts.append(jnp.dot(p.astype(BF16), pw_ref[g], preferred_element_type=F32))
    pool_o = (jnp.concatenate(pool_parts, axis=-1) * ps_ref[...]).astype(BF16)

    merged = (jax.nn.sigmoid(f32(g0_ref)) * jnp.dot(attn_ref[0], wb_ref[0], preferred_element_type=F32)
              + jax.nn.sigmoid(f32(g1_ref)) * jnp.dot(conv_o, wb_ref[1], preferred_element_type=F32)
              + jax.nn.sigmoid(f32(g2_ref)) * jnp.dot(pool_o, wb_ref[2], preferred_element_type=F32))
    mix = jnp.dot(merged.astype(BF16), wo_ref[...], preferred_element_type=F32)
    o_ref[0] = x_ref[0] + gate_ref[0] * _rms(mix, gpost_ref[...])


def _merge(attn, z, x, mod, g_post, conv_w, pool_w, pool_scale, w_branch, w_out, *, layer, ctx, tm):
    bt, L, _ = x.shape
    hb = tm // HALO
    nhb = L // HALO

    def col(c, width=COL_BLK):
        return pl.BlockSpec((1, tm, width), lambda b, i, c=c: (b, i, c))

    def prev(c):
        return pl.BlockSpec((1, HALO, COL_BLK), lambda b, i, c=c: (b, jnp.maximum(i * hb - 1, 0), c))

    def nxt(c):
        return pl.BlockSpec((1, HALO, COL_BLK),
                            lambda b, i, c=c: (b, jnp.minimum((i + 1) * hb, nhb - 1), c))

    def full(a):
        return pl.BlockSpec(a.shape, lambda b, i, n=a.ndim: (0,) * n)

    in_specs = [pl.BlockSpec((1, tm, BRANCH_W), lambda b, i: (b, i, 0)),
                col(CB_COL), col(CC_COL), col(CX_COL), col(PIN_COL)]
    in_specs += [col(k, D_MODEL) for k in range(N_BRANCH)]
    in_specs += [prev(CC_COL), prev(CX_COL), prev(PIN_COL), nxt(CC_COL), nxt(CX_COL), nxt(PIN_COL)]
    in_specs += [pl.BlockSpec((1, tm, D_MODEL), lambda b, i: (b, i, 0)),
                 _mod_spec(layer, 2, ctx, 2),
                 full(g_post), full(conv_w), full(pool_w), full(pool_scale), full(w_branch),
                 full(w_out)]
    zz = z
    return pl.pallas_call(
        functools.partial(_merge_kernel, tm=tm, seq_len=L),
        grid=(bt, L // tm),
        in_specs=in_specs,
        out_specs=pl.BlockSpec((1, tm, D_MODEL), lambda b, i: (b, i, 0)),
        out_shape=jax.ShapeDtypeStruct((bt, L, D_MODEL), F32),
        scratch_shapes=[pltpu.VMEM((tm + 2 * HALO, BRANCH_W), F32),
                        pltpu.VMEM((tm + 2 * HALO, BRANCH_W), F32)],
        compiler_params=_params(("parallel", "arbitrary")),
        name="mixer_merge",
    )(attn, *([zz] * 13), x, mod, g_post, conv_w, pool_w, pool_scale, w_branch, w_out)


def _swiglu(h, wg_ref, wu_ref, wd_ref):
    f = None
    for c in range(D_FF // FF_CHUNK):
        cs = slice(c * FF_CHUNK, (c + 1) * FF_CHUNK)
        gt = jnp.dot(h, wg_ref[0, :, cs], preferred_element_type=F32)
        up = jnp.dot(h, wu_ref[0, :, cs], preferred_element_type=F32)
        a = (gt * jax.nn.sigmoid(gt) * up).astype(BF16)
        part = jnp.dot(a, wd_ref[0, cs, :], preferred_element_type=F32)
        f = part if f is None else f + part
    return f


def _ffn_kernel(x_ref, g_ref, sh_ref, sc_ref, gate_ref, gpost_ref, wg_ref, wu_ref, wd_ref, o_ref):
    h = (_rms(x_ref[0], g_ref[...]) * (1.0 + sc_ref[0]) + sh_ref[0]).astype(BF16)
    f = _swiglu(h, wg_ref, wu_ref, wd_ref)
    o_ref[0] = x_ref[0] + gate_ref[0] * _rms(f, gpost_ref[...])


def _ffn(x, g_pre, g_post, mod, wg, wu, wd, *, layer, ctx, tm):
    bt, L, _ = x.shape
    vec = pl.BlockSpec((1, D_MODEL), lambda b, i: (0, 0))
    return pl.pallas_call(
        _ffn_kernel,
        grid=(bt, L // tm),
        in_specs=[pl.BlockSpec((1, tm, D_MODEL), lambda b, i: (b, i, 0)),
                  vec,
                  _mod_spec(layer, 3, ctx, 2),
                  _mod_spec(layer, 4, ctx, 2),
                  _mod_spec(layer, 5, ctx, 2),
                  vec,
                  pl.BlockSpec((1, D_MODEL, D_FF), lambda b, i: (0, 0, 0)),
                  pl.BlockSpec((1, D_MODEL, D_FF), lambda b, i: (0, 0, 0)),
                  pl.BlockSpec((1, D_FF, D_MODEL), lambda b, i: (0, 0, 0))],
        out_specs=pl.BlockSpec((1, tm, D_MODEL), lambda b, i: (b, i, 0)),
        out_shape=jax.ShapeDtypeStruct((bt, L, D_MODEL), F32),
        compiler_params=_params(("parallel", "parallel")),
        name="channel_mixer",
    )(x, g_pre, mod, mod, mod, g_post, wg, wu, wd)


ROUTE_TM = 512
GROUP_TM = 512
COPY_TB = 1024
TOP_K = 2
RANK_LANE, PROB_LANE, EXPERT_LANE = 0, 2, 4


def _flat_mod_spec(layer, chunk, ctx, tm, seq_len):
    def idx(i):
        row = CTX_MOD_ROW if ctx else (i * tm) // seq_len
        return (layer * MOD_ROWS + row, 0, chunk)
    return pl.BlockSpec((1, 1, D_MODEL), idx)


def _lane_pick(v, lane, k):
    return jnp.sum(jnp.where(lane == k, v, 0.0), axis=-1, keepdims=True)


def _route_kernel(x_ref, g_ref, sh_ref, sc_ref, wr_ref, h_ref, info_ref, cnt_ref, run_scr):
    @pl.when(pl.program_id(0) == 0)
    def _():
        run_scr[...] = jnp.zeros_like(run_scr)

    h = _rms(x_ref[...], g_ref[...]) * (1.0 + sc_ref[0]) + sh_ref[0]
    h_ref[...] = h
    logits = jnp.dot(h, wr_ref[...], preferred_element_type=F32, precision=lax.Precision.HIGHEST)
    lane = lax.broadcasted_iota(jnp.int32, logits.shape, 1)
    neg = jnp.float32(-jnp.inf)
    l1 = jnp.where(lane < N_EXPERTS, logits, neg)
    m1 = jnp.max(l1, axis=-1, keepdims=True)
    i1 = jnp.min(jnp.where(l1 == m1, lane, LANES), axis=-1, keepdims=True)
    l2 = jnp.where(lane == i1, neg, l1)
    m2 = jnp.max(l2, axis=-1, keepdims=True)
    i2 = jnp.min(jnp.where(l2 == m2, lane, LANES), axis=-1, keepdims=True)
    e2 = jnp.exp(m2 - m1)
    den = 1.0 + e2

    chosen = jnp.where(jnp.logical_or(lane == i1, lane == i2), 1.0, 0.0)
    tm = chosen.shape[0]
    earlier = (lax.broadcasted_iota(jnp.int32, (tm, tm), 1)
               < lax.broadcasted_iota(jnp.int32, (tm, tm), 0))
    before = jnp.dot(jnp.where(earlier, 1.0, 0.0).astype(BF16), chosen.astype(BF16),
                     preferred_element_type=F32)
    rank = before + run_scr[...]
    rec = [_lane_pick(rank, lane, i1), _lane_pick(rank, lane, i2), 1.0 / den, e2 / den,
           i1.astype(F32), i2.astype(F32)]
    info = jnp.zeros_like(logits)
    for k, v in enumerate(rec):
        info = jnp.where(lane == k, v, info)
    info_ref[...] = info
    run_scr[...] += jnp.sum(chosen, axis=0, keepdims=True)
    cnt_ref[...] = run_scr[...]


def _route(x2, g, mod, wr_pad, *, layer, ctx, seq_len):
    n = x2.shape[0]
    tm = ROUTE_TM
    vec = pl.BlockSpec((1, D_MODEL), lambda i: (0, 0))
    return pl.pallas_call(
        _route_kernel,
        grid=(n // tm,),
        in_specs=[pl.BlockSpec((tm, D_MODEL), lambda i: (i, 0)),
                  vec,
                  _flat_mod_spec(layer, 3, ctx, tm, seq_len),
                  _flat_mod_spec(layer, 4, ctx, tm, seq_len),
                  pl.BlockSpec((D_MODEL, LANES), lambda i: (0, 0))],
        out_specs=[pl.BlockSpec((tm, D_MODEL), lambda i: (i, 0)),
                   pl.BlockSpec((tm, LANES), lambda i: (i, 0)),
                   pl.BlockSpec((1, LANES), lambda i: (0, 0))],
        out_shape=[jax.ShapeDtypeStruct((n, D_MODEL), F32),
                   jax.ShapeDtypeStruct((n, LANES), F32),
                   jax.ShapeDtypeStruct((1, LANES), F32)],
        scratch_shapes=[pltpu.VMEM((1, LANES), F32)],
        compiler_params=_params(("arbitrary",)),
        name="route",
    )(x2, g, mod, mod, wr_pad)


def _rowcopy_kernel(idx_ref, src_ref, dst_ref, sem):
    def row_copy(s, d):
        return pltpu.make_async_copy(src_ref.at[pl.ds(s, 1)], dst_ref.at[pl.ds(d, 1)], sem)

    def issue(t, carry):
        row_copy(idx_ref[0, 0, t], idx_ref[0, 1, t]).start()
        return carry

    lax.fori_loop(0, COPY_TB, issue, 0, unroll=8)

    def drain(t, carry):
        row_copy(0, 0).wait()
        return carry

    lax.fori_loop(0, COPY_TB, drain, 0, unroll=8)


def _rowcopy(src, idx, n_dst):
    steps = idx.shape[0]
    return pl.pallas_call(
        _rowcopy_kernel,
        grid=(steps,),
        in_specs=[pl.BlockSpec((1, 2, COPY_TB), lambda i: (i, 0, 0), memory_space=pltpu.SMEM),
                  pl.BlockSpec(memory_space=pl.ANY)],
        out_specs=pl.BlockSpec(memory_space=pl.ANY),
        out_shape=jax.ShapeDtypeStruct((n_dst, src.shape[1]), src.dtype),
        scratch_shapes=[pltpu.SemaphoreType.DMA(())],
        compiler_params=_params(("arbitrary",)),
        name="row_permute",
    )(idx, src)


def _group_kernel(vblk_ref, vexp_ref, vlo_ref, vhi_ref, vfirst_ref, vvalid_ref,
                  h_ref, wg_ref, wu_ref, wd_ref, y_ref):
    del vblk_ref, vexp_ref
    v = pl.program_id(0)

    @pl.when(vvalid_ref[v] == 1)
    def _():
        f = _swiglu(h_ref[...].astype(BF16), wg_ref, wu_ref, wd_ref)

        @pl.when(vfirst_ref[v] == 1)
        def _():
            y_ref[...] = f

        @pl.when(vfirst_ref[v] == 0)
        def _():
            row = lax.broadcasted_iota(jnp.int32, (f.shape[0], 1), 0)
            mine = jnp.logical_and(row >= vlo_ref[v], row < vhi_ref[v])
            y_ref[...] = jnp.where(mine, f, y_ref[...])


def _grouped_experts(hs, visits, wg, wu, wd):
    n_rows = hs.shape[0]
    tg = GROUP_TM
    nv = visits[0].shape[0]

    def wspec(shape):
        return pl.BlockSpec((1,) + shape, lambda v, blk, exp, *_: (exp[v], 0, 0))

    row_spec = pl.BlockSpec((tg, D_MODEL), lambda v, blk, *_: (blk[v], 0))
    return pl.pallas_call(
        _group_kernel,
        grid_spec=pltpu.PrefetchScalarGridSpec(
            num_scalar_prefetch=len(visits),
            grid=(nv,),
            in_specs=[row_spec, wspec((D_MODEL, D_FF)), wspec((D_MODEL, D_FF)),
                      wspec((D_FF, D_MODEL))],
            out_specs=row_spec),
        out_shape=jax.ShapeDtypeStruct((n_rows, D_MODEL), F32),
        compiler_params=_params(("arbitrary",)),
        name="grouped_experts",
    )(*visits, hs, wg, wu, wd)


def _visit_tables(counts, n_rows):
    tg = GROUP_TM
    nv = n_rows // tg + N_EXPERTS - 1
    ends = jnp.cumsum(counts)
    starts = ends - counts
    first_tile = starts // tg
    last_tile = jnp.maximum(ends - 1, 0) // tg
    nvis = jnp.where(counts > 0, last_tile - first_tile + 1, 0)
    vend = jnp.cumsum(nvis)
    total = vend[-1]
    v = jnp.minimum(jnp.arange(nv, dtype=jnp.int32), total - 1)
    exp = jnp.sum(v[:, None] >= vend[None, :], axis=1).astype(jnp.int32)
    tile = first_tile[exp] + v - (vend[exp] - nvis[exp])
    lo = jnp.maximum(starts[exp], tile * tg) - tile * tg
    hi = jnp.minimum(ends[exp], (tile + 1) * tg) - tile * tg
    first = jnp.concatenate([jnp.ones((1,), jnp.int32), (tile[1:] != tile[:-1]).astype(jnp.int32)])
    valid = (jnp.arange(nv) < total).astype(jnp.int32)
    return tuple(a.astype(jnp.int32) for a in (tile, exp, lo, hi, first, valid)), starts


def _combine_kernel(x_ref, y1_ref, y2_ref, info_ref, gate_ref, gpost_ref, o_ref):
    info = info_ref[...]
    lane = lax.broadcasted_iota(jnp.int32, info.shape, 1)
    f = (_lane_pick(info, lane, PROB_LANE) * y1_ref[...]
         + _lane_pick(info, lane, PROB_LANE + 1) * y2_ref[...])
    o_ref[...] = x_ref[...] + gate_ref[0] * _rms(f, gpost_ref[...])


def _combine(x2, y12, info, mod, g_post, *, layer, ctx, seq_len):
    n = x2.shape[0]
    tm = ROUTE_TM
    nt = n // tm
    row = pl.BlockSpec((tm, D_MODEL), lambda i: (i, 0))
    return pl.pallas_call(
        _combine_kernel,
        grid=(nt,),
        in_specs=[row, row, pl.BlockSpec((tm, D_MODEL), lambda i: (nt + i, 0)),
                  pl.BlockSpec((tm, LANES), lambda i: (i, 0)),
                  _flat_mod_spec(layer, 5, ctx, tm, seq_len),
                  pl.BlockSpec((1, D_MODEL), lambda i: (0, 0))],
        out_specs=row,
        out_shape=jax.ShapeDtypeStruct((n, D_MODEL), F32),
        compiler_params=_params(("parallel",)),
        name="combine",
    )(x2, y12, y12, info, mod, g_post)


def _copy_index(src_rows, dst_rows):
    return jnp.stack([src_rows.reshape(-1, COPY_TB), dst_rows.reshape(-1, COPY_TB)],
                     axis=1).astype(jnp.int32)


def _moe(x, g_pre, g_post, mod, wr_pad, wg, wu, wd, *, layer, ctx):
    bt, seq_len, _ = x.shape
    n = bt * seq_len
    x2 = x.reshape(n, D_MODEL)
    h, info, cnt = _route(x2, g_pre, mod, wr_pad, layer=layer, ctx=ctx, seq_len=seq_len)

    counts = cnt[0, :N_EXPERTS].astype(jnp.int32)
    visits, starts = _visit_tables(counts, TOP_K * n)
    expert = info[:, EXPERT_LANE:EXPERT_LANE + TOP_K].astype(jnp.int32)
    rank = info[:, RANK_LANE:RANK_LANE + TOP_K].astype(jnp.int32)
    pos = (starts[expert] + rank).T.reshape(-1)
    token = jnp.tile(jnp.arange(n, dtype=jnp.int32), TOP_K)

    hs = _rowcopy(h, _copy_index(token, pos), TOP_K * n)
    ys = _grouped_experts(hs, visits, wg, wu, wd)
    y12 = _rowcopy(ys, _copy_index(pos, jnp.arange(TOP_K * n, dtype=jnp.int32)), TOP_K * n)
    out = _combine(x2, y12, info, mod, g_post, layer=layer, ctx=ctx, seq_len=seq_len)
    return out.reshape(bt, seq_len, D_MODEL)


def _rope_tables(seq_len):
    rows = jnp.repeat(jnp.arange(seq_len // GRID_W), GRID_W).astype(F32)
    cols = jnp.tile(jnp.arange(GRID_W), seq_len // GRID_W).astype(F32)
    inv = ROPE_BASE ** (-jnp.arange(ROPE_FREQS, dtype=F32) / ROPE_FREQS)
    ang = jnp.concatenate([rows[:, None] * inv, rows[:, None] * inv,
                           cols[:, None] * inv, cols[:, None] * inv], axis=1)
    ang = jnp.tile(ang, (1, LANES // HEAD_QK))
    cos, sin = jnp.cos(ang), jnp.sin(ang)
    low = (jnp.arange(LANES) % (2 * ROPE_FREQS)) < ROPE_FREQS
    return cos, jnp.where(low, -sin, 0.0), jnp.where(low, 0.0, sin)


def kernel(x, c, ctx, c_ctx, w_mod, b_mod, g_pre_mix, g_post_mix, g_pre_ffn, g_post_ffn,
           w_in, lambda_q1, lambda_k1, lambda_q2, lambda_k2, g_subln, conv_w, pool_w,
           pool_scale, w_branch, w_out, ffn_w_gate, ffn_w_up, ffn_w_down, router_w,
           moe_w_gate, moe_w_up, moe_w_down):
    depth = w_in.shape[0]
    bsz, seq, _ = x.shape
    ctx_len = ctx.shape[1]

    cvec = jnp.zeros((MOD_ROWS, D_MODEL), F32).at[:bsz].set(c).at[CTX_MOD_ROW].set(c_ctx)
    mod = _modulation(cvec, w_mod, b_mod).reshape(depth * MOD_ROWS, 1, 6 * D_MODEL)
    tables = _rope_tables(seq)
    no_tables = tuple(t[:ctx_len] for t in tables)

    y = ctx
    for i in range(depth):
        last = i == depth - 1
        lam_init = 0.8 - 0.6 * math.exp(-0.3 * i)
        split = w_in.shape[2] - GATES_W
        w_in_i = jnp.concatenate([w_in[i, :, split:], w_in[i, :, :split]], axis=1).astype(BF16)
        lam_vecs = jnp.stack([lambda_q1[i], lambda_k1[i], lambda_q2[i], lambda_k2[i]]).astype(F32)
        g_sub = g_subln[i].reshape(1, HEAD_V)
        g_pm, g_qm = g_pre_mix[i].reshape(1, D_MODEL), g_post_mix[i].reshape(1, D_MODEL)
        g_pf, g_qf = g_pre_ffn[i].reshape(1, D_MODEL), g_post_ffn[i].reshape(1, D_MODEL)
        mix_w = (conv_w[i], pool_w[i].astype(BF16), pool_scale[i].reshape(1, BRANCH_W),
                 w_branch[i].astype(BF16), w_out[i].astype(BF16))

        z = _inproj(x, g_pm, mod, w_in_i, tables, layer=i, ctx=False, rope=True,
                    col_start=0, ncols=IN_W // COL_BLK, tm=1024)
        kv_full = (K_COL * HEAD_BLKS, V_COL * HEAD_BLKS)
        if last:
            zc = _inproj(y, g_pm, mod, w_in_i, no_tables, layer=i, ctx=True, rope=False,
                         col_start=K_COL, ncols=2, tm=ctx_len)
            segs = [(z,) + kv_full, (zc, 0, HEAD_BLKS)]
        else:
            zc = _inproj(y, g_pm, mod, w_in_i, no_tables, layer=i, ctx=True, rope=False,
                         col_start=0, ncols=IN_W // COL_BLK, tm=ctx_len)
            segs = [(z,) + kv_full, (zc,) + kv_full]
            attn_c = _attention(zc, [(zc,) + kv_full], lam_vecs, g_sub, lam_init=lam_init,
                                tq=ctx_len)
            y = _merge(attn_c, zc, y, mod, g_qm, *mix_w, layer=i, ctx=True, tm=ctx_len)
        attn_l = _attention(z, segs, lam_vecs, g_sub, lam_init=lam_init, tq=256)
        x = _merge(attn_l, z, x, mod, g_qm, *mix_w, layer=i, ctx=False, tm=512)

        j = i // 2
        streams = [(x, False, 512)] + ([] if last else [(y, True, ctx_len)])
        if i % 2 == 0:
            ffn_w = (ffn_w_gate[j:j + 1].astype(BF16), ffn_w_up[j:j + 1].astype(BF16),
                     ffn_w_down[j:j + 1].astype(BF16))
            outs = [_ffn(t, g_pf, g_qf, mod, *ffn_w, layer=i, ctx=is_ctx, tm=tm)
                    for t, is_ctx, tm in streams]
        else:
            ffn_w = (moe_w_gate[j].astype(BF16), moe_w_up[j].astype(BF16), moe_w_down[j].astype(BF16))
            wr_pad = jnp.zeros((D_MODEL, LANES), F32).at[:, :N_EXPERTS].set(router_w[j])
            outs = [_moe(t, g_pf, g_qf, mod, wr_pad, *ffn_w, layer=i, ctx=is_ctx)
                    for t, is_ctx, _ in streams]
        x = outs[0]
        if not last:
            y = outs[1]
    return x
```

```python
import functools
import math

import jax
import jax.numpy as jnp
from jax import lax
from jax.experimental import pallas as pl
from jax.experimental.pallas import tpu as pltpu
from jax.experimental.pallas import tpu_sc as plsc

F32 = jnp.float32
BF16 = jnp.bfloat16

D_MODEL = 1024
GRID_W = 64
N_HEADS = 4
HEAD_QK = 64
HEAD_V = 128
ROPE_BASE = 10000.0
ROPE_FREQS = HEAD_QK // 4
CONV_K = 3
POOL_WINDOWS = (2, 4, 8, 16)
POOL_GW = 128
N_BRANCH = 3
BRANCH_W = 512
GATES_W = N_BRANCH * D_MODEL
IN_W = GATES_W + 7 * BRANCH_W
D_FF = 2816
N_EXPERTS = 8
EPS = 1e-6

LANES = 128
BF16_SUBLANES = 16
COL_BLK = 512
Q_COL, K_COL, V_COL, CB_COL, CC_COL, CX_COL, PIN_COL = range(6, 13)
HEAD_BLKS = COL_BLK // 128
MOD_ROWS = 16
CTX_MOD_ROW = 8
HALO = BF16_SUBLANES
FF_CHUNK = 256
VMEM_LIMIT = 56 * 1024 * 1024


def _params(sem, vmem=VMEM_LIMIT):
    return pltpu.CompilerParams(dimension_semantics=sem, vmem_limit_bytes=vmem)


def _rms(t, g):
    return t * lax.rsqrt(jnp.mean(t * t, axis=-1, keepdims=True) + EPS) * g


def _mod_kernel(c_ref, w_ref, b_ref, o_ref):
    c = c_ref[...]
    s = c * jax.nn.sigmoid(c)
    o_ref[0] = jnp.dot(s, w_ref[0], preferred_element_type=F32,
                       precision=lax.Precision.HIGHEST) + b_ref[0]


def _modulation(cvec, w_mod, b_mod):
    depth = w_mod.shape[0]
    wcols = w_mod.shape[2]
    tn = 1536
    return pl.pallas_call(
        _mod_kernel,
        grid=(depth, wcols // tn),
        in_specs=[pl.BlockSpec((MOD_ROWS, D_MODEL), lambda l, j: (0, 0)),
                  pl.BlockSpec((1, D_MODEL, tn), lambda l, j: (l, 0, j)),
                  pl.BlockSpec((1, 1, tn), lambda l, j: (l, 0, j))],
        out_specs=pl.BlockSpec((1, MOD_ROWS, tn), lambda l, j: (l, 0, j)),
        out_shape=jax.ShapeDtypeStruct((depth, MOD_ROWS, wcols), F32),
        compiler_params=_params(("parallel", "parallel")),
        name="modulation",
    )(cvec, w_mod, b_mod.reshape(depth, 1, wcols))


def _mod_spec(layer, chunk, ctx, ngrid):
    def idx(*g):
        row = CTX_MOD_ROW if ctx else g[0]
        return (layer * MOD_ROWS + row, 0, chunk)
    del ngrid
    return pl.BlockSpec((1, 1, D_MODEL), idx)


def _inproj_kernel(x_ref, g_ref, sh_ref, sc_ref, w_ref, cos_ref, sa_ref, sb_ref, o_ref, h_scr,
                   *, rope, col_start):
    j = pl.program_id(2)

    @pl.when(j == 0)
    def _():
        n = _rms(x_ref[0], g_ref[...])
        h_scr[...] = (n * (1.0 + sc_ref[0]) + sh_ref[0]).astype(BF16)

    z = jnp.dot(h_scr[...], w_ref[...], preferred_element_type=F32)
    if not rope:
        o_ref[0] = z.astype(BF16)
        return

    is_qk = jnp.logical_or(j + col_start == Q_COL, j + col_start == K_COL)

    @pl.when(is_qk)
    def _():
        cos, sa, sb = cos_ref[...], sa_ref[...], sb_ref[...]
        for c in range(COL_BLK // LANES):
            t = z[:, c * LANES:(c + 1) * LANES]
            r = (t * cos + pltpu.roll(t, LANES - ROPE_FREQS, 1) * sa
                 + pltpu.roll(t, ROPE_FREQS, 1) * sb)
            o_ref[0, :, c * LANES:(c + 1) * LANES] = r.astype(BF16)

    @pl.when(jnp.logical_not(is_qk))
    def _():
        o_ref[0] = z.astype(BF16)


def _inproj(x, g, mod, w, tables, *, layer, ctx, rope, col_start, ncols, tm):
    bt, L, _ = x.shape
    cos, sa, sb = tables
    grid = (bt, L // tm, ncols)
    tab_spec = pl.BlockSpec((tm, LANES), lambda b, i, j: (i, 0))
    return pl.pallas_call(
        functools.partial(_inproj_kernel, rope=rope, col_start=col_start),
        grid=grid,
        in_specs=[pl.BlockSpec((1, tm, D_MODEL), lambda b, i, j: (b, i, 0)),
                  pl.BlockSpec((1, D_MODEL), lambda b, i, j: (0, 0)),
                  _mod_spec(layer, 0, ctx, 3),
                  _mod_spec(layer, 1, ctx, 3),
                  pl.BlockSpec((D_MODEL, COL_BLK), lambda b, i, j: (0, j + col_start)),
                  tab_spec, tab_spec, tab_spec],
        out_specs=pl.BlockSpec((1, tm, COL_BLK), lambda b, i, j: (b, i, j)),
        out_shape=jax.ShapeDtypeStruct((bt, L, ncols * COL_BLK), BF16),
        scratch_shapes=[pltpu.VMEM((tm, D_MODEL), BF16)],
        compiler_params=_params(("parallel", "parallel", "arbitrary")),
        name="inproj",
    )(x, g, mod, mod, w, cos, sa, sb)


def _attn_kernel(*refs, nseg, lam_init):
    q_ref, lam_ref, gs_ref = refs[0], refs[1], refs[2]
    kv_refs = refs[3:3 + 2 * nseg]
    o_ref = refs[3 + 2 * nseg]

    lv = lam_ref[...]
    lam = (jnp.exp(jnp.sum(lv[0:1] * lv[1:2], axis=-1, keepdims=True))
           - jnp.exp(jnp.sum(lv[2:3] * lv[3:4], axis=-1, keepdims=True)) + lam_init)

    q = q_ref[0]
    lane = lax.broadcasted_iota(jnp.int32, q.shape, 1)
    qs = (q.astype(F32) * (HEAD_QK ** -0.5)).astype(BF16)
    q1 = jnp.where(lane < HEAD_QK, qs, jnp.zeros_like(qs))
    q2 = jnp.where(lane >= HEAD_QK, qs, jnp.zeros_like(qs))

    dn = (((1,), (1,)), ((), ()))
    s1 = [lax.dot_general(q1, kv_refs[2 * s][0], dn, preferred_element_type=F32) for s in range(nseg)]
    s2 = [lax.dot_general(q2, kv_refs[2 * s][0], dn, preferred_element_type=F32) for s in range(nseg)]

    def softmax_parts(ss):
        m = functools.reduce(jnp.maximum, [jnp.max(s, axis=-1, keepdims=True) for s in ss])
        es = [jnp.exp(s - m) for s in ss]
        l = functools.reduce(jnp.add, [jnp.sum(e, axis=-1, keepdims=True) for e in es])
        return es, l

    e1, l1 = softmax_parts(s1)
    e2, l2 = softmax_parts(s2)
    r1 = 1.0 / l1
    r2 = lam / l2
    o = None
    for s in range(nseg):
        a = (e1[s] * r1 - e2[s] * r2).astype(BF16)
        part = jnp.dot(a, kv_refs[2 * s + 1][0], preferred_element_type=F32)
        o = part if o is None else o + part
    o_ref[0] = (_rms(o, gs_ref[...]) * (1.0 - lam_init)).astype(BF16)


def _attention(zq, segs, lam_vecs, g_sub, *, lam_init, tq):
    bt, lq, _ = zq.shape
    q0 = Q_COL * HEAD_BLKS
    in_specs = [pl.BlockSpec((1, tq, HEAD_V), lambda b, h, i: (b, i, q0 + h)),
                pl.BlockSpec((4, HEAD_QK), lambda b, h, i: (0, 0)),
                pl.BlockSpec((1, HEAD_V), lambda b, h, i: (0, 0))]
    args = [zq, lam_vecs, g_sub]
    for arr, kc, vc in segs:
        t = arr.shape[1]
        in_specs.append(pl.BlockSpec((1, t, HEAD_V), lambda b, h, i, kc=kc: (b, 0, kc + h)))
        in_specs.append(pl.BlockSpec((1, t, HEAD_V), lambda b, h, i, vc=vc: (b, 0, vc + h)))
        args += [arr, arr]
    return pl.pallas_call(
        functools.partial(_attn_kernel, nseg=len(segs), lam_init=lam_init),
        grid=(bt, N_HEADS, lq // tq),
        in_specs=in_specs,
        out_specs=pl.BlockSpec((1, tq, HEAD_V), lambda b, h, i: (b, i, h)),
        out_shape=jax.ShapeDtypeStruct((bt, lq, N_HEADS * HEAD_V), BF16),
        compiler_params=_params(("parallel", "parallel", "arbitrary")),
        name="diff_attention",
    )(*args)


def _merge_kernel(attn_ref, cb_ref, cc_ref, cx_ref, pin_ref, g0_ref, g1_ref, g2_ref,
                  ccp_ref, cxp_ref, pinp_ref, ccn_ref, cxn_ref, pinn_ref,
                  x_ref, gate_ref, gpost_ref, cw_ref, pw_ref, ps_ref, wb_ref, wo_ref,
                  o_ref, u_scr, p_scr, *, tm, seq_len):
    i = pl.program_id(1)
    first = i == 0
    last = i == pl.num_programs(1) - 1

    def f32(ref):
        return ref[0].astype(F32)

    u_scr[0:HALO] = jnp.where(first, 0.0, f32(ccp_ref) * f32(cxp_ref))
    u_scr[HALO:HALO + tm] = f32(cc_ref) * f32(cx_ref)
    u_scr[HALO + tm:2 * HALO + tm] = jnp.where(last, 0.0, f32(ccn_ref) * f32(cxn_ref))
    p_scr[0:HALO] = jnp.where(first, 0.0, f32(pinp_ref))
    p_scr[HALO:HALO + tm] = f32(pin_ref)
    p_scr[HALO + tm:2 * HALO + tm] = jnp.where(last, 0.0, f32(pinn_ref))

    cw = cw_ref[...]
    conv = (cw[0:1] * u_scr[HALO - 1:HALO - 1 + tm] + cw[1:2] * u_scr[HALO:HALO + tm]
            + cw[2:3] * u_scr[HALO + 1:HALO + 1 + tm])
    conv_o = (f32(cb_ref) * conv).astype(BF16)

    t = i * tm + lax.broadcasted_iota(jnp.int32, (tm, 1), 0)
    pool_parts = []
    for g, w in enumerate(POOL_WINDOWS):
        cs = slice(g * POOL_GW, (g + 1) * POOL_GW)
        acc = p_scr[HALO - w // 2:HALO - w // 2 + tm, cs]
        for d in range(-w // 2 + 1, w // 2):
            acc = acc + p_scr[HALO + d:HALO + d + tm, cs]
        lo = jnp.maximum(t - w // 2, 0)
        hi = jnp.minimum(t - w // 2 + w, seq_len)
        p = acc / (hi - lo).astype(F32) - p_scr[HALO:HALO + tm, cs]
        pool_parts.append(jnp.dot(p.astype(BF16), pw_ref[g], preferred_element_type=F32))
    pool_o = (jnp.concatenate(pool_parts, axis=-1) * ps_ref[...]).astype(BF16)

    merged = (jax.nn.sigmoid(f32(g0_ref)) * jnp.dot(attn_ref[0], wb_ref[0], preferred_element_type=F32)
              + jax.nn.sigmoid(f32(g1_ref)) * jnp.dot(conv_o, wb_ref[1], preferred_element_type=F32)
              + jax.nn.sigmoid(f32(g2_ref)) * jnp.dot(pool_o, wb_ref[2], preferred_element_type=F32))
    mix = jnp.dot(merged.astype(BF16), wo_ref[...], preferred_element_type=F32)
    o_ref[0] = x_ref[0] + gate_ref[0] * _rms(mix, gpost_ref[...])


def _merge(attn, z, x, mod, g_post, conv_w, pool_w, pool_scale, w_branch, w_out, *, layer, ctx, tm):
    bt, L, _ = x.shape
    hb = tm // HALO
    nhb = L // HALO

    def col(c, width=COL_BLK):
        return pl.BlockSpec((1, tm, width), lambda b, i, c=c: (b, i, c))

    def prev(c):
        return pl.BlockSpec((1, HALO, COL_BLK), lambda b, i, c=c: (b, jnp.maximum(i * hb - 1, 0), c))

    def nxt(c):
        return pl.BlockSpec((1, HALO, COL_BLK),
                            lambda b, i, c=c: (b, jnp.minimum((i + 1) * hb, nhb - 1), c))

    def full(a):
        return pl.BlockSpec(a.shape, lambda b, i, n=a.ndim: (0,) * n)

    in_specs = [pl.BlockSpec((1, tm, BRANCH_W), lambda b, i: (b, i, 0)),
                col(CB_COL), col(CC_COL), col(CX_COL), col(PIN_COL)]
    in_specs += [col(k, D_MODEL) for k in range(N_BRANCH)]
    in_specs += [prev(CC_COL), prev(CX_COL), prev(PIN_COL), nxt(CC_COL), nxt(CX_COL), nxt(PIN_COL)]
    in_specs += [pl.BlockSpec((1, tm, D_MODEL), lambda b, i: (b, i, 0)),
                 _mod_spec(layer, 2, ctx, 2),
                 full(g_post), full(conv_w), full(pool_w), full(pool_scale), full(w_branch),
                 full(w_out)]
    zz = z
    return pl.pallas_call(
        functools.partial(_merge_kernel, tm=tm, seq_len=L),
        grid=(bt, L // tm),
        in_specs=in_specs,
        out_specs=pl.BlockSpec((1, tm, D_MODEL), lambda b, i: (b, i, 0)),
        out_shape=jax.ShapeDtypeStruct((bt, L, D_MODEL), F32),
        scratch_shapes=[pltpu.VMEM((tm + 2 * HALO, BRANCH_W), F32),
                        pltpu.VMEM((tm + 2 * HALO, BRANCH_W), F32)],
        compiler_params=_params(("parallel", "arbitrary")),
        name="mixer_merge",
    )(attn, *([zz] * 13), x, mod, g_post, conv_w, pool_w, pool_scale, w_branch, w_out)


def _swiglu(h, wg_ref, wu_ref, wd_ref):
    f = None
    for c in range(D_FF // FF_CHUNK):
        cs = slice(c * FF_CHUNK, (c + 1) * FF_CHUNK)
        gt = jnp.dot(h, wg_ref[0, :, cs], preferred_element_type=F32)
        up = jnp.dot(h, wu_ref[0, :, cs], preferred_element_type=F32)
        a = (gt * jax.nn.sigmoid(gt) * up).astype(BF16)
        part = jnp.dot(a, wd_ref[0, cs, :], preferred_element_type=F32)
        f = part if f is None else f + part
    return f


def _ffn_kernel(x_ref, g_ref, sh_ref, sc_ref, gate_ref, gpost_ref, wg_ref, wu_ref, wd_ref, o_ref):
    h = (_rms(x_ref[0], g_ref[...]) * (1.0 + sc_ref[0]) + sh_ref[0]).astype(BF16)
    f = _swiglu(h, wg_ref, wu_ref, wd_ref)
    o_ref[0] = x_ref[0] + gate_ref[0] * _rms(f, gpost_ref[...])


def _ffn(x, g_pre, g_post, mod, wg, wu, wd, *, layer, ctx, tm):
    bt, L, _ = x.shape
    vec = pl.BlockSpec((1, D_MODEL), lambda b, i: (0, 0))
    return pl.pallas_call(
        _ffn_kernel,
        grid=(bt, L // tm),
        in_specs=[pl.BlockSpec((1, tm, D_MODEL), lambda b, i: (b, i, 0)),
                  vec,
                  _mod_spec(layer, 3, ctx, 2),
                  _mod_spec(layer, 4, ctx, 2),
                  _mod_spec(layer, 5, ctx, 2),
                  vec,
                  pl.BlockSpec((1, D_MODEL, D_FF), lambda b, i: (0, 0, 0)),
                  pl.BlockSpec((1, D_MODEL, D_FF), lambda b, i: (0, 0, 0)),
                  pl.BlockSpec((1, D_FF, D_MODEL), lambda b, i: (0, 0, 0))],
        out_specs=pl.BlockSpec((1, tm, D_MODEL), lambda b, i: (b, i, 0)),
        out_shape=jax.ShapeDtypeStruct((bt, L, D_MODEL), F32),
        compiler_params=_params(("parallel", "parallel")),
        name="channel_mixer",
    )(x, g_pre, mod, mod, mod, g_post, wg, wu, wd)


ROUTE_TM = 512
GROUP_TM = 512
TOP_K = 2
ROW_TILE = (8, LANES)
SC_CORES, SC_SUBCORES = 2, 16
SC_WORKERS = SC_CORES * SC_SUBCORES
SC_CHUNK = 32
RANK_LANE, PROB_LANE, EXPERT_LANE = 0, 2, 4


def _flat_mod_spec(layer, chunk, ctx, tm, seq_len):
    def idx(i):
        row = CTX_MOD_ROW if ctx else (i * tm) // seq_len
        return (layer * MOD_ROWS + row, 0, chunk)
    return pl.BlockSpec((1, 1, D_MODEL), idx)


def _lane_pick(v, lane, k):
    return jnp.sum(jnp.where(lane == k, v, 0.0), axis=-1, keepdims=True)


def _route_kernel(x_ref, g_ref, sh_ref, sc_ref, wr_ref, h_ref, info_ref, cnt_ref, run_scr):
    @pl.when(pl.program_id(0) == 0)
    def _():
        run_scr[...] = jnp.zeros_like(run_scr)

    h = _rms(x_ref[...], g_ref[...]) * (1.0 + sc_ref[0]) + sh_ref[0]
    h_ref[...] = h.reshape(h_ref.shape)
    logits = jnp.dot(h, wr_ref[...], preferred_element_type=F32, precision=lax.Precision.HIGHEST)
    lane = lax.broadcasted_iota(jnp.int32, logits.shape, 1)
    neg = jnp.float32(-jnp.inf)
    l1 = jnp.where(lane < N_EXPERTS, logits, neg)
    m1 = jnp.max(l1, axis=-1, keepdims=True)
    i1 = jnp.min(jnp.where(l1 == m1, lane, LANES), axis=-1, keepdims=True)
    l2 = jnp.where(lane == i1, neg, l1)
    m2 = jnp.max(l2, axis=-1, keepdims=True)
    i2 = jnp.min(jnp.where(l2 == m2, lane, LANES), axis=-1, keepdims=True)
    e2 = jnp.exp(m2 - m1)
    den = 1.0 + e2

    chosen = jnp.where(jnp.logical_or(lane == i1, lane == i2), 1.0, 0.0)
    tm = chosen.shape[0]
    earlier = (lax.broadcasted_iota(jnp.int32, (tm, tm), 1)
               < lax.broadcasted_iota(jnp.int32, (tm, tm), 0))
    before = jnp.dot(jnp.where(earlier, 1.0, 0.0).astype(BF16), chosen.astype(BF16),
                     preferred_element_type=F32)
    rank = before + run_scr[...]
    rec = [_lane_pick(rank, lane, i1), _lane_pick(rank, lane, i2), 1.0 / den, e2 / den,
           i1.astype(F32), i2.astype(F32)]
    info = jnp.zeros_like(logits)
    for k, v in enumerate(rec):
        info = jnp.where(lane == k, v, info)
    info_ref[...] = info
    run_scr[...] += jnp.sum(chosen, axis=0, keepdims=True)
    cnt_ref[...] = run_scr[...]


def _route(x2, g, mod, wr_pad, *, layer, ctx, seq_len):
    n = x2.shape[0]
    tm = ROUTE_TM
    vec = pl.BlockSpec((1, D_MODEL), lambda i: (0, 0))
    return pl.pallas_call(
        _route_kernel,
        grid=(n // tm,),
        in_specs=[pl.BlockSpec((tm, D_MODEL), lambda i: (i, 0)),
                  vec,
                  _flat_mod_spec(layer, 3, ctx, tm, seq_len),
                  _flat_mod_spec(layer, 4, ctx, tm, seq_len),
                  pl.BlockSpec((D_MODEL, LANES), lambda i: (0, 0))],
        out_specs=[pl.BlockSpec((tm,) + ROW_TILE, lambda i: (i, 0, 0)),
                   pl.BlockSpec((tm, LANES), lambda i: (i, 0)),
                   pl.BlockSpec((1, LANES), lambda i: (0, 0))],
        out_shape=[jax.ShapeDtypeStruct((n,) + ROW_TILE, F32),
                   jax.ShapeDtypeStruct((n, LANES), F32),
                   jax.ShapeDtypeStruct((1, LANES), F32)],
        scratch_shapes=[pltpu.VMEM((1, LANES), F32)],
        compiler_params=_params(("arbitrary",)),
        name="route",
    )(x2, g, mod, mod, wr_pad)


def _sc_gather(table, idx):
    n_out = idx.shape[0]
    per_worker = n_out // SC_WORKERS
    n_chunks = per_worker // SC_CHUNK
    mesh = plsc.VectorSubcoreMesh(core_axis_name="c", subcore_axis_name="s")

    @functools.partial(
        pl.kernel, mesh=mesh,
        out_type=jax.ShapeDtypeStruct((n_out,) + table.shape[1:], table.dtype),
        scratch_types=[pltpu.VMEM((per_worker,), jnp.int32),
                       pltpu.VMEM((SC_CHUNK,) + table.shape[1:], table.dtype),
                       pltpu.SemaphoreType.DMA],
        name="sc_row_gather")
    def gather(table_hbm, idx_hbm, out_hbm, idx_v, rows_v, sem):
        wid = lax.axis_index("s") * SC_CORES + lax.axis_index("c")
        base = wid * per_worker
        pltpu.sync_copy(idx_hbm.at[pl.ds(base, per_worker)], idx_v)

        @pl.loop(0, n_chunks)
        def _(c):
            off = c * SC_CHUNK
            pltpu.async_copy(table_hbm.at[idx_v.at[pl.ds(off, SC_CHUNK)]], rows_v, sem).wait()
            pltpu.sync_copy(rows_v, out_hbm.at[pl.ds(base + off, SC_CHUNK)])

    return gather(table, idx)


def _group_kernel(vblk_ref, vexp_ref, vlo_ref, vhi_ref, vfirst_ref, vvalid_ref,
                  h_ref, wg_ref, wu_ref, wd_ref, y_ref):
    del vblk_ref, vexp_ref
    v = pl.program_id(0)

    @pl.when(vvalid_ref[v] == 1)
    def _():
        tg = h_ref.shape[0]
        h = h_ref[...].reshape(tg, D_MODEL).astype(BF16)
        f = _swiglu(h, wg_ref, wu_ref, wd_ref).reshape(y_ref.shape)

        @pl.when(vfirst_ref[v] == 1)
        def _():
            y_ref[...] = f

        @pl.when(vfirst_ref[v] == 0)
        def _():
            row = lax.broadcasted_iota(jnp.int32, (tg, 1, 1), 0)
            mine = jnp.logical_and(row >= vlo_ref[v], row < vhi_ref[v])
            y_ref[...] = jnp.where(mine, f, y_ref[...])


def _grouped_experts(hs, visits, wg, wu, wd):
    n_rows = hs.shape[0]
    tg = GROUP_TM
    nv = visits[0].shape[0]

    def wspec(shape):
        return pl.BlockSpec((1,) + shape, lambda v, blk, exp, *_: (exp[v], 0, 0))

    row_spec = pl.BlockSpec((tg,) + ROW_TILE, lambda v, blk, *_: (blk[v], 0, 0))
    return pl.pallas_call(
        _group_kernel,
        grid_spec=pltpu.PrefetchScalarGridSpec(
            num_scalar_prefetch=len(visits),
            grid=(nv,),
            in_specs=[row_spec, wspec((D_MODEL, D_FF)), wspec((D_MODEL, D_FF)),
                      wspec((D_FF, D_MODEL))],
            out_specs=row_spec),
        out_shape=jax.ShapeDtypeStruct((n_rows,) + ROW_TILE, F32),
        compiler_params=_params(("arbitrary",)),
        name="grouped_experts",
    )(*visits, hs, wg, wu, wd)


def _visit_tables(counts, n_rows):
    tg = GROUP_TM
    nv = n_rows // tg + N_EXPERTS - 1
    ends = jnp.cumsum(counts)
    starts = ends - counts
    first_tile = starts // tg
    last_tile = jnp.maximum(ends - 1, 0) // tg
    nvis = jnp.where(counts > 0, last_tile - first_tile + 1, 0)
    vend = jnp.cumsum(nvis)
    total = vend[-1]
    v = jnp.minimum(jnp.arange(nv, dtype=jnp.int32), total - 1)
    exp = jnp.sum(v[:, None] >= vend[None, :], axis=1).astype(jnp.int32)
    tile = first_tile[exp] + v - (vend[exp] - nvis[exp])
    lo = jnp.maximum(starts[exp], tile * tg) - tile * tg
    hi = jnp.minimum(ends[exp], (tile + 1) * tg) - tile * tg
    first = jnp.concatenate([jnp.ones((1,), jnp.int32), (tile[1:] != tile[:-1]).astype(jnp.int32)])
    valid = (jnp.arange(nv) < total).astype(jnp.int32)
    return tuple(a.astype(jnp.int32) for a in (tile, exp, lo, hi, first, valid)), starts


def _combine_kernel(x_ref, y1_ref, y2_ref, info_ref, gate_ref, gpost_ref, o_ref):
    info = info_ref[...]
    lane = lax.broadcasted_iota(jnp.int32, info.shape, 1)
    f = (_lane_pick(info, lane, PROB_LANE) * y1_ref[...].reshape(x_ref.shape)
         + _lane_pick(info, lane, PROB_LANE + 1) * y2_ref[...].reshape(x_ref.shape))
    o_ref[...] = x_ref[...] + gate_ref[0] * _rms(f, gpost_ref[...])


def _combine(x2, y12, info, mod, g_post, *, layer, ctx, seq_len):
    n = x2.shape[0]
    tm = ROUTE_TM
    nt = n // tm
    row = pl.BlockSpec((tm, D_MODEL), lambda i: (i, 0))
    return pl.pallas_call(
        _combine_kernel,
        grid=(nt,),
        in_specs=[row, pl.BlockSpec((tm,) + ROW_TILE, lambda i: (i, 0, 0)),
                  pl.BlockSpec((tm,) + ROW_TILE, lambda i: (nt + i, 0, 0)),
                  pl.BlockSpec((tm, LANES), lambda i: (i, 0)),
                  _flat_mod_spec(layer, 5, ctx, tm, seq_len),
                  pl.BlockSpec((1, D_MODEL), lambda i: (0, 0))],
        out_specs=row,
        out_shape=jax.ShapeDtypeStruct((n, D_MODEL), F32),
        compiler_params=_params(("parallel",)),
        name="combine",
    )(x2, y12, y12, info, mod, g_post)


def _moe(x, g_pre, g_post, mod, wr_pad, wg, wu, wd, *, layer, ctx):
    bt, seq_len, _ = x.shape
    n = bt * seq_len
    x2 = x.reshape(n, D_MODEL)
    h, info, cnt = _route(x2, g_pre, mod, wr_pad, layer=layer, ctx=ctx, seq_len=seq_len)

    counts = cnt[0, :N_EXPERTS].astype(jnp.int32)
    visits, starts = _visit_tables(counts, TOP_K * n)
    expert = info[:, EXPERT_LANE:EXPERT_LANE + TOP_K].astype(jnp.int32)
    rank = info[:, RANK_LANE:RANK_LANE + TOP_K].astype(jnp.int32)
    pos = (starts[expert] + rank).T.reshape(-1)
    token = jnp.tile(jnp.arange(n, dtype=jnp.int32), TOP_K)
    source = jnp.zeros((TOP_K * n,), jnp.int32).at[pos].set(token, unique_indices=True)

    hs = _sc_gather(h, source)
    ys = _grouped_experts(hs, visits, wg, wu, wd)
    y12 = _sc_gather(ys, pos)
    out = _combine(x2, y12, info, mod, g_post, layer=layer, ctx=ctx, seq_len=seq_len)
    return out.reshape(bt, seq_len, D_MODEL)


def _rope_tables(seq_len):
    rows = jnp.repeat(jnp.arange(seq_len // GRID_W), GRID_W).astype(F32)
    cols = jnp.tile(jnp.arange(GRID_W), seq_len // GRID_W).astype(F32)
    inv = ROPE_BASE ** (-jnp.arange(ROPE_FREQS, dtype=F32) / ROPE_FREQS)
    ang = jnp.concatenate([rows[:, None] * inv, rows[:, None] * inv,
                           cols[:, None] * inv, cols[:, None] * inv], axis=1)
    ang = jnp.tile(ang, (1, LANES // HEAD_QK))
    cos, sin = jnp.cos(ang), jnp.sin(ang)
    low = (jnp.arange(LANES) % (2 * ROPE_FREQS)) < ROPE_FREQS
    return cos, jnp.where(low, -sin, 0.0), jnp.where(low, 0.0, sin)


def kernel(x, c, ctx, c_ctx, w_mod, b_mod, g_pre_mix, g_post_mix, g_pre_ffn, g_post_ffn,
           w_in, lambda_q1, lambda_k1, lambda_q2, lambda_k2, g_subln, conv_w, pool_w,
           pool_scale, w_branch, w_out, ffn_w_gate, ffn_w_up, ffn_w_down, router_w,
           moe_w_gate, moe_w_up, moe_w_down):
    depth = w_in.shape[0]
    bsz, seq, _ = x.shape
    ctx_len = ctx.shape[1]

    cvec = jnp.zeros((MOD_ROWS, D_MODEL), F32).at[:bsz].set(c).at[CTX_MOD_ROW].set(c_ctx)
    mod = _modulation(cvec, w_mod, b_mod).reshape(depth * MOD_ROWS, 1, 6 * D_MODEL)
    tables = _rope_tables(seq)
    no_tables = tuple(t[:ctx_len] for t in tables)

    y = ctx
    for i in range(depth):
        last = i == depth - 1
        lam_init = 0.8 - 0.6 * math.exp(-0.3 * i)
        split = w_in.shape[2] - GATES_W
        w_in_i = jnp.concatenate([w_in[i, :, split:], w_in[i, :, :split]], axis=1).astype(BF16)
        lam_vecs = jnp.stack([lambda_q1[i], lambda_k1[i], lambda_q2[i], lambda_k2[i]]).astype(F32)
        g_sub = g_subln[i].reshape(1, HEAD_V)
        g_pm, g_qm = g_pre_mix[i].reshape(1, D_MODEL), g_post_mix[i].reshape(1, D_MODEL)
        g_pf, g_qf = g_pre_ffn[i].reshape(1, D_MODEL), g_post_ffn[i].reshape(1, D_MODEL)
        mix_w = (conv_w[i], pool_w[i].astype(BF16), pool_scale[i].reshape(1, BRANCH_W),
                 w_branch[i].astype(BF16), w_out[i].astype(BF16))

        z = _inproj(x, g_pm, mod, w_in_i, tables, layer=i, ctx=False, rope=True,
                    col_start=0, ncols=IN_W // COL_BLK, tm=1024)
        kv_full = (K_COL * HEAD_BLKS, V_COL * HEAD_BLKS)
        if last:
            zc = _inproj(y, g_pm, mod, w_in_i, no_tables, layer=i, ctx=True, rope=False,
                         col_start=K_COL, ncols=2, tm=ctx_len)
            segs = [(z,) + kv_full, (zc, 0, HEAD_BLKS)]
        else:
            zc = _inproj(y, g_pm, mod, w_in_i, no_tables, layer=i, ctx=True, rope=False,
                         col_start=0, ncols=IN_W // COL_BLK, tm=ctx_len)
            segs = [(z,) + kv_full, (zc,) + kv_full]
            attn_c = _attention(zc, [(zc,) + kv_full], lam_vecs, g_sub, lam_init=lam_init,
                                tq=ctx_len)
            y = _merge(attn_c, zc, y, mod, g_qm, *mix_w, layer=i, ctx=True, tm=ctx_len)
        attn_l = _attention(z, segs, lam_vecs, g_sub, lam_init=lam_init, tq=256)
        x = _merge(attn_l, z, x, mod, g_qm, *mix_w, layer=i, ctx=False, tm=512)

        j = i // 2
        streams = [(x, False, 512)] + ([] if last else [(y, True, ctx_len)])
        if i % 2 == 0:
            ffn_w = (ffn_w_gate[j:j + 1].astype(BF16), ffn_w_up[j:j + 1].astype(BF16),
                     ffn_w_down[j:j + 1].astype(BF16))
            outs = [_ffn(t, g_pf, g_qf, mod, *ffn_w, layer=i, ctx=is_ctx, tm=tm)
                    for t, is_ctx, tm in streams]
        else:
            ffn_w = (moe_w_gate[j].astype(BF16), moe_w_up[j].astype(BF16), moe_w_down[j].astype(BF16))
            wr_pad = jnp.zeros((D_MODEL, LANES), F32).at[:, :N_EXPERTS].set(router_w[j])
            outs = [_moe(t, g_pf, g_qf, mod, wr_pad, *ffn_w, layer=i, ctx=is_ctx)
                    for t, is_ctx, _ in streams]
        x = outs[0]
        if not last:
            y = outs[1]
    return x
```

```python
import functools
import math

import jax
import jax.numpy as jnp
from jax import lax
from jax.experimental import pallas as pl
from jax.experimental.pallas import tpu as pltpu
from jax.experimental.pallas import tpu_sc as plsc

F32 = jnp.float32
BF16 = jnp.bfloat16

D_MODEL = 1024
GRID_W = 64
N_HEADS = 4
HEAD_QK = 64
HEAD_V = 128
ROPE_BASE = 10000.0
ROPE_FREQS = HEAD_QK // 4
CONV_K = 3
POOL_WINDOWS = (2, 4, 8, 16)
POOL_GW = 128
N_BRANCH = 3
BRANCH_W = 512
GATES_W = N_BRANCH * D_MODEL
IN_W = GATES_W + 7 * BRANCH_W
D_FF = 2816
N_EXPERTS = 8
EPS = 1e-6

LANES = 128
BF16_SUBLANES = 16
COL_BLK = 512
Q_COL, K_COL, V_COL, CB_COL, CC_COL, CX_COL, PIN_COL = range(6, 13)
HEAD_BLKS = COL_BLK // 128
MOD_ROWS = 16
CTX_MOD_ROW = 8
HALO = BF16_SUBLANES
FF_CHUNK = 256
KEY_CHUNK = 1024
LOG2_E = 1.4426950408889634
VMEM_LIMIT = 56 * 1024 * 1024


def _params(sem, vmem=VMEM_LIMIT):
    return pltpu.CompilerParams(dimension_semantics=sem, vmem_limit_bytes=vmem)


def _rms(t, g):
    return t * lax.rsqrt(jnp.mean(t * t, axis=-1, keepdims=True) + EPS) * g


def _mod_kernel(c_ref, w_ref, b_ref, o_ref):
    c = c_ref[...]
    s = c * jax.nn.sigmoid(c)
    o_ref[0] = jnp.dot(s, w_ref[0], preferred_element_type=F32,
                       precision=lax.Precision.HIGHEST) + b_ref[0]


def _modulation(cvec, w_mod, b_mod):
    depth = w_mod.shape[0]
    wcols = w_mod.shape[2]
    tn = 1536
    return pl.pallas_call(
        _mod_kernel,
        grid=(depth, wcols // tn),
        in_specs=[pl.BlockSpec((MOD_ROWS, D_MODEL), lambda l, j: (0, 0)),
                  pl.BlockSpec((1, D_MODEL, tn), lambda l, j: (l, 0, j)),
                  pl.BlockSpec((1, 1, tn), lambda l, j: (l, 0, j))],
        out_specs=pl.BlockSpec((1, MOD_ROWS, tn), lambda l, j: (l, 0, j)),
        out_shape=jax.ShapeDtypeStruct((depth, MOD_ROWS, wcols), F32),
        compiler_params=_params(("parallel", "parallel")),
        name="modulation",
    )(cvec, w_mod, b_mod.reshape(depth, 1, wcols))


def _mod_spec(layer, chunk, ctx, ngrid):
    def idx(*g):
        row = CTX_MOD_ROW if ctx else g[0]
        return (layer * MOD_ROWS + row, 0, chunk)
    del ngrid
    return pl.BlockSpec((1, 1, D_MODEL), idx)


def _inproj_kernel(x_ref, g_ref, sh_ref, sc_ref, w_ref, cos_ref, sa_ref, sb_ref, o_ref,
                   *, rope, col_start):
    h = (_rms(x_ref[0], g_ref[...]) * (1.0 + sc_ref[0]) + sh_ref[0]).astype(BF16)
    for j in range(w_ref.shape[1] // COL_BLK):
        z = jnp.dot(h, w_ref[:, j * COL_BLK:(j + 1) * COL_BLK], preferred_element_type=F32)
        if rope and j + col_start in (Q_COL, K_COL):
            cos, sa, sb = cos_ref[...], sa_ref[...], sb_ref[...]
            for c in range(COL_BLK // LANES):
                t = z[:, c * LANES:(c + 1) * LANES]
                r = (t * cos + pltpu.roll(t, LANES - ROPE_FREQS, 1) * sa
                     + pltpu.roll(t, ROPE_FREQS, 1) * sb)
                o_ref[0, :, j * COL_BLK + c * LANES:j * COL_BLK + (c + 1) * LANES] = r.astype(BF16)
        else:
            o_ref[0, :, j * COL_BLK:(j + 1) * COL_BLK] = z.astype(BF16)


def _resident(shape):
    return pl.BlockSpec(shape, lambda *_: (0,) * len(shape), pipeline_mode=pl.Buffered(1))


def _inproj(x, g, mod, w, tables, *, layer, ctx, rope, col_start, tm):
    bt, L, _ = x.shape
    cos, sa, sb = tables
    tab_spec = pl.BlockSpec((tm, LANES), lambda b, i: (i, 0))
    return pl.pallas_call(
        functools.partial(_inproj_kernel, rope=rope, col_start=col_start),
        grid=(bt, L // tm),
        in_specs=[pl.BlockSpec((1, tm, D_MODEL), lambda b, i: (b, i, 0)),
                  _resident((1, D_MODEL)),
                  _mod_spec(layer, 0, ctx, 2),
                  _mod_spec(layer, 1, ctx, 2),
                  _resident(w.shape),
                  tab_spec, tab_spec, tab_spec],
        out_specs=pl.BlockSpec((1, tm, w.shape[1]), lambda b, i: (b, i, 0)),
        out_shape=jax.ShapeDtypeStruct((bt, L, w.shape[1]), BF16),
        compiler_params=_params(("parallel", "parallel")),
        name="inproj",
    )(x, g, mod, mod, w, cos, sa, sb)


def _attn_kernel(*refs, nseg, lam_init):
    q_ref, lam_ref, gs_ref = refs[0], refs[1], refs[2]
    kv_refs = refs[3:3 + 2 * nseg]
    o_ref = refs[3 + 2 * nseg]

    lv = lam_ref[...]
    lam = (jnp.exp(jnp.sum(lv[0:1] * lv[1:2], axis=-1, keepdims=True))
           - jnp.exp(jnp.sum(lv[2:3] * lv[3:4], axis=-1, keepdims=True)) + lam_init)

    q = q_ref[0]
    tq = q.shape[0]
    lane = lax.broadcasted_iota(jnp.int32, q.shape, 1)
    qs = (q.astype(F32) * (HEAD_QK ** -0.5 * LOG2_E)).astype(BF16)
    qmap = [jnp.where(lane < HEAD_QK, qs, jnp.zeros_like(qs)),
            jnp.where(lane >= HEAD_QK, qs, jnp.zeros_like(qs))]

    m = [jnp.full((tq, 1), -jnp.inf, F32) for _ in range(2)]
    acc = [jnp.zeros((tq, 2 * HEAD_V), F32) for _ in range(2)]
    dn = (((1,), (1,)), ((), ()))
    for s in range(nseg):
        k_ref, v_ref = kv_refs[2 * s], kv_refs[2 * s + 1]
        n_keys = k_ref.shape[1]
        for c0 in range(0, n_keys, KEY_CHUNK):
            ck = min(KEY_CHUNK, n_keys - c0)
            k = k_ref[0, c0:c0 + ck, :]
            one_col = jnp.where(lax.broadcasted_iota(jnp.int32, (ck, HEAD_V), 1) == 0, 1.0, 0.0)
            v_aug = jnp.concatenate([v_ref[0, c0:c0 + ck, :], one_col.astype(BF16)], axis=1)
            for j in range(2):
                sc = lax.dot_general(qmap[j], k, dn, preferred_element_type=F32)
                m_new = jnp.maximum(m[j], jnp.max(sc, axis=-1, keepdims=True))
                p = jnp.exp2(sc - m_new).astype(BF16)
                acc[j] = acc[j] * jnp.exp2(m[j] - m_new) + jnp.dot(p, v_aug,
                                                                  preferred_element_type=F32)
                m[j] = m_new
    o = (acc[0][:, :HEAD_V] / acc[0][:, HEAD_V:HEAD_V + 1]
         - acc[1][:, :HEAD_V] * (lam / acc[1][:, HEAD_V:HEAD_V + 1]))
    o_ref[0] = (_rms(o, gs_ref[...]) * (1.0 - lam_init)).astype(BF16)


def _attention(zq, segs, lam_vecs, g_sub, *, lam_init, tq):
    bt, lq, _ = zq.shape
    q0 = Q_COL * HEAD_BLKS
    in_specs = [pl.BlockSpec((1, tq, HEAD_V), lambda b, h, i: (b, i, q0 + h)),
                pl.BlockSpec((4, HEAD_QK), lambda b, h, i: (0, 0)),
                pl.BlockSpec((1, HEAD_V), lambda b, h, i: (0, 0))]
    args = [zq, lam_vecs, g_sub]
    for arr, kc, vc in segs:
        t = arr.shape[1]
        in_specs.append(pl.BlockSpec((1, t, HEAD_V), lambda b, h, i, kc=kc: (b, 0, kc + h)))
        in_specs.append(pl.BlockSpec((1, t, HEAD_V), lambda b, h, i, vc=vc: (b, 0, vc + h)))
        args += [arr, arr]
    return pl.pallas_call(
        functools.partial(_attn_kernel, nseg=len(segs), lam_init=lam_init),
        grid=(bt, N_HEADS, lq // tq),
        in_specs=in_specs,
        out_specs=pl.BlockSpec((1, tq, HEAD_V), lambda b, h, i: (b, i, h)),
        out_shape=jax.ShapeDtypeStruct((bt, lq, N_HEADS * HEAD_V), BF16),
        compiler_params=_params(("parallel", "parallel", "arbitrary")),
        name="diff_attention",
    )(*args)


def _merge_kernel(attn_ref, cb_ref, cc_ref, cx_ref, pin_ref, g0_ref, g1_ref, g2_ref,
                  ccp_ref, cxp_ref, pinp_ref, ccn_ref, cxn_ref, pinn_ref,
                  x_ref, gate_ref, gpost_ref, cw_ref, pw_ref, ps_ref, wb_ref, wo_ref,
                  o_ref, u_scr, p_scr, *, tm, seq_len):
    i = pl.program_id(1)
    first = i == 0
    last = i == pl.num_programs(1) - 1

    def f32(ref):
        return ref[0].astype(F32)

    u_scr[0:HALO] = jnp.where(first, 0.0, f32(ccp_ref) * f32(cxp_ref))
    u_scr[HALO:HALO + tm] = f32(cc_ref) * f32(cx_ref)
    u_scr[HALO + tm:2 * HALO + tm] = jnp.where(last, 0.0, f32(ccn_ref) * f32(cxn_ref))
    p_scr[0:HALO] = jnp.where(first, 0.0, f32(pinp_ref))
    p_scr[HALO:HALO + tm] = f32(pin_ref)
    p_scr[HALO + tm:2 * HALO + tm] = jnp.where(last, 0.0, f32(pinn_ref))

    cw = cw_ref[...]
    conv = (cw[0:1] * u_scr[HALO - 1:HALO - 1 + tm] + cw[1:2] * u_scr[HALO:HALO + tm]
            + cw[2:3] * u_scr[HALO + 1:HALO + 1 + tm])
    conv_o = (f32(cb_ref) * conv).astype(BF16)

    t = i * tm + lax.broadcasted_iota(jnp.int32, (tm, 1), 0)
    pool_parts = []
    for g, w in enumerate(POOL_WINDOWS):
        cs = slice(g * POOL_GW, (g + 1) * POOL_GW)
        acc = p_scr[HALO - w // 2:HALO - w // 2 + tm, cs]
        for d in range(-w // 2 + 1, w // 2):
            acc = acc + p_scr[HALO + d:HALO + d + tm, cs]
        lo = jnp.maximum(t - w // 2, 0)
        hi = jnp.minimum(t - w // 2 + w, seq_len)
        p = acc / (hi - lo).astype(F32) - p_scr[HALO:HALO + tm, cs]
        pool_parts.append(jnp.dot(p.astype(BF16), pw_ref[g], preferred_element_type=F32))
    pool_o = (jnp.concatenate(pool_parts, axis=-1) * ps_ref[...]).astype(BF16)

    merged = (jax.nn.sigmoid(f32(g0_ref)) * jnp.dot(attn_ref[0], wb_ref[0], preferred_element_type=F32)
              + jax.nn.sigmoid(f32(g1_ref)) * jnp.dot(conv_o, wb_ref[1], preferred_element_type=F32)
              + jax.nn.sigmoid(f32(g2_ref)) * jnp.dot(pool_o, wb_ref[2], preferred_element_type=F32))
    mix = jnp.dot(merged.astype(BF16), wo_ref[...], preferred_element_type=F32)
    o_ref[0] = x_ref[0] + gate_ref[0] * _rms(mix, gpost_ref[...])


def _merge(attn, z, x, mod, g_post, conv_w, pool_w, pool_scale, w_branch, w_out, *, layer, ctx, tm):
    bt, L, _ = x.shape
    hb = tm // HALO
    nhb = L // HALO

    def col(c, width=COL_BLK):
        return pl.BlockSpec((1, tm, width), lambda b, i, c=c: (b, i, c))

    def prev(c):
        return pl.BlockSpec((1, HALO, COL_BLK), lambda b, i, c=c: (b, jnp.maximum(i * hb - 1, 0), c))

    def nxt(c):
        return pl.BlockSpec((1, HALO, COL_BLK),
                            lambda b, i, c=c: (b, jnp.minimum((i + 1) * hb, nhb - 1), c))

    def full(a):
        return pl.BlockSpec(a.shape, lambda b, i, n=a.ndim: (0,) * n)

    in_specs = [pl.BlockSpec((1, tm, BRANCH_W), lambda b, i: (b, i, 0)),
                col(CB_COL), col(CC_COL), col(CX_COL), col(PIN_COL)]
    in_specs += [col(k, D_MODEL) for k in range(N_BRANCH)]
    in_specs += [prev(CC_COL), prev(CX_COL), prev(PIN_COL), nxt(CC_COL), nxt(CX_COL), nxt(PIN_COL)]
    in_specs += [pl.BlockSpec((1, tm, D_MODEL), lambda b, i: (b, i, 0)),
                 _mod_spec(layer, 2, ctx, 2),
                 full(g_post), full(conv_w), full(pool_w), full(pool_scale), full(w_branch),
                 full(w_out)]
    zz = z
    return pl.pallas_call(
        functools.partial(_merge_kernel, tm=tm, seq_len=L),
        grid=(bt, L // tm),
        in_specs=in_specs,
        out_specs=pl.BlockSpec((1, tm, D_MODEL), lambda b, i: (b, i, 0)),
        out_shape=jax.ShapeDtypeStruct((bt, L, D_MODEL), F32),
        scratch_shapes=[pltpu.VMEM((tm + 2 * HALO, BRANCH_W), F32),
                        pltpu.VMEM((tm + 2 * HALO, BRANCH_W), F32)],
        compiler_params=_params(("parallel", "arbitrary")),
        name="mixer_merge",
    )(attn, *([zz] * 13), x, mod, g_post, conv_w, pool_w, pool_scale, w_branch, w_out)


def _swiglu(h, wg_ref, wu_ref, wd_ref):
    f = None
    for c in range(D_FF // FF_CHUNK):
        cs = slice(c * FF_CHUNK, (c + 1) * FF_CHUNK)
        gt = jnp.dot(h, wg_ref[0, :, cs], preferred_element_type=F32)
        up = jnp.dot(h, wu_ref[0, :, cs], preferred_element_type=F32)
        a = (gt * jax.nn.sigmoid(gt) * up).astype(BF16)
        part = jnp.dot(a, wd_ref[0, cs, :], preferred_element_type=F32)
        f = part if f is None else f + part
    return f


def _ffn_kernel(x_ref, g_ref, sh_ref, sc_ref, gate_ref, gpost_ref, wg_ref, wu_ref, wd_ref, o_ref):
    h = (_rms(x_ref[0], g_ref[...]) * (1.0 + sc_ref[0]) + sh_ref[0]).astype(BF16)
    f = _swiglu(h, wg_ref, wu_ref, wd_ref)
    o_ref[0] = x_ref[0] + gate_ref[0] * _rms(f, gpost_ref[...])


def _ffn(x, g_pre, g_post, mod, wg, wu, wd, *, layer, ctx, tm):
    bt, L, _ = x.shape
    vec = pl.BlockSpec((1, D_MODEL), lambda b, i: (0, 0))
    return pl.pallas_call(
        _ffn_kernel,
        grid=(bt, L // tm),
        in_specs=[pl.BlockSpec((1, tm, D_MODEL), lambda b, i: (b, i, 0)),
                  vec,
                  _mod_spec(layer, 3, ctx, 2),
                  _mod_spec(layer, 4, ctx, 2),
                  _mod_spec(layer, 5, ctx, 2),
                  vec,
                  pl.BlockSpec((1, D_MODEL, D_FF), lambda b, i: (0, 0, 0)),
                  pl.BlockSpec((1, D_MODEL, D_FF), lambda b, i: (0, 0, 0)),
                  pl.BlockSpec((1, D_FF, D_MODEL), lambda b, i: (0, 0, 0))],
        out_specs=pl.BlockSpec((1, tm, D_MODEL), lambda b, i: (b, i, 0)),
        out_shape=jax.ShapeDtypeStruct((bt, L, D_MODEL), F32),
        compiler_params=_params(("parallel", "parallel")),
        name="channel_mixer",
    )(x, g_pre, mod, mod, mod, g_post, wg, wu, wd)


ROUTE_TM = 512
GROUP_TM = 512
TOP_K = 2
ROW_TILE = (8, LANES)
SC_CORES, SC_SUBCORES = 2, 16
SC_WORKERS = SC_CORES * SC_SUBCORES
SC_CHUNK = 32
RANK_LANE, PROB_LANE, EXPERT_LANE = 0, 2, 4


def _flat_mod_spec(layer, chunk, ctx, tm, seq_len):
    def idx(i):
        row = CTX_MOD_ROW if ctx else (i * tm) // seq_len
        return (layer * MOD_ROWS + row, 0, chunk)
    return pl.BlockSpec((1, 1, D_MODEL), idx)


def _lane_pick(v, lane, k):
    return jnp.sum(jnp.where(lane == k, v, 0.0), axis=-1, keepdims=True)


def _route_kernel(x_ref, g_ref, sh_ref, sc_ref, wr_ref, h_ref, info_ref, cnt_ref, run_scr):
    @pl.when(pl.program_id(0) == 0)
    def _():
        run_scr[...] = jnp.zeros_like(run_scr)

    h = _rms(x_ref[...], g_ref[...]) * (1.0 + sc_ref[0]) + sh_ref[0]
    h_ref[...] = h.reshape(h_ref.shape)
    logits = jnp.dot(h, wr_ref[...], preferred_element_type=F32, precision=lax.Precision.HIGHEST)
    lane = lax.broadcasted_iota(jnp.int32, logits.shape, 1)
    neg = jnp.float32(-jnp.inf)
    l1 = jnp.where(lane < N_EXPERTS, logits, neg)
    m1 = jnp.max(l1, axis=-1, keepdims=True)
    i1 = jnp.min(jnp.where(l1 == m1, lane, LANES), axis=-1, keepdims=True)
    l2 = jnp.where(lane == i1, neg, l1)
    m2 = jnp.max(l2, axis=-1, keepdims=True)
    i2 = jnp.min(jnp.where(l2 == m2, lane, LANES), axis=-1, keepdims=True)
    e2 = jnp.exp(m2 - m1)
    den = 1.0 + e2

    chosen = jnp.where(jnp.logical_or(lane == i1, lane == i2), 1.0, 0.0)
    tm = chosen.shape[0]
    earlier = (lax.broadcasted_iota(jnp.int32, (tm, tm), 1)
               < lax.broadcasted_iota(jnp.int32, (tm, tm), 0))
    before = jnp.dot(jnp.where(earlier, 1.0, 0.0).astype(BF16), chosen.astype(BF16),
                     preferred_element_type=F32)
    rank = before + run_scr[...]
    rec = [_lane_pick(rank, lane, i1), _lane_pick(rank, lane, i2), 1.0 / den, e2 / den,
           i1.astype(F32), i2.astype(F32)]
    info = jnp.zeros_like(logits)
    for k, v in enumerate(rec):
        info = jnp.where(lane == k, v, info)
    info_ref[...] = info
    run_scr[...] += jnp.sum(chosen, axis=0, keepdims=True)
    cnt_ref[...] = run_scr[...]


def _route(x2, g, mod, wr_pad, *, layer, ctx, seq_len):
    n = x2.shape[0]
    tm = ROUTE_TM
    vec = pl.BlockSpec((1, D_MODEL), lambda i: (0, 0))
    return pl.pallas_call(
        _route_kernel,
        grid=(n // tm,),
        in_specs=[pl.BlockSpec((tm, D_MODEL), lambda i: (i, 0)),
                  vec,
                  _flat_mod_spec(layer, 3, ctx, tm, seq_len),
                  _flat_mod_spec(layer, 4, ctx, tm, seq_len),
                  pl.BlockSpec((D_MODEL, LANES), lambda i: (0, 0))],
        out_specs=[pl.BlockSpec((tm,) + ROW_TILE, lambda i: (i, 0, 0)),
                   pl.BlockSpec((tm, LANES), lambda i: (i, 0)),
                   pl.BlockSpec((1, LANES), lambda i: (0, 0))],
        out_shape=[jax.ShapeDtypeStruct((n,) + ROW_TILE, F32),
                   jax.ShapeDtypeStruct((n, LANES), F32),
                   jax.ShapeDtypeStruct((1, LANES), F32)],
        scratch_shapes=[pltpu.VMEM((1, LANES), F32)],
        compiler_params=_params(("arbitrary",)),
        name="route",
    )(x2, g, mod, mod, wr_pad)


def _sc_gather(table, idx):
    n_out = idx.shape[0]
    per_worker = n_out // SC_WORKERS
    n_chunks = per_worker // SC_CHUNK
    mesh = plsc.VectorSubcoreMesh(core_axis_name="c", subcore_axis_name="s")

    @functools.partial(
        pl.kernel, mesh=mesh,
        out_type=jax.ShapeDtypeStruct((n_out,) + table.shape[1:], table.dtype),
        scratch_types=[pltpu.VMEM((per_worker,), jnp.int32),
                       pltpu.VMEM((SC_CHUNK,) + table.shape[1:], table.dtype),
                       pltpu.SemaphoreType.DMA],
        name="sc_row_gather")
    def gather(table_hbm, idx_hbm, out_hbm, idx_v, rows_v, sem):
        wid = lax.axis_index("s") * SC_CORES + lax.axis_index("c")
        base = wid * per_worker
        pltpu.sync_copy(idx_hbm.at[pl.ds(base, per_worker)], idx_v)

        @pl.loop(0, n_chunks)
        def _(c):
            off = c * SC_CHUNK
            pltpu.async_copy(table_hbm.at[idx_v.at[pl.ds(off, SC_CHUNK)]], rows_v, sem).wait()
            pltpu.sync_copy(rows_v, out_hbm.at[pl.ds(base + off, SC_CHUNK)])

    return gather(table, idx)


def _sc_scatter(rows, dest):
    n = rows.shape[0]
    n_out = dest.size
    per_worker = n_out // SC_WORKERS
    n_chunks = per_worker // SC_CHUNK
    mesh = plsc.VectorSubcoreMesh(core_axis_name="c", subcore_axis_name="s")

    @functools.partial(
        pl.kernel, mesh=mesh,
        out_type=jax.ShapeDtypeStruct((n_out,) + rows.shape[1:], rows.dtype),
        scratch_types=[pltpu.VMEM((n_chunks, SC_CHUNK), jnp.int32),
                       pltpu.VMEM((SC_CHUNK,) + rows.shape[1:], rows.dtype),
                       pltpu.SemaphoreType.DMA],
        name="sc_row_scatter")
    def scatter(rows_hbm, dest_hbm, out_hbm, dest_v, rows_v, sem):
        wid = lax.axis_index("s") * SC_CORES + lax.axis_index("c")
        src_base = lax.rem(wid * per_worker, n)
        pltpu.sync_copy(dest_hbm.at[pl.ds(wid * n_chunks, n_chunks)], dest_v)

        @pl.loop(0, n_chunks)
        def _(c):
            pltpu.sync_copy(rows_hbm.at[pl.ds(src_base + c * SC_CHUNK, SC_CHUNK)], rows_v)
            pltpu.async_copy(rows_v, out_hbm.at[dest_v.at[c]], sem).wait()

    return scatter(rows, dest)


def _group_kernel(vblk_ref, vexp_ref, vlo_ref, vhi_ref, vfirst_ref, vvalid_ref,
                  h_ref, wg_ref, wu_ref, wd_ref, y_ref):
    del vblk_ref, vexp_ref
    v = pl.program_id(0)

    @pl.when(vvalid_ref[v] == 1)
    def _():
        tg = h_ref.shape[0]
        h = h_ref[...].reshape(tg, D_MODEL).astype(BF16)
        f = _swiglu(h, wg_ref, wu_ref, wd_ref).reshape(y_ref.shape)

        @pl.when(vfirst_ref[v] == 1)
        def _():
            y_ref[...] = f

        @pl.when(vfirst_ref[v] == 0)
        def _():
            row = lax.broadcasted_iota(jnp.int32, (tg, 1, 1), 0)
            mine = jnp.logical_and(row >= vlo_ref[v], row < vhi_ref[v])
            y_ref[...] = jnp.where(mine, f, y_ref[...])


def _grouped_experts(hs, visits, wg, wu, wd):
    n_rows = hs.shape[0]
    tg = GROUP_TM
    nv = visits[0].shape[0]

    def wspec(shape):
        return pl.BlockSpec((1,) + shape, lambda v, blk, exp, *_: (exp[v], 0, 0))

    row_spec = pl.BlockSpec((tg,) + ROW_TILE, lambda v, blk, *_: (blk[v], 0, 0))
    return pl.pallas_call(
        _group_kernel,
        grid_spec=pltpu.PrefetchScalarGridSpec(
            num_scalar_prefetch=len(visits),
            grid=(nv,),
            in_specs=[row_spec, wspec((D_MODEL, D_FF)), wspec((D_MODEL, D_FF)),
                      wspec((D_FF, D_MODEL))],
            out_specs=row_spec),
        out_shape=jax.ShapeDtypeStruct((n_rows,) + ROW_TILE, F32),
        compiler_params=_params(("arbitrary",)),
        name="grouped_experts",
    )(*visits, hs, wg, wu, wd)


def _visit_tables(counts, n_rows):
    tg = GROUP_TM
    nv = n_rows // tg + N_EXPERTS - 1
    ends = jnp.cumsum(counts)
    starts = ends - counts
    first_tile = starts // tg
    last_tile = jnp.maximum(ends - 1, 0) // tg
    nvis = jnp.where(counts > 0, last_tile - first_tile + 1, 0)
    vend = jnp.cumsum(nvis)
    total = vend[-1]
    v = jnp.minimum(jnp.arange(nv, dtype=jnp.int32), total - 1)
    exp = jnp.sum(v[:, None] >= vend[None, :], axis=1).astype(jnp.int32)
    tile = first_tile[exp] + v - (vend[exp] - nvis[exp])
    lo = jnp.maximum(starts[exp], tile * tg) - tile * tg
    hi = jnp.minimum(ends[exp], (tile + 1) * tg) - tile * tg
    first = jnp.concatenate([jnp.ones((1,), jnp.int32), (tile[1:] != tile[:-1]).astype(jnp.int32)])
    valid = (jnp.arange(nv) < total).astype(jnp.int32)
    return tuple(a.astype(jnp.int32) for a in (tile, exp, lo, hi, first, valid)), starts


def _combine_kernel(x_ref, y1_ref, y2_ref, info_ref, gate_ref, gpost_ref, o_ref):
    info = info_ref[...]
    lane = lax.broadcasted_iota(jnp.int32, info.shape, 1)
    f = (_lane_pick(info, lane, PROB_LANE) * y1_ref[...].reshape(x_ref.shape)
         + _lane_pick(info, lane, PROB_LANE + 1) * y2_ref[...].reshape(x_ref.shape))
    o_ref[...] = x_ref[...] + gate_ref[0] * _rms(f, gpost_ref[...])


def _combine(x2, y12, info, mod, g_post, *, layer, ctx, seq_len):
    n = x2.shape[0]
    tm = ROUTE_TM
    nt = n // tm
    row = pl.BlockSpec((tm, D_MODEL), lambda i: (i, 0))
    return pl.pallas_call(
        _combine_kernel,
        grid=(nt,),
        in_specs=[row, pl.BlockSpec((tm,) + ROW_TILE, lambda i: (i, 0, 0)),
                  pl.BlockSpec((tm,) + ROW_TILE, lambda i: (nt + i, 0, 0)),
                  pl.BlockSpec((tm, LANES), lambda i: (i, 0)),
                  _flat_mod_spec(layer, 5, ctx, tm, seq_len),
                  pl.BlockSpec((1, D_MODEL), lambda i: (0, 0))],
        out_specs=row,
        out_shape=jax.ShapeDtypeStruct((n, D_MODEL), F32),
        compiler_params=_params(("parallel",)),
        name="combine",
    )(x2, y12, y12, info, mod, g_post)


def _moe(x, g_pre, g_post, mod, wr_pad, wg, wu, wd, *, layer, ctx):
    bt, seq_len, _ = x.shape
    n = bt * seq_len
    x2 = x.reshape(n, D_MODEL)
    h, info, cnt = _route(x2, g_pre, mod, wr_pad, layer=layer, ctx=ctx, seq_len=seq_len)

    counts = cnt[0, :N_EXPERTS].astype(jnp.int32)
    visits, starts = _visit_tables(counts, TOP_K * n)
    expert = info[:, EXPERT_LANE:EXPERT_LANE + TOP_K].astype(jnp.int32)
    rank = info[:, RANK_LANE:RANK_LANE + TOP_K].astype(jnp.int32)
    pos = (starts[expert] + rank).T.reshape(-1)
    hs = _sc_scatter(h, pos.reshape(-1, SC_CHUNK))
    ys = _grouped_experts(hs, visits, wg, wu, wd)
    y12 = _sc_gather(ys, pos)
    out = _combine(x2, y12, info, mod, g_post, layer=layer, ctx=ctx, seq_len=seq_len)
    return out.reshape(bt, seq_len, D_MODEL)


def _rope_tables(seq_len):
    rows = jnp.repeat(jnp.arange(seq_len // GRID_W), GRID_W).astype(F32)
    cols = jnp.tile(jnp.arange(GRID_W), seq_len // GRID_W).astype(F32)
    inv = ROPE_BASE ** (-jnp.arange(ROPE_FREQS, dtype=F32) / ROPE_FREQS)
    ang = jnp.concatenate([rows[:, None] * inv, rows[:, None] * inv,
                           cols[:, None] * inv, cols[:, None] * inv], axis=1)
    ang = jnp.tile(ang, (1, LANES // HEAD_QK))
    cos, sin = jnp.cos(ang), jnp.sin(ang)
    low = (jnp.arange(LANES) % (2 * ROPE_FREQS)) < ROPE_FREQS
    return cos, jnp.where(low, -sin, 0.0), jnp.where(low, 0.0, sin)


def kernel(x, c, ctx, c_ctx, w_mod, b_mod, g_pre_mix, g_post_mix, g_pre_ffn, g_post_ffn,
           w_in, lambda_q1, lambda_k1, lambda_q2, lambda_k2, g_subln, conv_w, pool_w,
           pool_scale, w_branch, w_out, ffn_w_gate, ffn_w_up, ffn_w_down, router_w,
           moe_w_gate, moe_w_up, moe_w_down):
    depth = w_in.shape[0]
    bsz, seq, _ = x.shape
    ctx_len = ctx.shape[1]

    cvec = jnp.zeros((MOD_ROWS, D_MODEL), F32).at[:bsz].set(c).at[CTX_MOD_ROW].set(c_ctx)
    mod = _modulation(cvec, w_mod, b_mod).reshape(depth * MOD_ROWS, 1, 6 * D_MODEL)
    tables = _rope_tables(seq)
    no_tables = tuple(t[:ctx_len] for t in tables)

    y = ctx
    for i in range(depth):
        last = i == depth - 1
        lam_init = 0.8 - 0.6 * math.exp(-0.3 * i)
        split = w_in.shape[2] - GATES_W
        w_in_i = jnp.concatenate([w_in[i, :, split:], w_in[i, :, :split]], axis=1).astype(BF16)
        lam_vecs = jnp.stack([lambda_q1[i], lambda_k1[i], lambda_q2[i], lambda_k2[i]]).astype(F32)
        g_sub = g_subln[i].reshape(1, HEAD_V)
        g_pm, g_qm = g_pre_mix[i].reshape(1, D_MODEL), g_post_mix[i].reshape(1, D_MODEL)
        g_pf, g_qf = g_pre_ffn[i].reshape(1, D_MODEL), g_post_ffn[i].reshape(1, D_MODEL)
        mix_w = (conv_w[i], pool_w[i].astype(BF16), pool_scale[i].reshape(1, BRANCH_W),
                 w_branch[i].astype(BF16), w_out[i].astype(BF16))

        z = _inproj(x, g_pm, mod, w_in_i, tables, layer=i, ctx=False, rope=True,
                    col_start=0, tm=512)
        kv_full = (K_COL * HEAD_BLKS, V_COL * HEAD_BLKS)
        if last:
            w_kv = w_in_i[:, K_COL * COL_BLK:(V_COL + 1) * COL_BLK]
            zc = _inproj(y, g_pm, mod, w_kv, no_tables, layer=i, ctx=True, rope=False,
                         col_start=K_COL, tm=ctx_len)
            segs = [(z,) + kv_full, (zc, 0, HEAD_BLKS)]
        else:
            zc = _inproj(y, g_pm, mod, w_in_i, no_tables, layer=i, ctx=True, rope=False,
                         col_start=0, tm=ctx_len)
            segs = [(z,) + kv_full, (zc,) + kv_full]
            attn_c = _attention(zc, [(zc,) + kv_full], lam_vecs, g_sub, lam_init=lam_init,
                                tq=ctx_len)
            y = _merge(attn_c, zc, y, mod, g_qm, *mix_w, layer=i, ctx=True, tm=ctx_len)
        attn_l = _attention(z, segs, lam_vecs, g_sub, lam_init=lam_init, tq=256)
        x = _merge(attn_l, z, x, mod, g_qm, *mix_w, layer=i, ctx=False, tm=512)

        j = i // 2
        streams = [(x, False, 512)] + ([] if last else [(y, True, ctx_len)])
        if i % 2 == 0:
            ffn_w = (ffn_w_gate[j:j + 1].astype(BF16), ffn_w_up[j:j + 1].astype(BF16),
                     ffn_w_down[j:j + 1].astype(BF16))
            outs = [_ffn(t, g_pf, g_qf, mod, *ffn_w, layer=i, ctx=is_ctx, tm=tm)
                    for t, is_ctx, tm in streams]
        else:
            ffn_w = (moe_w_gate[j].astype(BF16), moe_w_up[j].astype(BF16), moe_w_down[j].astype(BF16))
            wr_pad = jnp.zeros((D_MODEL, LANES), F32).at[:, :N_EXPERTS].set(router_w[j])
            outs = [_moe(t, g_pf, g_qf, mod, wr_pad, *ffn_w, layer=i, ctx=is_ctx)
                    for t, is_ctx, _ in streams]
        x = outs[0]
        if not last:
            y = outs[1]
    return x
```

```python
import functools
import math

import jax
import jax.numpy as jnp
from jax import lax
from jax.experimental import pallas as pl
from jax.experimental.pallas import tpu as pltpu
from jax.experimental.pallas import tpu_sc as plsc

F32 = jnp.float32
BF16 = jnp.bfloat16

D_MODEL = 1024
GRID_W = 64
N_HEADS = 4
HEAD_QK = 64
HEAD_V = 128
ROPE_BASE = 10000.0
ROPE_FREQS = HEAD_QK // 4
CONV_K = 3
POOL_WINDOWS = (2, 4, 8, 16)
POOL_GW = 128
N_BRANCH = 3
BRANCH_W = 512
GATES_W = N_BRANCH * D_MODEL
IN_W = GATES_W + 7 * BRANCH_W
D_FF = 2816
N_EXPERTS = 8
EPS = 1e-6

LANES = 128
BF16_SUBLANES = 16
COL_BLK = 512
Q_COL, K_COL, V_COL, CB_COL, CC_COL, CX_COL, PIN_COL = range(6, 13)
HEAD_BLKS = COL_BLK // 128
MOD_ROWS = 16
CTX_MOD_ROW = 8
HALO = BF16_SUBLANES
FF_CHUNK = 256
KEY_CHUNK = 1024
LOG2_E = 1.4426950408889634
VMEM_LIMIT = 56 * 1024 * 1024


def _params(sem, vmem=VMEM_LIMIT):
    return pltpu.CompilerParams(dimension_semantics=sem, vmem_limit_bytes=vmem)


def _rms(t, g):
    return t * lax.rsqrt(jnp.mean(t * t, axis=-1, keepdims=True) + EPS) * g


def _mod_kernel(c_ref, w_ref, b_ref, o_ref):
    c = c_ref[...]
    s = c * jax.nn.sigmoid(c)
    o_ref[0] = jnp.dot(s, w_ref[0], preferred_element_type=F32,
                       precision=lax.Precision.HIGHEST) + b_ref[0]


def _modulation(cvec, w_mod, b_mod):
    depth = w_mod.shape[0]
    wcols = w_mod.shape[2]
    tn = 1536
    return pl.pallas_call(
        _mod_kernel,
        grid=(depth, wcols // tn),
        in_specs=[pl.BlockSpec((MOD_ROWS, D_MODEL), lambda l, j: (0, 0)),
                  pl.BlockSpec((1, D_MODEL, tn), lambda l, j: (l, 0, j)),
                  pl.BlockSpec((1, 1, tn), lambda l, j: (l, 0, j))],
        out_specs=pl.BlockSpec((1, MOD_ROWS, tn), lambda l, j: (l, 0, j)),
        out_shape=jax.ShapeDtypeStruct((depth, MOD_ROWS, wcols), F32),
        compiler_params=_params(("parallel", "parallel")),
        name="modulation",
    )(cvec, w_mod, b_mod.reshape(depth, 1, wcols))


def _mod_spec(layer, chunk, ctx, ngrid):
    def idx(*g):
        row = CTX_MOD_ROW if ctx else g[0]
        return (layer * MOD_ROWS + row, 0, chunk)
    del ngrid
    return pl.BlockSpec((1, 1, D_MODEL), idx)


def _inproj_kernel(x_ref, g_ref, sh_ref, sc_ref, w_ref, cos_ref, sa_ref, sb_ref, o_ref,
                   *, rope, col_start):
    h = (_rms(x_ref[0], g_ref[...]) * (1.0 + sc_ref[0]) + sh_ref[0]).astype(BF16)
    for j in range(w_ref.shape[1] // COL_BLK):
        z = jnp.dot(h, w_ref[:, j * COL_BLK:(j + 1) * COL_BLK], preferred_element_type=F32)
        if rope and j + col_start in (Q_COL, K_COL):
            cos, sa, sb = cos_ref[...], sa_ref[...], sb_ref[...]
            for c in range(COL_BLK // LANES):
                t = z[:, c * LANES:(c + 1) * LANES]
                r = (t * cos + pltpu.roll(t, LANES - ROPE_FREQS, 1) * sa
                     + pltpu.roll(t, ROPE_FREQS, 1) * sb)
                o_ref[0, :, j * COL_BLK + c * LANES:j * COL_BLK + (c + 1) * LANES] = r.astype(BF16)
        else:
            o_ref[0, :, j * COL_BLK:(j + 1) * COL_BLK] = z.astype(BF16)


def _resident(shape):
    return pl.BlockSpec(shape, lambda *_: (0,) * len(shape), pipeline_mode=pl.Buffered(1))


def _inproj(x, g, mod, w, tables, *, layer, ctx, rope, col_start, tm):
    bt, L, _ = x.shape
    cos, sa, sb = tables
    tab_spec = pl.BlockSpec((tm, LANES), lambda b, i: (i, 0))
    return pl.pallas_call(
        functools.partial(_inproj_kernel, rope=rope, col_start=col_start),
        grid=(bt, L // tm),
        in_specs=[pl.BlockSpec((1, tm, D_MODEL), lambda b, i: (b, i, 0)),
                  _resident((1, D_MODEL)),
                  _mod_spec(layer, 0, ctx, 2),
                  _mod_spec(layer, 1, ctx, 2),
                  _resident(w.shape),
                  tab_spec, tab_spec, tab_spec],
        out_specs=pl.BlockSpec((1, tm, w.shape[1]), lambda b, i: (b, i, 0)),
        out_shape=jax.ShapeDtypeStruct((bt, L, w.shape[1]), BF16),
        compiler_params=_params(("parallel", "parallel")),
        name="inproj",
    )(x, g, mod, mod, w, cos, sa, sb)


def _attn_kernel(*refs, nseg, lam_init):
    q_ref, lam_ref, gs_ref = refs[0], refs[1], refs[2]
    kv_refs = refs[3:3 + 2 * nseg]
    o_ref = refs[3 + 2 * nseg]

    lv = lam_ref[...]
    lam = (jnp.exp(jnp.sum(lv[0:1] * lv[1:2], axis=-1, keepdims=True))
           - jnp.exp(jnp.sum(lv[2:3] * lv[3:4], axis=-1, keepdims=True)) + lam_init)

    q = q_ref[0]
    tq = q.shape[0]
    lane = lax.broadcasted_iota(jnp.int32, q.shape, 1)
    qs = (q.astype(F32) * (HEAD_QK ** -0.5 * LOG2_E)).astype(BF16)
    qmap = [jnp.where(lane < HEAD_QK, qs, jnp.zeros_like(qs)),
            jnp.where(lane >= HEAD_QK, qs, jnp.zeros_like(qs))]

    m = [jnp.full((tq, 1), -jnp.inf, F32) for _ in range(2)]
    acc = [jnp.zeros((tq, 2 * HEAD_V), F32) for _ in range(2)]
    dn = (((1,), (1,)), ((), ()))
    for s in range(nseg):
        k_ref, v_ref = kv_refs[2 * s], kv_refs[2 * s + 1]
        n_keys = k_ref.shape[1]
        for c0 in range(0, n_keys, KEY_CHUNK):
            ck = min(KEY_CHUNK, n_keys - c0)
            k = k_ref[0, c0:c0 + ck, :]
            one_col = jnp.where(lax.broadcasted_iota(jnp.int32, (ck, HEAD_V), 1) == 0, 1.0, 0.0)
            v_aug = jnp.concatenate([v_ref[0, c0:c0 + ck, :], one_col.astype(BF16)], axis=1)
            for j in range(2):
                sc = lax.dot_general(qmap[j], k, dn, preferred_element_type=F32)
                m_new = jnp.maximum(m[j], jnp.max(sc, axis=-1, keepdims=True))
                p = jnp.exp2(sc - m_new).astype(BF16)
                acc[j] = acc[j] * jnp.exp2(m[j] - m_new) + jnp.dot(p, v_aug,
                                                                  preferred_element_type=F32)
                m[j] = m_new
    o = (acc[0][:, :HEAD_V] / acc[0][:, HEAD_V:HEAD_V + 1]
         - acc[1][:, :HEAD_V] * (lam / acc[1][:, HEAD_V:HEAD_V + 1]))
    o_ref[0] = (_rms(o, gs_ref[...]) * (1.0 - lam_init)).astype(BF16)


def _attention(zq, segs, lam_vecs, g_sub, *, lam_init, tq):
    bt, lq, _ = zq.shape
    q0 = Q_COL * HEAD_BLKS
    in_specs = [pl.BlockSpec((1, tq, HEAD_V), lambda b, h, i: (b, i, q0 + h)),
                pl.BlockSpec((4, HEAD_QK), lambda b, h, i: (0, 0)),
                pl.BlockSpec((1, HEAD_V), lambda b, h, i: (0, 0))]
    args = [zq, lam_vecs, g_sub]
    for arr, kc, vc in segs:
        t = arr.shape[1]
        in_specs.append(pl.BlockSpec((1, t, HEAD_V), lambda b, h, i, kc=kc: (b, 0, kc + h)))
        in_specs.append(pl.BlockSpec((1, t, HEAD_V), lambda b, h, i, vc=vc: (b, 0, vc + h)))
        args += [arr, arr]
    return pl.pallas_call(
        functools.partial(_attn_kernel, nseg=len(segs), lam_init=lam_init),
        grid=(bt, N_HEADS, lq // tq),
        in_specs=in_specs,
        out_specs=pl.BlockSpec((1, tq, HEAD_V), lambda b, h, i: (b, i, h)),
        out_shape=jax.ShapeDtypeStruct((bt, lq, N_HEADS * HEAD_V), BF16),
        compiler_params=_params(("parallel", "parallel", "arbitrary")),
        name="diff_attention",
    )(*args)


def _merge_kernel(attn_ref, cb_ref, cc_ref, cx_ref, pin_ref, g0_ref, g1_ref, g2_ref,
                  ccp_ref, cxp_ref, pinp_ref, ccn_ref, cxn_ref, pinn_ref,
                  x_ref, gate_ref, gpost_ref, cw_ref, pw_ref, ps_ref, wb_ref, wo_ref,
                  o_ref, u_scr, p_scr, *, tm, seq_len):
    i = pl.program_id(1)
    first = i == 0
    last = i == pl.num_programs(1) - 1

    def f32(ref):
        return ref[0].astype(F32)

    u_scr[0:HALO] = jnp.where(first, 0.0, f32(ccp_ref) * f32(cxp_ref))
    u_scr[HALO:HALO + tm] = f32(cc_ref) * f32(cx_ref)
    u_scr[HALO + tm:2 * HALO + tm] = jnp.where(last, 0.0, f32(ccn_ref) * f32(cxn_ref))
    p_scr[0:HALO] = jnp.where(first, 0.0, f32(pinp_ref))
    p_scr[HALO:HALO + tm] = f32(pin_ref)
    p_scr[HALO + tm:2 * HALO + tm] = jnp.where(last, 0.0, f32(pinn_ref))

    cw = cw_ref[...]
    conv = (cw[0:1] * u_scr[HALO - 1:HALO - 1 + tm] + cw[1:2] * u_scr[HALO:HALO + tm]
            + cw[2:3] * u_scr[HALO + 1:HALO + 1 + tm])
    conv_o = (f32(cb_ref) * conv).astype(BF16)

    t = i * tm + lax.broadcasted_iota(jnp.int32, (tm, 1), 0)
    pool_parts = []
    for g, w in enumerate(POOL_WINDOWS):
        cs = slice(g * POOL_GW, (g + 1) * POOL_GW)
        acc = p_scr[HALO - w // 2:HALO - w // 2 + tm, cs]
        for d in range(-w // 2 + 1, w // 2):
            acc = acc + p_scr[HALO + d:HALO + d + tm, cs]
        lo = jnp.maximum(t - w // 2, 0)
        hi = jnp.minimum(t - w // 2 + w, seq_len)
        p = acc / (hi - lo).astype(F32) - p_scr[HALO:HALO + tm, cs]
        pool_parts.append(jnp.dot(p.astype(BF16), pw_ref[g], preferred_element_type=F32))
    pool_o = (jnp.concatenate(pool_parts, axis=-1) * ps_ref[...]).astype(BF16)

    def gate(ref):
        return jax.nn.sigmoid(f32(ref))

    merged = (gate(g0_ref) * jnp.dot(attn_ref[0], wb_ref[0], preferred_element_type=F32)
              + gate(g1_ref) * jnp.dot(conv_o, wb_ref[1], preferred_element_type=F32)
              + gate(g2_ref) * jnp.dot(pool_o, wb_ref[2], preferred_element_type=F32))
    mix = jnp.dot(merged.astype(BF16), wo_ref[...], preferred_element_type=F32)
    o_ref[0] = x_ref[0] + gate_ref[0] * _rms(mix, gpost_ref[...])


def _merge(attn, z, x, mod, g_post, conv_w, pool_w, pool_scale, w_branch, w_out, *, layer, ctx, tm):
    bt, L, _ = x.shape
    hb = tm // HALO
    nhb = L // HALO

    def col(c, width=COL_BLK):
        return pl.BlockSpec((1, tm, width), lambda b, i, c=c: (b, i, c))

    def prev(c):
        return pl.BlockSpec((1, HALO, COL_BLK), lambda b, i, c=c: (b, jnp.maximum(i * hb - 1, 0), c))

    def nxt(c):
        return pl.BlockSpec((1, HALO, COL_BLK),
                            lambda b, i, c=c: (b, jnp.minimum((i + 1) * hb, nhb - 1), c))

    def full(a):
        return pl.BlockSpec(a.shape, lambda b, i, n=a.ndim: (0,) * n)

    in_specs = [pl.BlockSpec((1, tm, BRANCH_W), lambda b, i: (b, i, 0)),
                col(CB_COL), col(CC_COL), col(CX_COL), col(PIN_COL)]
    in_specs += [col(k, D_MODEL) for k in range(N_BRANCH)]
    in_specs += [prev(CC_COL), prev(CX_COL), prev(PIN_COL), nxt(CC_COL), nxt(CX_COL), nxt(PIN_COL)]
    in_specs += [pl.BlockSpec((1, tm, D_MODEL), lambda b, i: (b, i, 0)),
                 _mod_spec(layer, 2, ctx, 2),
                 full(g_post), full(conv_w), full(pool_w), full(pool_scale), full(w_branch),
                 full(w_out)]
    zz = z
    return pl.pallas_call(
        functools.partial(_merge_kernel, tm=tm, seq_len=L),
        grid=(bt, L // tm),
        in_specs=in_specs,
        out_specs=pl.BlockSpec((1, tm, D_MODEL), lambda b, i: (b, i, 0)),
        out_shape=jax.ShapeDtypeStruct((bt, L, D_MODEL), F32),
        scratch_shapes=[pltpu.VMEM((tm + 2 * HALO, BRANCH_W), F32),
                        pltpu.VMEM((tm + 2 * HALO, BRANCH_W), F32)],
        compiler_params=_params(("parallel", "arbitrary")),
        name="mixer_merge",
    )(attn, *([zz] * 13), x, mod, g_post, conv_w, pool_w, pool_scale, w_branch, w_out)


def _swiglu(h, wg_ref, wu_ref, wd_ref):
    f = None
    for c in range(D_FF // FF_CHUNK):
        cs = slice(c * FF_CHUNK, (c + 1) * FF_CHUNK)
        gt = jnp.dot(h, wg_ref[0, :, cs], preferred_element_type=F32)
        up = jnp.dot(h, wu_ref[0, :, cs], preferred_element_type=F32)
        a = (gt * jax.nn.sigmoid(gt) * up).astype(BF16)
        part = jnp.dot(a, wd_ref[0, cs, :], preferred_element_type=F32)
        f = part if f is None else f + part
    return f


def _ffn_kernel(x_ref, g_ref, sh_ref, sc_ref, gate_ref, gpost_ref, wg_ref, wu_ref, wd_ref, o_ref):
    h = (_rms(x_ref[0], g_ref[...]) * (1.0 + sc_ref[0]) + sh_ref[0]).astype(BF16)
    f = _swiglu(h, wg_ref, wu_ref, wd_ref)
    o_ref[0] = x_ref[0] + gate_ref[0] * _rms(f, gpost_ref[...])


def _ffn(x, g_pre, g_post, mod, wg, wu, wd, *, layer, ctx, tm):
    bt, L, _ = x.shape
    vec = pl.BlockSpec((1, D_MODEL), lambda b, i: (0, 0))
    return pl.pallas_call(
        _ffn_kernel,
        grid=(bt, L // tm),
        in_specs=[pl.BlockSpec((1, tm, D_MODEL), lambda b, i: (b, i, 0)),
                  vec,
                  _mod_spec(layer, 3, ctx, 2),
                  _mod_spec(layer, 4, ctx, 2),
                  _mod_spec(layer, 5, ctx, 2),
                  vec,
                  pl.BlockSpec((1, D_MODEL, D_FF), lambda b, i: (0, 0, 0)),
                  pl.BlockSpec((1, D_MODEL, D_FF), lambda b, i: (0, 0, 0)),
                  pl.BlockSpec((1, D_FF, D_MODEL), lambda b, i: (0, 0, 0))],
        out_specs=pl.BlockSpec((1, tm, D_MODEL), lambda b, i: (b, i, 0)),
        out_shape=jax.ShapeDtypeStruct((bt, L, D_MODEL), F32),
        compiler_params=_params(("parallel", "parallel")),
        name="channel_mixer",
    )(x, g_pre, mod, mod, mod, g_post, wg, wu, wd)


ROUTE_TM = 512
GROUP_TM = 512
TOP_K = 2
ROW_TILE = (8, LANES)
SC_CORES, SC_SUBCORES = 2, 16
SC_WORKERS = SC_CORES * SC_SUBCORES
SC_CHUNK = 32
RANK_LANE, PROB_LANE, EXPERT_LANE = 0, 2, 4


def _flat_mod_spec(layer, chunk, ctx, tm, seq_len):
    def idx(i):
        row = CTX_MOD_ROW if ctx else (i * tm) // seq_len
        return (layer * MOD_ROWS + row, 0, chunk)
    return pl.BlockSpec((1, 1, D_MODEL), idx)


def _lane_pick(v, lane, k):
    return jnp.sum(jnp.where(lane == k, v, 0.0), axis=-1, keepdims=True)


def _route_kernel(x_ref, g_ref, sh_ref, sc_ref, wr_ref, h_ref, info_ref, cnt_ref, run_scr):
    @pl.when(pl.program_id(0) == 0)
    def _():
        run_scr[...] = jnp.zeros_like(run_scr)

    h = _rms(x_ref[...], g_ref[...]) * (1.0 + sc_ref[0]) + sh_ref[0]
    h_ref[...] = h.reshape(h_ref.shape)
    w = wr_ref[...]
    h_hi, w_hi = h.astype(BF16), w.astype(BF16)
    h_lo = (h - h_hi.astype(F32)).astype(BF16)
    w_lo = (w - w_hi.astype(F32)).astype(BF16)
    logits = (jnp.dot(h_hi, w_hi, preferred_element_type=F32)
              + jnp.dot(h_lo, w_hi, preferred_element_type=F32)
              + jnp.dot(h_hi, w_lo, preferred_element_type=F32))
    lane = lax.broadcasted_iota(jnp.int32, logits.shape, 1)
    neg = jnp.float32(-jnp.inf)
    l1 = jnp.where(lane < N_EXPERTS, logits, neg)
    m1 = jnp.max(l1, axis=-1, keepdims=True)
    i1 = jnp.min(jnp.where(l1 == m1, lane, LANES), axis=-1, keepdims=True)
    l2 = jnp.where(lane == i1, neg, l1)
    m2 = jnp.max(l2, axis=-1, keepdims=True)
    i2 = jnp.min(jnp.where(l2 == m2, lane, LANES), axis=-1, keepdims=True)
    e2 = jnp.exp(m2 - m1)
    den = 1.0 + e2

    chosen = jnp.where(jnp.logical_or(lane == i1, lane == i2), 1.0, 0.0)
    tm = chosen.shape[0]
    earlier = (lax.broadcasted_iota(jnp.int32, (tm, tm), 1)
               < lax.broadcasted_iota(jnp.int32, (tm, tm), 0))
    before = jnp.dot(jnp.where(earlier, 1.0, 0.0).astype(BF16), chosen.astype(BF16),
                     preferred_element_type=F32)
    rank = before + run_scr[...]
    rec = [_lane_pick(rank, lane, i1), _lane_pick(rank, lane, i2), 1.0 / den, e2 / den,
           i1.astype(F32), i2.astype(F32)]
    info = jnp.zeros_like(logits)
    for k, v in enumerate(rec):
        info = jnp.where(lane == k, v, info)
    info_ref[...] = info
    run_scr[...] += jnp.sum(chosen, axis=0, keepdims=True)
    cnt_ref[...] = run_scr[...]


def _route(x2, g, mod, wr_pad, *, layer, ctx, seq_len):
    n = x2.shape[0]
    tm = ROUTE_TM
    vec = pl.BlockSpec((1, D_MODEL), lambda i: (0, 0))
    return pl.pallas_call(
        _route_kernel,
        grid=(n // tm,),
        in_specs=[pl.BlockSpec((tm, D_MODEL), lambda i: (i, 0)),
                  vec,
                  _flat_mod_spec(layer, 3, ctx, tm, seq_len),
                  _flat_mod_spec(layer, 4, ctx, tm, seq_len),
                  pl.BlockSpec((D_MODEL, LANES), lambda i: (0, 0))],
        out_specs=[pl.BlockSpec((tm,) + ROW_TILE, lambda i: (i, 0, 0)),
                   pl.BlockSpec((tm, LANES), lambda i: (i, 0)),
                   pl.BlockSpec((1, LANES), lambda i: (0, 0))],
        out_shape=[jax.ShapeDtypeStruct((n,) + ROW_TILE, F32),
                   jax.ShapeDtypeStruct((n, LANES), F32),
                   jax.ShapeDtypeStruct((1, LANES), F32)],
        scratch_shapes=[pltpu.VMEM((1, LANES), F32)],
        compiler_params=_params(("arbitrary",)),
        name="route",
    )(x2, g, mod, mod, wr_pad)


def _sc_gather(table, idx):
    n_out = idx.shape[0]
    per_worker = n_out // SC_WORKERS
    n_chunks = per_worker // SC_CHUNK
    mesh = plsc.VectorSubcoreMesh(core_axis_name="c", subcore_axis_name="s")

    @functools.partial(
        pl.kernel, mesh=mesh,
        out_type=jax.ShapeDtypeStruct((n_out,) + table.shape[1:], table.dtype),
        scratch_types=[pltpu.VMEM((per_worker,), jnp.int32),
                       pltpu.VMEM((SC_CHUNK,) + table.shape[1:], table.dtype),
                       pltpu.SemaphoreType.DMA],
        name="sc_row_gather")
    def gather(table_hbm, idx_hbm, out_hbm, idx_v, rows_v, sem):
        wid = lax.axis_index("s") * SC_CORES + lax.axis_index("c")
        base = wid * per_worker
        pltpu.sync_copy(idx_hbm.at[pl.ds(base, per_worker)], idx_v)

        @pl.loop(0, n_chunks)
        def _(c):
            off = c * SC_CHUNK
            pltpu.async_copy(table_hbm.at[idx_v.at[pl.ds(off, SC_CHUNK)]], rows_v, sem).wait()
            pltpu.sync_copy(rows_v, out_hbm.at[pl.ds(base + off, SC_CHUNK)])

    return gather(table, idx)


def _sc_scatter(rows, dest):
    n = rows.shape[0]
    n_out = dest.size
    per_worker = n_out // SC_WORKERS
    n_chunks = per_worker // SC_CHUNK
    mesh = plsc.VectorSubcoreMesh(core_axis_name="c", subcore_axis_name="s")

    @functools.partial(
        pl.kernel, mesh=mesh,
        out_type=jax.ShapeDtypeStruct((n_out,) + rows.shape[1:], rows.dtype),
        scratch_types=[pltpu.VMEM((n_chunks, SC_CHUNK), jnp.int32),
                       pltpu.VMEM((SC_CHUNK,) + rows.shape[1:], rows.dtype),
                       pltpu.SemaphoreType.DMA],
        name="sc_row_scatter")
    def scatter(rows_hbm, dest_hbm, out_hbm, dest_v, rows_v, sem):
        wid = lax.axis_index("s") * SC_CORES + lax.axis_index("c")
        src_base = lax.rem(wid * per_worker, n)
        pltpu.sync_copy(dest_hbm.at[pl.ds(wid * n_chunks, n_chunks)], dest_v)

        @pl.loop(0, n_chunks)
        def _(c):
            pltpu.sync_copy(rows_hbm.at[pl.ds(src_base + c * SC_CHUNK, SC_CHUNK)], rows_v)
            pltpu.async_copy(rows_v, out_hbm.at[dest_v.at[c]], sem).wait()

    return scatter(rows, dest)


def _group_kernel(vblk_ref, vexp_ref, vlo_ref, vhi_ref, vfirst_ref, vvalid_ref,
                  h_ref, wg_ref, wu_ref, wd_ref, y_ref):
    del vblk_ref, vexp_ref
    v = pl.program_id(0)

    @pl.when(vvalid_ref[v] == 1)
    def _():
        tg = h_ref.shape[0]
        h = h_ref[...].reshape(tg, D_MODEL).astype(BF16)
        f = _swiglu(h, wg_ref, wu_ref, wd_ref).reshape(y_ref.shape)

        @pl.when(vfirst_ref[v] == 1)
        def _():
            y_ref[...] = f

        @pl.when(vfirst_ref[v] == 0)
        def _():
            row = lax.broadcasted_iota(jnp.int32, (tg, 1, 1), 0)
            mine = jnp.logical_and(row >= vlo_ref[v], row < vhi_ref[v])
            y_ref[...] = jnp.where(mine, f, y_ref[...])


def _grouped_experts(hs, visits, wg, wu, wd):
    n_rows = hs.shape[0]
    tg = GROUP_TM
    nv = visits[0].shape[0]

    def wspec(shape):
        return pl.BlockSpec((1,) + shape, lambda v, blk, exp, *_: (exp[v], 0, 0))

    row_spec = pl.BlockSpec((tg,) + ROW_TILE, lambda v, blk, *_: (blk[v], 0, 0))
    return pl.pallas_call(
        _group_kernel,
        grid_spec=pltpu.PrefetchScalarGridSpec(
            num_scalar_prefetch=len(visits),
            grid=(nv,),
            in_specs=[row_spec, wspec((D_MODEL, D_FF)), wspec((D_MODEL, D_FF)),
                      wspec((D_FF, D_MODEL))],
            out_specs=row_spec),
        out_shape=jax.ShapeDtypeStruct((n_rows,) + ROW_TILE, F32),
        compiler_params=_params(("arbitrary",)),
        name="grouped_experts",
    )(*visits, hs, wg, wu, wd)


def _visit_tables(counts, n_rows):
    tg = GROUP_TM
    nv = n_rows // tg + N_EXPERTS - 1
    ends = jnp.cumsum(counts)
    starts = ends - counts
    first_tile = starts // tg
    last_tile = jnp.maximum(ends - 1, 0) // tg
    nvis = jnp.where(counts > 0, last_tile - first_tile + 1, 0)
    vend = jnp.cumsum(nvis)
    total = vend[-1]
    v = jnp.minimum(jnp.arange(nv, dtype=jnp.int32), total - 1)
    exp = jnp.sum(v[:, None] >= vend[None, :], axis=1).astype(jnp.int32)
    tile = first_tile[exp] + v - (vend[exp] - nvis[exp])
    lo = jnp.maximum(starts[exp], tile * tg) - tile * tg
    hi = jnp.minimum(ends[exp], (tile + 1) * tg) - tile * tg
    first = jnp.concatenate([jnp.ones((1,), jnp.int32), (tile[1:] != tile[:-1]).astype(jnp.int32)])
    valid = (jnp.arange(nv) < total).astype(jnp.int32)
    return tuple(a.astype(jnp.int32) for a in (tile, exp, lo, hi, first, valid)), starts


def _combine_kernel(x_ref, y1_ref, y2_ref, info_ref, gate_ref, gpost_ref, o_ref):
    info = info_ref[...]
    lane = lax.broadcasted_iota(jnp.int32, info.shape, 1)
    f = (_lane_pick(info, lane, PROB_LANE) * y1_ref[...].reshape(x_ref.shape)
         + _lane_pick(info, lane, PROB_LANE + 1) * y2_ref[...].reshape(x_ref.shape))
    o_ref[...] = x_ref[...] + gate_ref[0] * _rms(f, gpost_ref[...])


def _combine(x2, y12, info, mod, g_post, *, layer, ctx, seq_len):
    n = x2.shape[0]
    tm = ROUTE_TM
    nt = n // tm
    row = pl.BlockSpec((tm, D_MODEL), lambda i: (i, 0))
    return pl.pallas_call(
        _combine_kernel,
        grid=(nt,),
        in_specs=[row, pl.BlockSpec((tm,) + ROW_TILE, lambda i: (i, 0, 0)),
                  pl.BlockSpec((tm,) + ROW_TILE, lambda i: (nt + i, 0, 0)),
                  pl.BlockSpec((tm, LANES), lambda i: (i, 0)),
                  _flat_mod_spec(layer, 5, ctx, tm, seq_len),
                  pl.BlockSpec((1, D_MODEL), lambda i: (0, 0))],
        out_specs=row,
        out_shape=jax.ShapeDtypeStruct((n, D_MODEL), F32),
        compiler_params=_params(("parallel",)),
        name="combine",
    )(x2, y12, y12, info, mod, g_post)


def _moe(x, g_pre, g_post, mod, wr_pad, wg, wu, wd, *, layer, ctx):
    bt, seq_len, _ = x.shape
    n = bt * seq_len
    x2 = x.reshape(n, D_MODEL)
    h, info, cnt = _route(x2, g_pre, mod, wr_pad, layer=layer, ctx=ctx, seq_len=seq_len)

    counts = cnt[0, :N_EXPERTS].astype(jnp.int32)
    visits, starts = _visit_tables(counts, TOP_K * n)
    expert = info[:, EXPERT_LANE:EXPERT_LANE + TOP_K].astype(jnp.int32)
    rank = info[:, RANK_LANE:RANK_LANE + TOP_K].astype(jnp.int32)
    pos = (starts[expert] + rank).T.reshape(-1)
    hs = _sc_scatter(h, pos.reshape(-1, SC_CHUNK))
    ys = _grouped_experts(hs, visits, wg, wu, wd)
    y12 = _sc_gather(ys, pos)
    out = _combine(x2, y12, info, mod, g_post, layer=layer, ctx=ctx, seq_len=seq_len)
    return out.reshape(bt, seq_len, D_MODEL)


def _rope_tables(seq_len):
    rows = jnp.repeat(jnp.arange(seq_len // GRID_W), GRID_W).astype(F32)
    cols = jnp.tile(jnp.arange(GRID_W), seq_len // GRID_W).astype(F32)
    inv = ROPE_BASE ** (-jnp.arange(ROPE_FREQS, dtype=F32) / ROPE_FREQS)
    ang = jnp.concatenate([rows[:, None] * inv, rows[:, None] * inv,
                           cols[:, None] * inv, cols[:, None] * inv], axis=1)
    ang = jnp.tile(ang, (1, LANES // HEAD_QK))
    cos, sin = jnp.cos(ang), jnp.sin(ang)
    low = (jnp.arange(LANES) % (2 * ROPE_FREQS)) < ROPE_FREQS
    return cos, jnp.where(low, -sin, 0.0), jnp.where(low, 0.0, sin)


def kernel(x, c, ctx, c_ctx, w_mod, b_mod, g_pre_mix, g_post_mix, g_pre_ffn, g_post_ffn,
           w_in, lambda_q1, lambda_k1, lambda_q2, lambda_k2, g_subln, conv_w, pool_w,
           pool_scale, w_branch, w_out, ffn_w_gate, ffn_w_up, ffn_w_down, router_w,
           moe_w_gate, moe_w_up, moe_w_down):
    depth = w_in.shape[0]
    bsz, seq, _ = x.shape
    ctx_len = ctx.shape[1]

    cvec = jnp.zeros((MOD_ROWS, D_MODEL), F32).at[:bsz].set(c).at[CTX_MOD_ROW].set(c_ctx)
    mod = _modulation(cvec, w_mod, b_mod).reshape(depth * MOD_ROWS, 1, 6 * D_MODEL)
    tables = _rope_tables(seq)
    no_tables = tuple(t[:ctx_len] for t in tables)

    y = ctx
    for i in range(depth):
        last = i == depth - 1
        lam_init = 0.8 - 0.6 * math.exp(-0.3 * i)
        split = w_in.shape[2] - GATES_W
        w_in_i = jnp.concatenate([w_in[i, :, split:], w_in[i, :, :split]], axis=1).astype(BF16)
        lam_vecs = jnp.stack([lambda_q1[i], lambda_k1[i], lambda_q2[i], lambda_k2[i]]).astype(F32)
        g_sub = g_subln[i].reshape(1, HEAD_V)
        g_pm, g_qm = g_pre_mix[i].reshape(1, D_MODEL), g_post_mix[i].reshape(1, D_MODEL)
        g_pf, g_qf = g_pre_ffn[i].reshape(1, D_MODEL), g_post_ffn[i].reshape(1, D_MODEL)
        mix_w = (conv_w[i], pool_w[i].astype(BF16), pool_scale[i].reshape(1, BRANCH_W),
                 w_branch[i].astype(BF16), w_out[i].astype(BF16))

        z = _inproj(x, g_pm, mod, w_in_i, tables, layer=i, ctx=False, rope=True,
                    col_start=0, tm=512)
        kv_full = (K_COL * HEAD_BLKS, V_COL * HEAD_BLKS)
        if last:
            w_kv = w_in_i[:, K_COL * COL_BLK:(V_COL + 1) * COL_BLK]
            zc = _inproj(y, g_pm, mod, w_kv, no_tables, layer=i, ctx=True, rope=False,
                         col_start=K_COL, tm=ctx_len)
            segs = [(z,) + kv_full, (zc, 0, HEAD_BLKS)]
        else:
            zc = _inproj(y, g_pm, mod, w_in_i, no_tables, layer=i, ctx=True, rope=False,
                         col_start=0, tm=ctx_len)
            segs = [(z,) + kv_full, (zc,) + kv_full]
            attn_c = _attention(zc, [(zc,) + kv_full], lam_vecs, g_sub, lam_init=lam_init,
                                tq=ctx_len)
            y = _merge(attn_c, zc, y, mod, g_qm, *mix_w, layer=i, ctx=True, tm=ctx_len)
        attn_l = _attention(z, segs, lam_vecs, g_sub, lam_init=lam_init, tq=1024)
        x = _merge(attn_l, z, x, mod, g_qm, *mix_w, layer=i, ctx=False, tm=512)

        j = i // 2
        streams = [(x, False, 512)] + ([] if last else [(y, True, ctx_len)])
        if i % 2 == 0:
            ffn_w = (ffn_w_gate[j:j + 1].astype(BF16), ffn_w_up[j:j + 1].astype(BF16),
                     ffn_w_down[j:j + 1].astype(BF16))
            outs = [_ffn(t, g_pf, g_qf, mod, *ffn_w, layer=i, ctx=is_ctx, tm=tm)
                    for t, is_ctx, tm in streams]
        else:
            ffn_w = (moe_w_gate[j].astype(BF16), moe_w_up[j].astype(BF16), moe_w_down[j].astype(BF16))
            wr_pad = jnp.zeros((D_MODEL, LANES), F32).at[:, :N_EXPERTS].set(router_w[j])
            outs = [_moe(t, g_pf, g_qf, mod, wr_pad, *ffn_w, layer=i, ctx=is_ctx)
                    for t, is_ctx, _ in streams]
        x = outs[0]
        if not last:
            y = outs[1]
    return x
```

```python
import functools
import math

import jax
import jax.numpy as jnp
from jax import lax
from jax.experimental import pallas as pl
from jax.experimental.pallas import tpu as pltpu
from jax.experimental.pallas import tpu_sc as plsc

F32 = jnp.float32
BF16 = jnp.bfloat16

D_MODEL = 1024
GRID_W = 64
N_HEADS = 4
HEAD_QK = 64
HEAD_V = 128
ROPE_BASE = 10000.0
ROPE_FREQS = HEAD_QK // 4
CONV_K = 3
POOL_WINDOWS = (2, 4, 8, 16)
POOL_GW = 128
N_BRANCH = 3
BRANCH_W = 512
GATES_W = N_BRANCH * D_MODEL
IN_W = GATES_W + 7 * BRANCH_W
D_FF = 2816
N_EXPERTS = 8
EPS = 1e-6

LANES = 128
BF16_SUBLANES = 16
COL_BLK = 512
Q_COL, K_COL, V_COL, CB_COL, CC_COL, CX_COL, PIN_COL = range(6, 13)
HEAD_BLKS = COL_BLK // 128
MOD_ROWS = 16
CTX_MOD_ROW = 8
HALO = BF16_SUBLANES
FF_CHUNK = 256
KEY_CHUNK = 1024
LOG2_E = 1.4426950408889634
VMEM_LIMIT = 56 * 1024 * 1024


def _params(sem, vmem=VMEM_LIMIT):
    return pltpu.CompilerParams(dimension_semantics=sem, vmem_limit_bytes=vmem)


def _rms(t, g):
    return t * lax.rsqrt(jnp.mean(t * t, axis=-1, keepdims=True) + EPS) * g


def _mod_kernel(c_ref, w_ref, b_ref, o_ref):
    c = c_ref[...]
    s = c * jax.nn.sigmoid(c)
    o_ref[0] = jnp.dot(s, w_ref[0], preferred_element_type=F32,
                       precision=lax.Precision.HIGHEST) + b_ref[0]


def _modulation(cvec, w_mod, b_mod):
    depth = w_mod.shape[0]
    wcols = w_mod.shape[2]
    tn = 1536
    return pl.pallas_call(
        _mod_kernel,
        grid=(depth, wcols // tn),
        in_specs=[pl.BlockSpec((MOD_ROWS, D_MODEL), lambda l, j: (0, 0)),
                  pl.BlockSpec((1, D_MODEL, tn), lambda l, j: (l, 0, j)),
                  pl.BlockSpec((1, 1, tn), lambda l, j: (l, 0, j))],
        out_specs=pl.BlockSpec((1, MOD_ROWS, tn), lambda l, j: (l, 0, j)),
        out_shape=jax.ShapeDtypeStruct((depth, MOD_ROWS, wcols), F32),
        compiler_params=_params(("parallel", "parallel")),
        name="modulation",
    )(cvec, w_mod, b_mod.reshape(depth, 1, wcols))


def _mod_spec(layer, chunk, ctx, ngrid):
    def idx(*g):
        row = CTX_MOD_ROW if ctx else g[0]
        return (layer * MOD_ROWS + row, 0, chunk)
    del ngrid
    return pl.BlockSpec((1, 1, D_MODEL), idx)


def _inproj_kernel(x_ref, g_ref, sh_ref, sc_ref, w_ref, cos_ref, sa_ref, sb_ref, o_ref,
                   *, rope, col_start):
    h = (_rms(x_ref[0], g_ref[...]) * (1.0 + sc_ref[0]) + sh_ref[0]).astype(BF16)
    for j in range(w_ref.shape[1] // COL_BLK):
        z = jnp.dot(h, w_ref[:, j * COL_BLK:(j + 1) * COL_BLK], preferred_element_type=F32)
        if rope and j + col_start in (Q_COL, K_COL):
            cos, sa, sb = cos_ref[...], sa_ref[...], sb_ref[...]
            for c in range(COL_BLK // LANES):
                t = z[:, c * LANES:(c + 1) * LANES]
                r = (t * cos + pltpu.roll(t, LANES - ROPE_FREQS, 1) * sa
                     + pltpu.roll(t, ROPE_FREQS, 1) * sb)
                o_ref[0, :, j * COL_BLK + c * LANES:j * COL_BLK + (c + 1) * LANES] = r.astype(BF16)
        else:
            o_ref[0, :, j * COL_BLK:(j + 1) * COL_BLK] = z.astype(BF16)


def _resident(shape):
    return pl.BlockSpec(shape, lambda *_: (0,) * len(shape), pipeline_mode=pl.Buffered(1))


def _inproj(x, g, mod, w, tables, *, layer, ctx, rope, col_start, tm):
    bt, L, _ = x.shape
    cos, sa, sb = tables
    tab_spec = pl.BlockSpec((tm, LANES), lambda b, i: (i, 0))
    return pl.pallas_call(
        functools.partial(_inproj_kernel, rope=rope, col_start=col_start),
        grid=(bt, L // tm),
        in_specs=[pl.BlockSpec((1, tm, D_MODEL), lambda b, i: (b, i, 0)),
                  _resident((1, D_MODEL)),
                  _mod_spec(layer, 0, ctx, 2),
                  _mod_spec(layer, 1, ctx, 2),
                  _resident(w.shape),
                  tab_spec, tab_spec, tab_spec],
        out_specs=pl.BlockSpec((1, tm, w.shape[1]), lambda b, i: (b, i, 0)),
        out_shape=jax.ShapeDtypeStruct((bt, L, w.shape[1]), BF16),
        compiler_params=_params(("parallel", "parallel")),
        name="inproj",
    )(x, g, mod, mod, w, cos, sa, sb)


def _attn_kernel(*refs, nseg, lam_init):
    q_ref, lam_ref, gs_ref = refs[0], refs[1], refs[2]
    kv_refs = refs[3:3 + 2 * nseg]
    o_ref = refs[3 + 2 * nseg]

    lv = lam_ref[...]
    lam = (jnp.exp(jnp.sum(lv[0:1] * lv[1:2], axis=-1, keepdims=True))
           - jnp.exp(jnp.sum(lv[2:3] * lv[3:4], axis=-1, keepdims=True)) + lam_init)

    q = q_ref[0]
    tq = q.shape[0]
    lane = lax.broadcasted_iota(jnp.int32, q.shape, 1)
    qs = (q.astype(F32) * (HEAD_QK ** -0.5 * LOG2_E)).astype(BF16)
    qmap = [jnp.where(lane < HEAD_QK, qs, jnp.zeros_like(qs)),
            jnp.where(lane >= HEAD_QK, qs, jnp.zeros_like(qs))]

    m = [jnp.full((tq, 1), -jnp.inf, F32) for _ in range(2)]
    acc = [jnp.zeros((tq, 2 * HEAD_V), F32) for _ in range(2)]
    dn = (((1,), (1,)), ((), ()))
    for s in range(nseg):
        k_ref, v_ref = kv_refs[2 * s], kv_refs[2 * s + 1]
        n_keys = k_ref.shape[1]
        for c0 in range(0, n_keys, KEY_CHUNK):
            ck = min(KEY_CHUNK, n_keys - c0)
            k = k_ref[0, c0:c0 + ck, :]
            one_col = jnp.where(lax.broadcasted_iota(jnp.int32, (ck, HEAD_V), 1) == 0, 1.0, 0.0)
            v_aug = jnp.concatenate([v_ref[0, c0:c0 + ck, :], one_col.astype(BF16)], axis=1)
            for j in range(2):
                sc = lax.dot_general(qmap[j], k, dn, preferred_element_type=F32)
                m_new = jnp.maximum(m[j], jnp.max(sc, axis=-1, keepdims=True))
                p = jnp.exp2(sc - m_new).astype(BF16)
                acc[j] = acc[j] * jnp.exp2(m[j] - m_new) + jnp.dot(p, v_aug,
                                                                  preferred_element_type=F32)
                m[j] = m_new
    o = (acc[0][:, :HEAD_V] / acc[0][:, HEAD_V:HEAD_V + 1]
         - acc[1][:, :HEAD_V] * (lam / acc[1][:, HEAD_V:HEAD_V + 1]))
    o_ref[0] = (_rms(o, gs_ref[...]) * (1.0 - lam_init)).astype(BF16)


def _attention(zq, segs, lam_vecs, g_sub, *, lam_init, tq):
    bt, lq, _ = zq.shape
    q0 = Q_COL * HEAD_BLKS
    in_specs = [pl.BlockSpec((1, tq, HEAD_V), lambda b, h, i: (b, i, q0 + h)),
                pl.BlockSpec((4, HEAD_QK), lambda b, h, i: (0, 0)),
                pl.BlockSpec((1, HEAD_V), lambda b, h, i: (0, 0))]
    args = [zq, lam_vecs, g_sub]
    for arr, kc, vc in segs:
        t = arr.shape[1]
        in_specs.append(pl.BlockSpec((1, t, HEAD_V), lambda b, h, i, kc=kc: (b, 0, kc + h)))
        in_specs.append(pl.BlockSpec((1, t, HEAD_V), lambda b, h, i, vc=vc: (b, 0, vc + h)))
        args += [arr, arr]
    return pl.pallas_call(
        functools.partial(_attn_kernel, nseg=len(segs), lam_init=lam_init),
        grid=(bt, N_HEADS, lq // tq),
        in_specs=in_specs,
        out_specs=pl.BlockSpec((1, tq, HEAD_V), lambda b, h, i: (b, i, h)),
        out_shape=jax.ShapeDtypeStruct((bt, lq, N_HEADS * HEAD_V), BF16),
        compiler_params=_params(("parallel", "parallel", "arbitrary")),
        name="diff_attention",
    )(*args)


def _merge_kernel(attn_ref, cb_ref, cc_ref, cx_ref, pin_ref, g0_ref, g1_ref, g2_ref,
                  ccp_ref, cxp_ref, pinp_ref, ccn_ref, cxn_ref, pinn_ref,
                  x_ref, gate_ref, gpost_ref, cw_ref, pw_ref, ps_ref, wb_ref, wo_ref,
                  o_ref, u_scr, p_scr, *, tm, seq_len):
    i = pl.program_id(1)
    first = i == 0
    last = i == pl.num_programs(1) - 1

    def f32(ref):
        return ref[0].astype(F32)

    u_scr[0:HALO] = jnp.where(first, 0.0, f32(ccp_ref) * f32(cxp_ref))
    u_scr[HALO:HALO + tm] = f32(cc_ref) * f32(cx_ref)
    u_scr[HALO + tm:2 * HALO + tm] = jnp.where(last, 0.0, f32(ccn_ref) * f32(cxn_ref))
    p_scr[0:HALO] = jnp.where(first, 0.0, f32(pinp_ref))
    p_scr[HALO:HALO + tm] = f32(pin_ref)
    p_scr[HALO + tm:2 * HALO + tm] = jnp.where(last, 0.0, f32(pinn_ref))

    cw = cw_ref[...]
    conv = (cw[0:1] * u_scr[HALO - 1:HALO - 1 + tm] + cw[1:2] * u_scr[HALO:HALO + tm]
            + cw[2:3] * u_scr[HALO + 1:HALO + 1 + tm])
    conv_o = (f32(cb_ref) * conv).astype(BF16)

    t = i * tm + lax.broadcasted_iota(jnp.int32, (tm, 1), 0)
    pool_parts = []
    for g, w in enumerate(POOL_WINDOWS):
        cs = slice(g * POOL_GW, (g + 1) * POOL_GW)
        acc = p_scr[HALO - w // 2:HALO - w // 2 + tm, cs]
        for d in range(-w // 2 + 1, w // 2):
            acc = acc + p_scr[HALO + d:HALO + d + tm, cs]
        lo = jnp.maximum(t - w // 2, 0)
        hi = jnp.minimum(t - w // 2 + w, seq_len)
        p = acc / (hi - lo).astype(F32) - p_scr[HALO:HALO + tm, cs]
        pool_parts.append(jnp.dot(p.astype(BF16), pw_ref[g], preferred_element_type=F32))
    pool_o = (jnp.concatenate(pool_parts, axis=-1) * ps_ref[...]).astype(BF16)

    def gate2(ref):
        return 1.0 + jnp.tanh(f32(ref))

    merged2 = (gate2(g0_ref) * jnp.dot(attn_ref[0], wb_ref[0], preferred_element_type=F32)
               + gate2(g1_ref) * jnp.dot(conv_o, wb_ref[1], preferred_element_type=F32)
               + gate2(g2_ref) * jnp.dot(pool_o, wb_ref[2], preferred_element_type=F32))
    mix = jnp.dot(merged2.astype(BF16), wo_ref[...], preferred_element_type=F32)
    o_ref[0] = x_ref[0] + gate_ref[0] * _rms(mix, gpost_ref[...])


def _merge(attn, z, x, mod, g_post, conv_w, pool_w, pool_scale, w_branch, w_out, *, layer, ctx, tm):
    bt, L, _ = x.shape
    hb = tm // HALO
    nhb = L // HALO

    def col(c, width=COL_BLK):
        return pl.BlockSpec((1, tm, width), lambda b, i, c=c: (b, i, c))

    def prev(c):
        return pl.BlockSpec((1, HALO, COL_BLK), lambda b, i, c=c: (b, jnp.maximum(i * hb - 1, 0), c))

    def nxt(c):
        return pl.BlockSpec((1, HALO, COL_BLK),
                            lambda b, i, c=c: (b, jnp.minimum((i + 1) * hb, nhb - 1), c))

    def full(a):
        return pl.BlockSpec(a.shape, lambda b, i, n=a.ndim: (0,) * n)

    in_specs = [pl.BlockSpec((1, tm, BRANCH_W), lambda b, i: (b, i, 0)),
                col(CB_COL), col(CC_COL), col(CX_COL), col(PIN_COL)]
    in_specs += [col(k, D_MODEL) for k in range(N_BRANCH)]
    in_specs += [prev(CC_COL), prev(CX_COL), prev(PIN_COL), nxt(CC_COL), nxt(CX_COL), nxt(PIN_COL)]
    in_specs += [pl.BlockSpec((1, tm, D_MODEL), lambda b, i: (b, i, 0)),
                 _mod_spec(layer, 2, ctx, 2),
                 full(g_post), full(conv_w), full(pool_w), full(pool_scale), full(w_branch),
                 full(w_out)]
    zz = z
    return pl.pallas_call(
        functools.partial(_merge_kernel, tm=tm, seq_len=L),
        grid=(bt, L // tm),
        in_specs=in_specs,
        out_specs=pl.BlockSpec((1, tm, D_MODEL), lambda b, i: (b, i, 0)),
        out_shape=jax.ShapeDtypeStruct((bt, L, D_MODEL), F32),
        scratch_shapes=[pltpu.VMEM((tm + 2 * HALO, BRANCH_W), F32),
                        pltpu.VMEM((tm + 2 * HALO, BRANCH_W), F32)],
        compiler_params=_params(("parallel", "arbitrary")),
        name="mixer_merge",
    )(attn, *([zz] * 13), x, mod, g_post, conv_w, pool_w, pool_scale, w_branch, w_out)


def _swiglu(h, wg_ref, wu_ref, wd_ref):
    f = None
    for c in range(D_FF // FF_CHUNK):
        cs = slice(c * FF_CHUNK, (c + 1) * FF_CHUNK)
        gt = jnp.dot(h, wg_ref[0, :, cs], preferred_element_type=F32)
        up = jnp.dot(h, wu_ref[0, :, cs], preferred_element_type=F32)
        a = (gt * jax.nn.sigmoid(gt) * up).astype(BF16)
        part = jnp.dot(a, wd_ref[0, cs, :], preferred_element_type=F32)
        f = part if f is None else f + part
    return f


def _ffn_kernel(x_ref, g_ref, sh_ref, sc_ref, gate_ref, gpost_ref, wg_ref, wu_ref, wd_ref, o_ref):
    h = (_rms(x_ref[0], g_ref[...]) * (1.0 + sc_ref[0]) + sh_ref[0]).astype(BF16)
    f = _swiglu(h, wg_ref, wu_ref, wd_ref)
    o_ref[0] = x_ref[0] + gate_ref[0] * _rms(f, gpost_ref[...])


def _ffn(x, g_pre, g_post, mod, wg, wu, wd, *, layer, ctx, tm):
    bt, L, _ = x.shape
    vec = pl.BlockSpec((1, D_MODEL), lambda b, i: (0, 0))
    return pl.pallas_call(
        _ffn_kernel,
        grid=(bt, L // tm),
        in_specs=[pl.BlockSpec((1, tm, D_MODEL), lambda b, i: (b, i, 0)),
                  vec,
                  _mod_spec(layer, 3, ctx, 2),
                  _mod_spec(layer, 4, ctx, 2),
                  _mod_spec(layer, 5, ctx, 2),
                  vec,
                  _resident(wg.shape), _resident(wu.shape), _resident(wd.shape)],
        out_specs=pl.BlockSpec((1, tm, D_MODEL), lambda b, i: (b, i, 0)),
        out_shape=jax.ShapeDtypeStruct((bt, L, D_MODEL), F32),
        compiler_params=_params(("parallel", "parallel")),
        name="channel_mixer",
    )(x, g_pre, mod, mod, mod, g_post, wg, wu, wd)


ROUTE_TM = 512
GROUP_TM = 512
TOP_K = 2
MOE_PARTS = 2
ROW_TILE = (8, LANES)
SC_CORES, SC_SUBCORES = 2, 16
SC_WORKERS = SC_CORES * SC_SUBCORES
SC_CHUNK = 32
RANK_LANE, PROB_LANE, EXPERT_LANE = 0, 2, 4


def _flat_mod_spec(layer, chunk, ctx, tm, seq_len, tile0=0):
    def idx(i):
        row = CTX_MOD_ROW if ctx else ((i + tile0) * tm) // seq_len
        return (layer * MOD_ROWS + row, 0, chunk)
    return pl.BlockSpec((1, 1, D_MODEL), idx)


def _lane_pick(v, lane, k):
    return jnp.sum(jnp.where(lane == k, v, 0.0), axis=-1, keepdims=True)


def _route_kernel(x_ref, g_ref, sh_ref, sc_ref, wr_ref, h_ref, info_ref, cnt_ref, run_scr):
    @pl.when(pl.program_id(0) == 0)
    def _():
        run_scr[...] = jnp.zeros_like(run_scr)

    h = _rms(x_ref[...], g_ref[...]) * (1.0 + sc_ref[0]) + sh_ref[0]
    h_ref[...] = h.reshape(h_ref.shape)
    w = wr_ref[...]
    h_hi, w_hi = h.astype(BF16), w.astype(BF16)
    h_lo = (h - h_hi.astype(F32)).astype(BF16)
    w_lo = (w - w_hi.astype(F32)).astype(BF16)
    logits = (jnp.dot(h_hi, w_hi, preferred_element_type=F32)
              + jnp.dot(h_lo, w_hi, preferred_element_type=F32)
              + jnp.dot(h_hi, w_lo, preferred_element_type=F32))
    lane = lax.broadcasted_iota(jnp.int32, logits.shape, 1)
    neg = jnp.float32(-jnp.inf)
    l1 = jnp.where(lane < N_EXPERTS, logits, neg)
    m1 = jnp.max(l1, axis=-1, keepdims=True)
    i1 = jnp.min(jnp.where(l1 == m1, lane, LANES), axis=-1, keepdims=True)
    l2 = jnp.where(lane == i1, neg, l1)
    m2 = jnp.max(l2, axis=-1, keepdims=True)
    i2 = jnp.min(jnp.where(l2 == m2, lane, LANES), axis=-1, keepdims=True)
    e2 = jnp.exp(m2 - m1)
    den = 1.0 + e2

    chosen = jnp.where(jnp.logical_or(lane == i1, lane == i2), 1.0, 0.0)
    tm = chosen.shape[0]
    earlier = (lax.broadcasted_iota(jnp.int32, (tm, tm), 1)
               < lax.broadcasted_iota(jnp.int32, (tm, tm), 0))
    before = jnp.dot(jnp.where(earlier, 1.0, 0.0).astype(BF16), chosen.astype(BF16),
                     preferred_element_type=F32)
    rank = before + run_scr[...]
    rec = [_lane_pick(rank, lane, i1), _lane_pick(rank, lane, i2), 1.0 / den, e2 / den,
           i1.astype(F32), i2.astype(F32)]
    info = jnp.zeros_like(logits)
    for k, v in enumerate(rec):
        info = jnp.where(lane == k, v, info)
    info_ref[...] = info
    run_scr[...] += jnp.sum(chosen, axis=0, keepdims=True)
    cnt_ref[...] = run_scr[...]


def _route(x2, g, mod, wr_pad, *, layer, ctx, seq_len, tile0, n):
    tm = ROUTE_TM
    vec = pl.BlockSpec((1, D_MODEL), lambda i: (0, 0))
    return pl.pallas_call(
        _route_kernel,
        grid=(n // tm,),
        in_specs=[pl.BlockSpec((tm, D_MODEL), lambda i: (i + tile0, 0)),
                  vec,
                  _flat_mod_spec(layer, 3, ctx, tm, seq_len, tile0),
                  _flat_mod_spec(layer, 4, ctx, tm, seq_len, tile0),
                  pl.BlockSpec((D_MODEL, LANES), lambda i: (0, 0))],
        out_specs=[pl.BlockSpec((tm,) + ROW_TILE, lambda i: (i, 0, 0)),
                   pl.BlockSpec((tm, LANES), lambda i: (i, 0)),
                   pl.BlockSpec((1, LANES), lambda i: (0, 0))],
        out_shape=[jax.ShapeDtypeStruct((n,) + ROW_TILE, F32),
                   jax.ShapeDtypeStruct((n, LANES), F32),
                   jax.ShapeDtypeStruct((1, LANES), F32)],
        scratch_shapes=[pltpu.VMEM((1, LANES), F32)],
        compiler_params=_params(("arbitrary",)),
        name="route",
    )(x2, g, mod, mod, wr_pad)


def _sc_gather(table, idx):
    n_out = idx.shape[0]
    per_worker = n_out // SC_WORKERS
    n_chunks = per_worker // SC_CHUNK
    mesh = plsc.VectorSubcoreMesh(core_axis_name="c", subcore_axis_name="s")

    @functools.partial(
        pl.kernel, mesh=mesh,
        out_type=jax.ShapeDtypeStruct((n_out,) + table.shape[1:], table.dtype),
        scratch_types=[pltpu.VMEM((per_worker,), jnp.int32),
                       pltpu.VMEM((SC_CHUNK,) + table.shape[1:], table.dtype),
                       pltpu.SemaphoreType.DMA],
        name="sc_row_gather")
    def gather(table_hbm, idx_hbm, out_hbm, idx_v, rows_v, sem):
        wid = lax.axis_index("s") * SC_CORES + lax.axis_index("c")
        base = wid * per_worker
        pltpu.sync_copy(idx_hbm.at[pl.ds(base, per_worker)], idx_v)

        @pl.loop(0, n_chunks)
        def _(c):
            off = c * SC_CHUNK
            pltpu.async_copy(table_hbm.at[idx_v.at[pl.ds(off, SC_CHUNK)]], rows_v, sem).wait()
            pltpu.sync_copy(rows_v, out_hbm.at[pl.ds(base + off, SC_CHUNK)])

    return gather(table, idx)


def _sc_scatter(rows, dest):
    n = rows.shape[0]
    n_out = dest.size
    per_worker = n_out // SC_WORKERS
    n_chunks = per_worker // SC_CHUNK
    mesh = plsc.VectorSubcoreMesh(core_axis_name="c", subcore_axis_name="s")

    @functools.partial(
        pl.kernel, mesh=mesh,
        out_type=jax.ShapeDtypeStruct((n_out,) + rows.shape[1:], rows.dtype),
        scratch_types=[pltpu.VMEM((n_chunks, SC_CHUNK), jnp.int32),
                       pltpu.VMEM((SC_CHUNK,) + rows.shape[1:], rows.dtype),
                       pltpu.SemaphoreType.DMA],
        name="sc_row_scatter")
    def scatter(rows_hbm, dest_hbm, out_hbm, dest_v, rows_v, sem):
        wid = lax.axis_index("s") * SC_CORES + lax.axis_index("c")
        src_base = lax.rem(wid * per_worker, n)
        pltpu.sync_copy(dest_hbm.at[pl.ds(wid * n_chunks, n_chunks)], dest_v)

        @pl.loop(0, n_chunks)
        def _(c):
            pltpu.sync_copy(rows_hbm.at[pl.ds(src_base + c * SC_CHUNK, SC_CHUNK)], rows_v)
            pltpu.async_copy(rows_v, out_hbm.at[dest_v.at[c]], sem).wait()

    return scatter(rows, dest)


def _group_kernel(vblk_ref, vexp_ref, vlo_ref, vhi_ref, vfirst_ref, vvalid_ref,
                  h_ref, wg_ref, wu_ref, wd_ref, y_ref):
    del vblk_ref, vexp_ref
    v = pl.program_id(0)

    @pl.when(vvalid_ref[v] == 1)
    def _():
        tg = h_ref.shape[0]
        h = h_ref[...].reshape(tg, D_MODEL).astype(BF16)
        f = _swiglu(h, wg_ref, wu_ref, wd_ref).reshape(y_ref.shape)

        @pl.when(vfirst_ref[v] == 1)
        def _():
            y_ref[...] = f

        @pl.when(vfirst_ref[v] == 0)
        def _():
            row = lax.broadcasted_iota(jnp.int32, (tg, 1, 1), 0)
            mine = jnp.logical_and(row >= vlo_ref[v], row < vhi_ref[v])
            y_ref[...] = jnp.where(mine, f, y_ref[...])


def _grouped_experts(hs, visits, wg, wu, wd):
    n_rows = hs.shape[0]
    tg = GROUP_TM
    nv = visits[0].shape[0]

    def wspec(shape):
        return pl.BlockSpec((1,) + shape, lambda v, blk, exp, *_: (exp[v], 0, 0))

    row_spec = pl.BlockSpec((tg,) + ROW_TILE, lambda v, blk, *_: (blk[v], 0, 0))
    return pl.pallas_call(
        _group_kernel,
        grid_spec=pltpu.PrefetchScalarGridSpec(
            num_scalar_prefetch=len(visits),
            grid=(nv,),
            in_specs=[row_spec, wspec((D_MODEL, D_FF)), wspec((D_MODEL, D_FF)),
                      wspec((D_FF, D_MODEL))],
            out_specs=row_spec),
        out_shape=jax.ShapeDtypeStruct((n_rows,) + ROW_TILE, F32),
        compiler_params=_params(("arbitrary",)),
        name="grouped_experts",
    )(*visits, hs, wg, wu, wd)


def _visit_tables(counts, n_rows):
    tg = GROUP_TM
    nv = n_rows // tg + N_EXPERTS - 1
    ends = jnp.cumsum(counts)
    starts = ends - counts
    first_tile = starts // tg
    last_tile = jnp.maximum(ends - 1, 0) // tg
    nvis = jnp.where(counts > 0, last_tile - first_tile + 1, 0)
    vend = jnp.cumsum(nvis)
    total = vend[-1]
    v = jnp.minimum(jnp.arange(nv, dtype=jnp.int32), total - 1)
    exp = jnp.sum(v[:, None] >= vend[None, :], axis=1).astype(jnp.int32)
    tile = first_tile[exp] + v - (vend[exp] - nvis[exp])
    lo = jnp.maximum(starts[exp], tile * tg) - tile * tg
    hi = jnp.minimum(ends[exp], (tile + 1) * tg) - tile * tg
    first = jnp.concatenate([jnp.ones((1,), jnp.int32), (tile[1:] != tile[:-1]).astype(jnp.int32)])
    valid = (jnp.arange(nv) < total).astype(jnp.int32)
    return tuple(a.astype(jnp.int32) for a in (tile, exp, lo, hi, first, valid)), starts


def _combine_kernel(x_ref, y1_ref, y2_ref, info_ref, gate_ref, gpost_ref, o_ref):
    info = info_ref[...]
    lane = lax.broadcasted_iota(jnp.int32, info.shape, 1)
    f = (_lane_pick(info, lane, PROB_LANE) * y1_ref[...].reshape(x_ref.shape)
         + _lane_pick(info, lane, PROB_LANE + 1) * y2_ref[...].reshape(x_ref.shape))
    o_ref[...] = x_ref[...] + gate_ref[0] * _rms(f, gpost_ref[...])


def _combine_into_kernel(x_ref, y1_ref, y2_ref, info_ref, gate_ref, gpost_ref, prev_ref, o_ref):
    del prev_ref
    _combine_kernel(x_ref, y1_ref, y2_ref, info_ref, gate_ref, gpost_ref, o_ref)


def _combine(x2, y12, info, mod, g_post, prev, *, layer, ctx, seq_len, tile0):
    n = x2.shape[0]
    tm = ROUTE_TM
    nt = info.shape[0] // tm
    row = pl.BlockSpec((tm, D_MODEL), lambda i: (i + tile0, 0))
    in_specs = [row, pl.BlockSpec((tm,) + ROW_TILE, lambda i: (i, 0, 0)),
                pl.BlockSpec((tm,) + ROW_TILE, lambda i: (nt + i, 0, 0)),
                pl.BlockSpec((tm, LANES), lambda i: (i, 0)),
                _flat_mod_spec(layer, 5, ctx, tm, seq_len, tile0),
                pl.BlockSpec((1, D_MODEL), lambda i: (0, 0))]
    args = [x2, y12, y12, info, mod, g_post]
    if prev is not None:
        in_specs.append(pl.BlockSpec(memory_space=pl.ANY))
        args.append(prev)
    return pl.pallas_call(
        _combine_kernel if prev is None else _combine_into_kernel,
        grid=(nt,),
        in_specs=in_specs,
        out_specs=row,
        out_shape=jax.ShapeDtypeStruct((n, D_MODEL), F32),
        input_output_aliases={} if prev is None else {len(args) - 1: 0},
        compiler_params=_params(("parallel",)),
        name="combine",
    )(*args)


def _moe(x, g_pre, g_post, mod, wr_pad, wg, wu, wd, *, layer, ctx):
    bt, seq_len, _ = x.shape
    n = bt * seq_len
    x2 = x.reshape(n, D_MODEL)
    parts = MOE_PARTS if n % (MOE_PARTS * SC_WORKERS * SC_CHUNK) == 0 else 1
    n_part = n // parts
    tiles_part = n_part // ROUTE_TM

    routed = [_route(x2, g_pre, mod, wr_pad, layer=layer, ctx=ctx, seq_len=seq_len,
                     tile0=p * tiles_part, n=n_part) for p in range(parts)]
    sorted_rows = []
    for h, info, cnt in routed:
        counts = cnt[0, :N_EXPERTS].astype(jnp.int32)
        visits, starts = _visit_tables(counts, TOP_K * n_part)
        expert = info[:, EXPERT_LANE:EXPERT_LANE + TOP_K].astype(jnp.int32)
        rank = info[:, RANK_LANE:RANK_LANE + TOP_K].astype(jnp.int32)
        pos = (starts[expert] + rank).T.reshape(-1)
        sorted_rows.append((_sc_scatter(h, pos.reshape(-1, SC_CHUNK)), visits, pos))
    gathered = [_sc_gather(_grouped_experts(hs, visits, wg, wu, wd), pos)
                for hs, visits, pos in sorted_rows]
    out = None
    for p, (y12, (_, info, _)) in enumerate(zip(gathered, routed)):
        out = _combine(x2, y12, info, mod, g_post, out, layer=layer, ctx=ctx, seq_len=seq_len,
                       tile0=p * tiles_part)
    return out.reshape(bt, seq_len, D_MODEL)


def _rope_tables(seq_len):
    rows = jnp.repeat(jnp.arange(seq_len // GRID_W), GRID_W).astype(F32)
    cols = jnp.tile(jnp.arange(GRID_W), seq_len // GRID_W).astype(F32)
    inv = ROPE_BASE ** (-jnp.arange(ROPE_FREQS, dtype=F32) / ROPE_FREQS)
    ang = jnp.concatenate([rows[:, None] * inv, rows[:, None] * inv,
                           cols[:, None] * inv, cols[:, None] * inv], axis=1)
    ang = jnp.tile(ang, (1, LANES // HEAD_QK))
    cos, sin = jnp.cos(ang), jnp.sin(ang)
    low = (jnp.arange(LANES) % (2 * ROPE_FREQS)) < ROPE_FREQS
    return cos, jnp.where(low, -sin, 0.0), jnp.where(low, 0.0, sin)


def kernel(x, c, ctx, c_ctx, w_mod, b_mod, g_pre_mix, g_post_mix, g_pre_ffn, g_post_ffn,
           w_in, lambda_q1, lambda_k1, lambda_q2, lambda_k2, g_subln, conv_w, pool_w,
           pool_scale, w_branch, w_out, ffn_w_gate, ffn_w_up, ffn_w_down, router_w,
           moe_w_gate, moe_w_up, moe_w_down):
    depth = w_in.shape[0]
    bsz, seq, _ = x.shape
    ctx_len = ctx.shape[1]

    cvec = jnp.zeros((MOD_ROWS, D_MODEL), F32).at[:bsz].set(c).at[CTX_MOD_ROW].set(c_ctx)
    mod = _modulation(cvec, w_mod, b_mod).reshape(depth * MOD_ROWS, 1, 6 * D_MODEL)
    tables = _rope_tables(seq)
    no_tables = tuple(t[:ctx_len] for t in tables)

    y = ctx
    for i in range(depth):
        last = i == depth - 1
        lam_init = 0.8 - 0.6 * math.exp(-0.3 * i)
        split = w_in.shape[2] - GATES_W
        w_in_i = jnp.concatenate([0.5 * w_in[i, :, split:], w_in[i, :, :split]], axis=1).astype(BF16)
        lam_vecs = jnp.stack([lambda_q1[i], lambda_k1[i], lambda_q2[i], lambda_k2[i]]).astype(F32)
        g_sub = g_subln[i].reshape(1, HEAD_V)
        g_pm, g_qm = g_pre_mix[i].reshape(1, D_MODEL), g_post_mix[i].reshape(1, D_MODEL)
        g_pf, g_qf = g_pre_ffn[i].reshape(1, D_MODEL), g_post_ffn[i].reshape(1, D_MODEL)
        mix_w = (conv_w[i], pool_w[i].astype(BF16), pool_scale[i].reshape(1, BRANCH_W),
                 w_branch[i].astype(BF16), (0.5 * w_out[i]).astype(BF16))

        z = _inproj(x, g_pm, mod, w_in_i, tables, layer=i, ctx=False, rope=True,
                    col_start=0, tm=512)
        kv_full = (K_COL * HEAD_BLKS, V_COL * HEAD_BLKS)
        if last:
            w_kv = w_in_i[:, K_COL * COL_BLK:(V_COL + 1) * COL_BLK]
            zc = _inproj(y, g_pm, mod, w_kv, no_tables, layer=i, ctx=True, rope=False,
                         col_start=K_COL, tm=ctx_len)
            segs = [(z,) + kv_full, (zc, 0, HEAD_BLKS)]
        else:
            zc = _inproj(y, g_pm, mod, w_in_i, no_tables, layer=i, ctx=True, rope=False,
                         col_start=0, tm=ctx_len)
            segs = [(z,) + kv_full, (zc,) + kv_full]
            attn_c = _attention(zc, [(zc,) + kv_full], lam_vecs, g_sub, lam_init=lam_init,
                                tq=ctx_len)
            y = _merge(attn_c, zc, y, mod, g_qm, *mix_w, layer=i, ctx=True, tm=ctx_len)
        attn_l = _attention(z, segs, lam_vecs, g_sub, lam_init=lam_init, tq=1024)
        x = _merge(attn_l, z, x, mod, g_qm, *mix_w, layer=i, ctx=False, tm=512)

        j = i // 2
        streams = [(x, False, 1024)] + ([] if last else [(y, True, ctx_len)])
        if i % 2 == 0:
            ffn_w = (ffn_w_gate[j:j + 1].astype(BF16), ffn_w_up[j:j + 1].astype(BF16),
                     ffn_w_down[j:j + 1].astype(BF16))
            outs = [_ffn(t, g_pf, g_qf, mod, *ffn_w, layer=i, ctx=is_ctx, tm=tm)
                    for t, is_ctx, tm in streams]
        else:
            ffn_w = (moe_w_gate[j].astype(BF16), moe_w_up[j].astype(BF16), moe_w_down[j].astype(BF16))
            wr_pad = jnp.zeros((D_MODEL, LANES), F32).at[:, :N_EXPERTS].set(router_w[j])
            outs = [_moe(t, g_pf, g_qf, mod, wr_pad, *ffn_w, layer=i, ctx=is_ctx)
                    for t, is_ctx, _ in streams]
        x = outs[0]
        if not last:
            y = outs[1]
    return x
```

```python
import functools
import math

import jax
import jax.numpy as jnp
from jax import lax
from jax.experimental import pallas as pl
from jax.experimental.pallas import tpu as pltpu
from jax.experimental.pallas import tpu_sc as plsc

F32 = jnp.float32
BF16 = jnp.bfloat16

D_MODEL = 1024
GRID_W = 64
N_HEADS = 4
HEAD_QK = 64
HEAD_V = 128
ROPE_BASE = 10000.0
ROPE_FREQS = HEAD_QK // 4
CONV_K = 3
POOL_WINDOWS = (2, 4, 8, 16)
POOL_GW = 128
N_BRANCH = 3
BRANCH_W = 512
GATES_W = N_BRANCH * D_MODEL
IN_W = GATES_W + 7 * BRANCH_W
D_FF = 2816
N_EXPERTS = 8
EPS = 1e-6

LANES = 128
BF16_SUBLANES = 16
COL_BLK = 512
Q_COL, K_COL, V_COL, CB_COL, CC_COL, CX_COL, PIN_COL = range(6, 13)
HEAD_BLKS = COL_BLK // 128
MOD_ROWS = 16
CTX_MOD_ROW = 8
HALO = BF16_SUBLANES
FF_CHUNK = 256
KEY_CHUNK = 1024
LOG2_E = 1.4426950408889634
MIN_ROW_SUM = 2.0 ** -88
VMEM_LIMIT = 56 * 1024 * 1024


def _params(sem, vmem=VMEM_LIMIT):
    return pltpu.CompilerParams(dimension_semantics=sem, vmem_limit_bytes=vmem)


def _rms(t, g):
    return t * lax.rsqrt(jnp.mean(t * t, axis=-1, keepdims=True) + EPS) * g


def _mod_kernel(c_ref, w_ref, b_ref, o_ref):
    c = c_ref[...]
    s = c * jax.nn.sigmoid(c)
    o_ref[0] = jnp.dot(s, w_ref[0], preferred_element_type=F32,
                       precision=lax.Precision.HIGHEST) + b_ref[0]


def _modulation(cvec, w_mod, b_mod):
    depth = w_mod.shape[0]
    wcols = w_mod.shape[2]
    tn = 1536
    return pl.pallas_call(
        _mod_kernel,
        grid=(depth, wcols // tn),
        in_specs=[pl.BlockSpec((MOD_ROWS, D_MODEL), lambda l, j: (0, 0)),
                  pl.BlockSpec((1, D_MODEL, tn), lambda l, j: (l, 0, j)),
                  pl.BlockSpec((1, 1, tn), lambda l, j: (l, 0, j))],
        out_specs=pl.BlockSpec((1, MOD_ROWS, tn), lambda l, j: (l, 0, j)),
        out_shape=jax.ShapeDtypeStruct((depth, MOD_ROWS, wcols), F32),
        compiler_params=_params(("parallel", "parallel")),
        name="modulation",
    )(cvec, w_mod, b_mod.reshape(depth, 1, wcols))


def _mod_spec(layer, chunk, ctx, ngrid):
    def idx(*g):
        row = CTX_MOD_ROW if ctx else g[0]
        return (layer * MOD_ROWS + row, 0, chunk)
    del ngrid
    return pl.BlockSpec((1, 1, D_MODEL), idx)


def _inproj_kernel(x_ref, g_ref, sh_ref, sc_ref, w_ref, cos_ref, sa_ref, sb_ref, o_ref,
                   *, rope, col_start):
    h = (_rms(x_ref[0], g_ref[...]) * (1.0 + sc_ref[0]) + sh_ref[0]).astype(BF16)
    for j in range(w_ref.shape[1] // COL_BLK):
        z = jnp.dot(h, w_ref[:, j * COL_BLK:(j + 1) * COL_BLK], preferred_element_type=F32)
        if rope and j + col_start in (Q_COL, K_COL):
            cos, sa, sb = cos_ref[...], sa_ref[...], sb_ref[...]
            for c in range(COL_BLK // LANES):
                t = z[:, c * LANES:(c + 1) * LANES]
                r = (t * cos + pltpu.roll(t, LANES - ROPE_FREQS, 1) * sa
                     + pltpu.roll(t, ROPE_FREQS, 1) * sb)
                o_ref[0, :, j * COL_BLK + c * LANES:j * COL_BLK + (c + 1) * LANES] = r.astype(BF16)
        else:
            o_ref[0, :, j * COL_BLK:(j + 1) * COL_BLK] = z.astype(BF16)


def _resident(shape):
    return pl.BlockSpec(shape, lambda *_: (0,) * len(shape), pipeline_mode=pl.Buffered(1))


def _inproj(x, g, mod, w, tables, *, layer, ctx, rope, col_start, tm):
    bt, L, _ = x.shape
    cos, sa, sb = tables
    tab_spec = pl.BlockSpec((tm, LANES), lambda b, i: (i, 0))
    return pl.pallas_call(
        functools.partial(_inproj_kernel, rope=rope, col_start=col_start),
        grid=(bt, L // tm),
        in_specs=[pl.BlockSpec((1, tm, D_MODEL), lambda b, i: (b, i, 0)),
                  _resident((1, D_MODEL)),
                  _mod_spec(layer, 0, ctx, 2),
                  _mod_spec(layer, 1, ctx, 2),
                  _resident(w.shape),
                  tab_spec, tab_spec, tab_spec],
        out_specs=pl.BlockSpec((1, tm, w.shape[1]), lambda b, i: (b, i, 0)),
        out_shape=jax.ShapeDtypeStruct((bt, L, w.shape[1]), BF16),
        compiler_params=_params(("parallel", "parallel")),
        name="inproj",
    )(x, g, mod, mod, w, cos, sa, sb)


def _map_sums(sq, lane):
    return [jnp.sum(jnp.where(lane < HEAD_QK, sq, 0.0), axis=-1, keepdims=True),
            jnp.sum(jnp.where(lane >= HEAD_QK, sq, 0.0), axis=-1, keepdims=True)]


def _attn_kernel(*refs, nseg, lam_init):
    q_ref, lam_ref, gs_ref = refs[0], refs[1], refs[2]
    kv_refs = refs[3:3 + 2 * nseg]
    o_ref, knorm_scr = refs[3 + 2 * nseg:]

    @pl.when(pl.program_id(2) == 0)
    def _():
        best = [jnp.zeros((1, 1), F32), jnp.zeros((1, 1), F32)]
        for s in range(nseg):
            kf = kv_refs[2 * s][0].astype(F32)
            sums = _map_sums(kf * kf, lax.broadcasted_iota(jnp.int32, kf.shape, 1))
            best = [jnp.maximum(b, jnp.max(n, axis=0, keepdims=True)) for b, n in zip(best, sums)]
        lane1 = lax.broadcasted_iota(jnp.int32, knorm_scr.shape, 1)
        knorm_scr[...] = jnp.where(lane1 == 0, best[0], jnp.where(lane1 == 1, best[1], 0.0))

    lv = lam_ref[...]
    lam = (jnp.exp(jnp.sum(lv[0:1] * lv[1:2], axis=-1, keepdims=True))
           - jnp.exp(jnp.sum(lv[2:3] * lv[3:4], axis=-1, keepdims=True)) + lam_init)

    q = q_ref[0]
    tq = q.shape[0]
    lane = lax.broadcasted_iota(jnp.int32, q.shape, 1)
    qs = (q.astype(F32) * (HEAD_QK ** -0.5 * LOG2_E)).astype(BF16)
    qmap = [jnp.where(lane < HEAD_QK, qs, jnp.zeros_like(qs)),
            jnp.where(lane >= HEAD_QK, qs, jnp.zeros_like(qs))]
    qf = qs.astype(F32)
    knorm = knorm_scr[...]
    bound = [jnp.sqrt(qn * knorm[:, j:j + 1]) for j, qn in enumerate(_map_sums(qf * qf, lane))]
    dn = (((1,), (1,)), ((), ()))

    def chunks():
        for s in range(nseg):
            k_ref, v_ref = kv_refs[2 * s], kv_refs[2 * s + 1]
            for c0 in range(0, k_ref.shape[1], KEY_CHUNK):
                ck = min(KEY_CHUNK, k_ref.shape[1] - c0)
                yield (k_ref[0, c0:c0 + ck, :],
                       jnp.concatenate([v_ref[0, c0:c0 + ck, :], jnp.ones((ck, HEAD_V), BF16)],
                                       axis=1))

    def finish(acc):
        o = (acc[0][:, :HEAD_V] / acc[0][:, HEAD_V:]
             - acc[1][:, :HEAD_V] * (lam / acc[1][:, HEAD_V:]))
        o_ref[0] = (_rms(o, gs_ref[...]) * (1.0 - lam_init)).astype(BF16)

    acc = [jnp.zeros((tq, 2 * HEAD_V), F32) for _ in range(2)]
    for k, v_aug in chunks():
        for j in range(2):
            sc = lax.dot_general(qmap[j], k, dn, preferred_element_type=F32)
            p = jnp.exp2(sc - bound[j]).astype(BF16)
            acc[j] = acc[j] + jnp.dot(p, v_aug, preferred_element_type=F32)
    finish(acc)

    smallest = jnp.min(jnp.minimum(acc[0][:, HEAD_V:], acc[1][:, HEAD_V:]))

    @pl.when(jnp.logical_not(smallest >= MIN_ROW_SUM))
    def _():
        m = [jnp.full((tq, 1), -jnp.inf, F32) for _ in range(2)]
        acc = [jnp.zeros((tq, 2 * HEAD_V), F32) for _ in range(2)]
        for k, v_aug in chunks():
            for j in range(2):
                sc = lax.dot_general(qmap[j], k, dn, preferred_element_type=F32)
                m_new = jnp.maximum(m[j], jnp.max(sc, axis=-1, keepdims=True))
                p = jnp.exp2(sc - m_new).astype(BF16)
                acc[j] = acc[j] * jnp.exp2(m[j] - m_new) + jnp.dot(p, v_aug,
                                                                  preferred_element_type=F32)
                m[j] = m_new
        finish(acc)


def _attention(zq, segs, lam_vecs, g_sub, *, lam_init, tq):
    bt, lq, _ = zq.shape
    q0 = Q_COL * HEAD_BLKS
    in_specs = [pl.BlockSpec((1, tq, HEAD_V), lambda b, h, i: (b, i, q0 + h)),
                pl.BlockSpec((4, HEAD_QK), lambda b, h, i: (0, 0)),
                pl.BlockSpec((1, HEAD_V), lambda b, h, i: (0, 0))]
    args = [zq, lam_vecs, g_sub]
    for arr, kc, vc in segs:
        t = arr.shape[1]
        in_specs.append(pl.BlockSpec((1, t, HEAD_V), lambda b, h, i, kc=kc: (b, 0, kc + h)))
        in_specs.append(pl.BlockSpec((1, t, HEAD_V), lambda b, h, i, vc=vc: (b, 0, vc + h)))
        args += [arr, arr]
    return pl.pallas_call(
        functools.partial(_attn_kernel, nseg=len(segs), lam_init=lam_init),
        grid=(bt, N_HEADS, lq // tq),
        in_specs=in_specs,
        out_specs=pl.BlockSpec((1, tq, HEAD_V), lambda b, h, i: (b, i, h)),
        out_shape=jax.ShapeDtypeStruct((bt, lq, N_HEADS * HEAD_V), BF16),
        scratch_shapes=[pltpu.VMEM((1, LANES), F32)],
        compiler_params=_params(("parallel", "parallel", "arbitrary")),
        name="diff_attention",
    )(*args)


def _merge_kernel(attn_ref, cb_ref, cc_ref, cx_ref, pin_ref, g0_ref, g1_ref, g2_ref,
                  ccp_ref, cxp_ref, pinp_ref, ccn_ref, cxn_ref, pinn_ref,
                  x_ref, gate_ref, gpost_ref, cw_ref, pw_ref, ps_ref, wb_ref, wo_ref,
                  o_ref, u_scr, p_scr, *, tm, seq_len):
    i = pl.program_id(1)
    first = i == 0
    last = i == pl.num_programs(1) - 1

    def f32(ref):
        return ref[0].astype(F32)

    u_scr[0:HALO] = jnp.where(first, 0.0, f32(ccp_ref) * f32(cxp_ref))
    u_scr[HALO:HALO + tm] = f32(cc_ref) * f32(cx_ref)
    u_scr[HALO + tm:2 * HALO + tm] = jnp.where(last, 0.0, f32(ccn_ref) * f32(cxn_ref))
    p_scr[0:HALO] = jnp.where(first, 0.0, f32(pinp_ref))
    p_scr[HALO:HALO + tm] = f32(pin_ref)
    p_scr[HALO + tm:2 * HALO + tm] = jnp.where(last, 0.0, f32(pinn_ref))

    cw = cw_ref[...]
    conv = (cw[0:1] * u_scr[HALO - 1:HALO - 1 + tm] + cw[1:2] * u_scr[HALO:HALO + tm]
            + cw[2:3] * u_scr[HALO + 1:HALO + 1 + tm])
    conv_o = (f32(cb_ref) * conv).astype(BF16)

    t = i * tm + lax.broadcasted_iota(jnp.int32, (tm, 1), 0)
    pool_parts = []
    for g, w in enumerate(POOL_WINDOWS):
        cs = slice(g * POOL_GW, (g + 1) * POOL_GW)
        acc = p_scr[HALO - w // 2:HALO - w // 2 + tm, cs]
        for d in range(-w // 2 + 1, w // 2):
            acc = acc + p_scr[HALO + d:HALO + d + tm, cs]
        lo = jnp.maximum(t - w // 2, 0)
        hi = jnp.minimum(t - w // 2 + w, seq_len)
        p = acc / (hi - lo).astype(F32) - p_scr[HALO:HALO + tm, cs]
        pool_parts.append(jnp.dot(p.astype(BF16), pw_ref[g], preferred_element_type=F32))
    pool_o = (jnp.concatenate(pool_parts, axis=-1) * ps_ref[...]).astype(BF16)

    def gate2(ref):
        return 1.0 + jnp.tanh(f32(ref))

    merged2 = (gate2(g0_ref) * jnp.dot(attn_ref[0], wb_ref[0], preferred_element_type=F32)
               + gate2(g1_ref) * jnp.dot(conv_o, wb_ref[1], preferred_element_type=F32)
               + gate2(g2_ref) * jnp.dot(pool_o, wb_ref[2], preferred_element_type=F32))
    mix = jnp.dot(merged2.astype(BF16), wo_ref[...], preferred_element_type=F32)
    o_ref[0] = x_ref[0] + gate_ref[0] * _rms(mix, gpost_ref[...])


def _merge(attn, z, x, mod, g_post, conv_w, pool_w, pool_scale, w_branch, w_out, *, layer, ctx, tm):
    bt, L, _ = x.shape
    hb = tm // HALO
    nhb = L // HALO

    def col(c, width=COL_BLK):
        return pl.BlockSpec((1, tm, width), lambda b, i, c=c: (b, i, c))

    def prev(c):
        return pl.BlockSpec((1, HALO, COL_BLK), lambda b, i, c=c: (b, jnp.maximum(i * hb - 1, 0), c))

    def nxt(c):
        return pl.BlockSpec((1, HALO, COL_BLK),
                            lambda b, i, c=c: (b, jnp.minimum((i + 1) * hb, nhb - 1), c))

    def full(a):
        return pl.BlockSpec(a.shape, lambda b, i, n=a.ndim: (0,) * n)

    in_specs = [pl.BlockSpec((1, tm, BRANCH_W), lambda b, i: (b, i, 0)),
                col(CB_COL), col(CC_COL), col(CX_COL), col(PIN_COL)]
    in_specs += [col(k, D_MODEL) for k in range(N_BRANCH)]
    in_specs += [prev(CC_COL), prev(CX_COL), prev(PIN_COL), nxt(CC_COL), nxt(CX_COL), nxt(PIN_COL)]
    in_specs += [pl.BlockSpec((1, tm, D_MODEL), lambda b, i: (b, i, 0)),
                 _mod_spec(layer, 2, ctx, 2),
                 full(g_post), full(conv_w), full(pool_w), full(pool_scale), full(w_branch),
                 full(w_out)]
    zz = z
    return pl.pallas_call(
        functools.partial(_merge_kernel, tm=tm, seq_len=L),
        grid=(bt, L // tm),
        in_specs=in_specs,
        out_specs=pl.BlockSpec((1, tm, D_MODEL), lambda b, i: (b, i, 0)),
        out_shape=jax.ShapeDtypeStruct((bt, L, D_MODEL), F32),
        scratch_shapes=[pltpu.VMEM((tm + 2 * HALO, BRANCH_W), F32),
                        pltpu.VMEM((tm + 2 * HALO, BRANCH_W), F32)],
        compiler_params=_params(("parallel", "arbitrary")),
        name="mixer_merge",
    )(attn, *([zz] * 13), x, mod, g_post, conv_w, pool_w, pool_scale, w_branch, w_out)


def _swiglu(h, wg_ref, wu_ref, wd_ref):
    f = None
    for c in range(D_FF // FF_CHUNK):
        cs = slice(c * FF_CHUNK, (c + 1) * FF_CHUNK)
        gt = jnp.dot(h, wg_ref[0, :, cs], preferred_element_type=F32)
        up = jnp.dot(h, wu_ref[0, :, cs], preferred_element_type=F32)
        a = (gt * jax.nn.sigmoid(gt) * up).astype(BF16)
        part = jnp.dot(a, wd_ref[0, cs, :], preferred_element_type=F32)
        f = part if f is None else f + part
    return f


def _ffn_kernel(x_ref, g_ref, sh_ref, sc_ref, gate_ref, gpost_ref, wg_ref, wu_ref, wd_ref, o_ref):
    h = (_rms(x_ref[0], g_ref[...]) * (1.0 + sc_ref[0]) + sh_ref[0]).astype(BF16)
    f = _swiglu(h, wg_ref, wu_ref, wd_ref)
    o_ref[0] = x_ref[0] + gate_ref[0] * _rms(f, gpost_ref[...])


def _ffn(x, g_pre, g_post, mod, wg, wu, wd, *, layer, ctx, tm):
    bt, L, _ = x.shape
    vec = pl.BlockSpec((1, D_MODEL), lambda b, i: (0, 0))
    return pl.pallas_call(
        _ffn_kernel,
        grid=(bt, L // tm),
        in_specs=[pl.BlockSpec((1, tm, D_MODEL), lambda b, i: (b, i, 0)),
                  vec,
                  _mod_spec(layer, 3, ctx, 2),
                  _mod_spec(layer, 4, ctx, 2),
                  _mod_spec(layer, 5, ctx, 2),
                  vec,
                  _resident(wg.shape), _resident(wu.shape), _resident(wd.shape)],
        out_specs=pl.BlockSpec((1, tm, D_MODEL), lambda b, i: (b, i, 0)),
        out_shape=jax.ShapeDtypeStruct((bt, L, D_MODEL), F32),
        compiler_params=_params(("parallel", "parallel")),
        name="channel_mixer",
    )(x, g_pre, mod, mod, mod, g_post, wg, wu, wd)


ROUTE_TM = 512
GROUP_TM = 512
TOP_K = 2
MOE_PARTS = 2
ROW_TILE = (8, LANES)
SC_CORES, SC_SUBCORES = 2, 16
SC_WORKERS = SC_CORES * SC_SUBCORES
SC_CHUNK = 32
RANK_LANE, PROB_LANE, EXPERT_LANE = 0, 2, 4


def _flat_mod_spec(layer, chunk, ctx, tm, seq_len, tile0=0):
    def idx(i):
        row = CTX_MOD_ROW if ctx else ((i + tile0) * tm) // seq_len
        return (layer * MOD_ROWS + row, 0, chunk)
    return pl.BlockSpec((1, 1, D_MODEL), idx)


def _lane_pick(v, lane, k):
    return jnp.sum(jnp.where(lane == k, v, 0.0), axis=-1, keepdims=True)


def _route_kernel(x_ref, g_ref, sh_ref, sc_ref, wr_ref, h_ref, info_ref, cnt_ref, run_scr):
    @pl.when(pl.program_id(0) == 0)
    def _():
        run_scr[...] = jnp.zeros_like(run_scr)

    h = _rms(x_ref[...], g_ref[...]) * (1.0 + sc_ref[0]) + sh_ref[0]
    h_ref[...] = h.reshape(h_ref.shape)
    w = wr_ref[...]
    h_hi, w_hi = h.astype(BF16), w.astype(BF16)
    h_lo = (h - h_hi.astype(F32)).astype(BF16)
    w_lo = (w - w_hi.astype(F32)).astype(BF16)
    logits = (jnp.dot(h_hi, w_hi, preferred_element_type=F32)
              + jnp.dot(h_lo, w_hi, preferred_element_type=F32)
              + jnp.dot(h_hi, w_lo, preferred_element_type=F32))
    lane = lax.broadcasted_iota(jnp.int32, logits.shape, 1)
    neg = jnp.float32(-jnp.inf)
    l1 = jnp.where(lane < N_EXPERTS, logits, neg)
    m1 = jnp.max(l1, axis=-1, keepdims=True)
    i1 = jnp.min(jnp.where(l1 == m1, lane, LANES), axis=-1, keepdims=True)
    l2 = jnp.where(lane == i1, neg, l1)
    m2 = jnp.max(l2, axis=-1, keepdims=True)
    i2 = jnp.min(jnp.where(l2 == m2, lane, LANES), axis=-1, keepdims=True)
    e2 = jnp.exp(m2 - m1)
    den = 1.0 + e2

    chosen = jnp.where(jnp.logical_or(lane == i1, lane == i2), 1.0, 0.0)
    tm = chosen.shape[0]
    earlier = (lax.broadcasted_iota(jnp.int32, (tm, tm), 1)
               < lax.broadcasted_iota(jnp.int32, (tm, tm), 0))
    before = jnp.dot(jnp.where(earlier, 1.0, 0.0).astype(BF16), chosen.astype(BF16),
                     preferred_element_type=F32)
    rank = before + run_scr[...]
    rec = [_lane_pick(rank, lane, i1), _lane_pick(rank, lane, i2), 1.0 / den, e2 / den,
           i1.astype(F32), i2.astype(F32)]
    info = jnp.zeros_like(logits)
    for k, v in enumerate(rec):
        info = jnp.where(lane == k, v, info)
    info_ref[...] = info
    run_scr[...] += jnp.sum(chosen, axis=0, keepdims=True)
    cnt_ref[...] = run_scr[...]


def _route(x2, g, mod, wr_pad, *, layer, ctx, seq_len, tile0, n):
    tm = ROUTE_TM
    vec = pl.BlockSpec((1, D_MODEL), lambda i: (0, 0))
    return pl.pallas_call(
        _route_kernel,
        grid=(n // tm,),
        in_specs=[pl.BlockSpec((tm, D_MODEL), lambda i: (i + tile0, 0)),
                  vec,
                  _flat_mod_spec(layer, 3, ctx, tm, seq_len, tile0),
                  _flat_mod_spec(layer, 4, ctx, tm, seq_len, tile0),
                  pl.BlockSpec((D_MODEL, LANES), lambda i: (0, 0))],
        out_specs=[pl.BlockSpec((tm,) + ROW_TILE, lambda i: (i, 0, 0)),
                   pl.BlockSpec((tm, LANES), lambda i: (i, 0)),
                   pl.BlockSpec((1, LANES), lambda i: (0, 0))],
        out_shape=[jax.ShapeDtypeStruct((n,) + ROW_TILE, F32),
                   jax.ShapeDtypeStruct((n, LANES), F32),
                   jax.ShapeDtypeStruct((1, LANES), F32)],
        scratch_shapes=[pltpu.VMEM((1, LANES), F32)],
        compiler_params=_params(("arbitrary",)),
        name="route",
    )(x2, g, mod, mod, wr_pad)


def _sc_gather(table, idx):
    n_out = idx.shape[0]
    per_worker = n_out // SC_WORKERS
    n_chunks = per_worker // SC_CHUNK
    mesh = plsc.VectorSubcoreMesh(core_axis_name="c", subcore_axis_name="s")

    @functools.partial(
        pl.kernel, mesh=mesh,
        out_type=jax.ShapeDtypeStruct((n_out,) + table.shape[1:], table.dtype),
        scratch_types=[pltpu.VMEM((per_worker,), jnp.int32),
                       pltpu.VMEM((SC_CHUNK,) + table.shape[1:], table.dtype),
                       pltpu.SemaphoreType.DMA],
        name="sc_row_gather")
    def gather(table_hbm, idx_hbm, out_hbm, idx_v, rows_v, sem):
        wid = lax.axis_index("s") * SC_CORES + lax.axis_index("c")
        base = wid * per_worker
        pltpu.sync_copy(idx_hbm.at[pl.ds(base, per_worker)], idx_v)

        @pl.loop(0, n_chunks)
        def _(c):
            off = c * SC_CHUNK
            pltpu.async_copy(table_hbm.at[idx_v.at[pl.ds(off, SC_CHUNK)]], rows_v, sem).wait()
            pltpu.sync_copy(rows_v, out_hbm.at[pl.ds(base + off, SC_CHUNK)])

    return gather(table, idx)


def _sc_scatter(rows, dest):
    n = rows.shape[0]
    n_out = dest.size
    per_worker = n_out // SC_WORKERS
    n_chunks = per_worker // SC_CHUNK
    mesh = plsc.VectorSubcoreMesh(core_axis_name="c", subcore_axis_name="s")

    @functools.partial(
        pl.kernel, mesh=mesh,
        out_type=jax.ShapeDtypeStruct((n_out,) + rows.shape[1:], rows.dtype),
        scratch_types=[pltpu.VMEM((n_chunks, SC_CHUNK), jnp.int32),
                       pltpu.VMEM((SC_CHUNK,) + rows.shape[1:], rows.dtype),
                       pltpu.SemaphoreType.DMA],
        name="sc_row_scatter")
    def scatter(rows_hbm, dest_hbm, out_hbm, dest_v, rows_v, sem):
        wid = lax.axis_index("s") * SC_CORES + lax.axis_index("c")
        src_base = lax.rem(wid * per_worker, n)
        pltpu.sync_copy(dest_hbm.at[pl.ds(wid * n_chunks, n_chunks)], dest_v)

        @pl.loop(0, n_chunks)
        def _(c):
            pltpu.sync_copy(rows_hbm.at[pl.ds(src_base + c * SC_CHUNK, SC_CHUNK)], rows_v)
            pltpu.async_copy(rows_v, out_hbm.at[dest_v.at[c]], sem).wait()

    return scatter(rows, dest)


def _group_kernel(vblk_ref, vexp_ref, vlo_ref, vhi_ref, vfirst_ref, vvalid_ref,
                  h_ref, wg_ref, wu_ref, wd_ref, y_ref):
    del vblk_ref, vexp_ref
    v = pl.program_id(0)

    @pl.when(vvalid_ref[v] == 1)
    def _():
        tg = h_ref.shape[0]
        h = h_ref[...].reshape(tg, D_MODEL).astype(BF16)
        f = _swiglu(h, wg_ref, wu_ref, wd_ref).reshape(y_ref.shape)

        @pl.when(vfirst_ref[v] == 1)
        def _():
            y_ref[...] = f

        @pl.when(vfirst_ref[v] == 0)
        def _():
            row = lax.broadcasted_iota(jnp.int32, (tg, 1, 1), 0)
            mine = jnp.logical_and(row >= vlo_ref[v], row < vhi_ref[v])
            y_ref[...] = jnp.where(mine, f, y_ref[...])


def _grouped_experts(hs, visits, wg, wu, wd):
    n_rows = hs.shape[0]
    tg = GROUP_TM
    nv = visits[0].shape[0]

    def wspec(shape):
        return pl.BlockSpec((1,) + shape, lambda v, blk, exp, *_: (exp[v], 0, 0))

    row_spec = pl.BlockSpec((tg,) + ROW_TILE, lambda v, blk, *_: (blk[v], 0, 0))
    return pl.pallas_call(
        _group_kernel,
        grid_spec=pltpu.PrefetchScalarGridSpec(
            num_scalar_prefetch=len(visits),
            grid=(nv,),
            in_specs=[row_spec, wspec((D_MODEL, D_FF)), wspec((D_MODEL, D_FF)),
                      wspec((D_FF, D_MODEL))],
            out_specs=row_spec),
        out_shape=jax.ShapeDtypeStruct((n_rows,) + ROW_TILE, F32),
        compiler_params=_params(("arbitrary",)),
        name="grouped_experts",
    )(*visits, hs, wg, wu, wd)


def _visit_tables(counts, n_rows):
    tg = GROUP_TM
    nv = n_rows // tg + N_EXPERTS - 1
    ends = jnp.cumsum(counts)
    starts = ends - counts
    first_tile = starts // tg
    last_tile = jnp.maximum(ends - 1, 0) // tg
    nvis = jnp.where(counts > 0, last_tile - first_tile + 1, 0)
    vend = jnp.cumsum(nvis)
    total = vend[-1]
    v = jnp.minimum(jnp.arange(nv, dtype=jnp.int32), total - 1)
    exp = jnp.sum(v[:, None] >= vend[None, :], axis=1).astype(jnp.int32)
    tile = first_tile[exp] + v - (vend[exp] - nvis[exp])
    lo = jnp.maximum(starts[exp], tile * tg) - tile * tg
    hi = jnp.minimum(ends[exp], (tile + 1) * tg) - tile * tg
    first = jnp.concatenate([jnp.ones((1,), jnp.int32), (tile[1:] != tile[:-1]).astype(jnp.int32)])
    valid = (jnp.arange(nv) < total).astype(jnp.int32)
    return tuple(a.astype(jnp.int32) for a in (tile, exp, lo, hi, first, valid)), starts


def _combine_kernel(x_ref, y1_ref, y2_ref, info_ref, gate_ref, gpost_ref, o_ref):
    info = info_ref[...]
    lane = lax.broadcasted_iota(jnp.int32, info.shape, 1)
    f = (_lane_pick(info, lane, PROB_LANE) * y1_ref[...].reshape(x_ref.shape)
         + _lane_pick(info, lane, PROB_LANE + 1) * y2_ref[...].reshape(x_ref.shape))
    o_ref[...] = x_ref[...] + gate_ref[0] * _rms(f, gpost_ref[...])


def _combine_into_kernel(x_ref, y1_ref, y2_ref, info_ref, gate_ref, gpost_ref, prev_ref, o_ref):
    del prev_ref
    _combine_kernel(x_ref, y1_ref, y2_ref, info_ref, gate_ref, gpost_ref, o_ref)


def _combine(x2, y12, info, mod, g_post, prev, *, layer, ctx, seq_len, tile0):
    n = x2.shape[0]
    tm = ROUTE_TM
    nt = info.shape[0] // tm
    row = pl.BlockSpec((tm, D_MODEL), lambda i: (i + tile0, 0))
    in_specs = [row, pl.BlockSpec((tm,) + ROW_TILE, lambda i: (i, 0, 0)),
                pl.BlockSpec((tm,) + ROW_TILE, lambda i: (nt + i, 0, 0)),
                pl.BlockSpec((tm, LANES), lambda i: (i, 0)),
                _flat_mod_spec(layer, 5, ctx, tm, seq_len, tile0),
                pl.BlockSpec((1, D_MODEL), lambda i: (0, 0))]
    args = [x2, y12, y12, info, mod, g_post]
    if prev is not None:
        in_specs.append(pl.BlockSpec(memory_space=pl.ANY))
        args.append(prev)
    return pl.pallas_call(
        _combine_kernel if prev is None else _combine_into_kernel,
        grid=(nt,),
        in_specs=in_specs,
        out_specs=row,
        out_shape=jax.ShapeDtypeStruct((n, D_MODEL), F32),
        input_output_aliases={} if prev is None else {len(args) - 1: 0},
        compiler_params=_params(("parallel",)),
        name="combine",
    )(*args)


def _moe(x, g_pre, g_post, mod, wr_pad, wg, wu, wd, *, layer, ctx):
    bt, seq_len, _ = x.shape
    n = bt * seq_len
    x2 = x.reshape(n, D_MODEL)
    parts = MOE_PARTS if n % (MOE_PARTS * SC_WORKERS * SC_CHUNK) == 0 else 1
    n_part = n // parts
    tiles_part = n_part // ROUTE_TM

    routed = [_route(x2, g_pre, mod, wr_pad, layer=layer, ctx=ctx, seq_len=seq_len,
                     tile0=p * tiles_part, n=n_part) for p in range(parts)]
    sorted_rows = []
    for h, info, cnt in routed:
        counts = cnt[0, :N_EXPERTS].astype(jnp.int32)
        visits, starts = _visit_tables(counts, TOP_K * n_part)
        expert = info[:, EXPERT_LANE:EXPERT_LANE + TOP_K].astype(jnp.int32)
        rank = info[:, RANK_LANE:RANK_LANE + TOP_K].astype(jnp.int32)
        pos = (starts[expert] + rank).T.reshape(-1)
        sorted_rows.append((_sc_scatter(h, pos.reshape(-1, SC_CHUNK)), visits, pos))
    gathered = [_sc_gather(_grouped_experts(hs, visits, wg, wu, wd), pos)
                for hs, visits, pos in sorted_rows]
    out = None
    for p, (y12, (_, info, _)) in enumerate(zip(gathered, routed)):
        out = _combine(x2, y12, info, mod, g_post, out, layer=layer, ctx=ctx, seq_len=seq_len,
                       tile0=p * tiles_part)
    return out.reshape(bt, seq_len, D_MODEL)


def _rope_tables(seq_len):
    rows = jnp.repeat(jnp.arange(seq_len // GRID_W), GRID_W).astype(F32)
    cols = jnp.tile(jnp.arange(GRID_W), seq_len // GRID_W).astype(F32)
    inv = ROPE_BASE ** (-jnp.arange(ROPE_FREQS, dtype=F32) / ROPE_FREQS)
    ang = jnp.concatenate([rows[:, None] * inv, rows[:, None] * inv,
                           cols[:, None] * inv, cols[:, None] * inv], axis=1)
    ang = jnp.tile(ang, (1, LANES // HEAD_QK))
    cos, sin = jnp.cos(ang), jnp.sin(ang)
    low = (jnp.arange(LANES) % (2 * ROPE_FREQS)) < ROPE_FREQS
    return cos, jnp.where(low, -sin, 0.0), jnp.where(low, 0.0, sin)


def kernel(x, c, ctx, c_ctx, w_mod, b_mod, g_pre_mix, g_post_mix, g_pre_ffn, g_post_ffn,
           w_in, lambda_q1, lambda_k1, lambda_q2, lambda_k2, g_subln, conv_w, pool_w,
           pool_scale, w_branch, w_out, ffn_w_gate, ffn_w_up, ffn_w_down, router_w,
           moe_w_gate, moe_w_up, moe_w_down):
    depth = w_in.shape[0]
    bsz, seq, _ = x.shape
    ctx_len = ctx.shape[1]

    cvec = jnp.zeros((MOD_ROWS, D_MODEL), F32).at[:bsz].set(c).at[CTX_MOD_ROW].set(c_ctx)
    mod = _modulation(cvec, w_mod, b_mod).reshape(depth * MOD_ROWS, 1, 6 * D_MODEL)
    tables = _rope_tables(seq)
    no_tables = tuple(t[:ctx_len] for t in tables)

    y = ctx
    for i in range(depth):
        last = i == depth - 1
        lam_init = 0.8 - 0.6 * math.exp(-0.3 * i)
        split = w_in.shape[2] - GATES_W
        w_in_i = jnp.concatenate([0.5 * w_in[i, :, split:], w_in[i, :, :split]], axis=1).astype(BF16)
        lam_vecs = jnp.stack([lambda_q1[i], lambda_k1[i], lambda_q2[i], lambda_k2[i]]).astype(F32)
        g_sub = g_subln[i].reshape(1, HEAD_V)
        g_pm, g_qm = g_pre_mix[i].reshape(1, D_MODEL), g_post_mix[i].reshape(1, D_MODEL)
        g_pf, g_qf = g_pre_ffn[i].reshape(1, D_MODEL), g_post_ffn[i].reshape(1, D_MODEL)
        mix_w = (conv_w[i], pool_w[i].astype(BF16), pool_scale[i].reshape(1, BRANCH_W),
                 w_branch[i].astype(BF16), (0.5 * w_out[i]).astype(BF16))

        z = _inproj(x, g_pm, mod, w_in_i, tables, layer=i, ctx=False, rope=True,
                    col_start=0, tm=512)
        kv_full = (K_COL * HEAD_BLKS, V_COL * HEAD_BLKS)
        if last:
            w_kv = w_in_i[:, K_COL * COL_BLK:(V_COL + 1) * COL_BLK]
            zc = _inproj(y, g_pm, mod, w_kv, no_tables, layer=i, ctx=True, rope=False,
                         col_start=K_COL, tm=ctx_len)
            segs = [(z,) + kv_full, (zc, 0, HEAD_BLKS)]
        else:
            zc = _inproj(y, g_pm, mod, w_in_i, no_tables, layer=i, ctx=True, rope=False,
                         col_start=0, tm=ctx_len)
            segs = [(z,) + kv_full, (zc,) + kv_full]
            attn_c = _attention(zc, [(zc,) + kv_full], lam_vecs, g_sub, lam_init=lam_init,
                                tq=ctx_len)
            y = _merge(attn_c, zc, y, mod, g_qm, *mix_w, layer=i, ctx=True, tm=ctx_len)
        attn_l = _attention(z, segs, lam_vecs, g_sub, lam_init=lam_init, tq=1024)
        x = _merge(attn_l, z, x, mod, g_qm, *mix_w, layer=i, ctx=False, tm=512)

        j = i // 2
        streams = [(x, False, 1024)] + ([] if last else [(y, True, ctx_len)])
        if i % 2 == 0:
            ffn_w = (ffn_w_gate[j:j + 1].astype(BF16), ffn_w_up[j:j + 1].astype(BF16),
                     ffn_w_down[j:j + 1].astype(BF16))
            outs = [_ffn(t, g_pf, g_qf, mod, *ffn_w, layer=i, ctx=is_ctx, tm=tm)
                    for t, is_ctx, tm in streams]
        else:
            ffn_w = (moe_w_gate[j].astype(BF16), moe_w_up[j].astype(BF16), moe_w_down[j].astype(BF16))
            wr_pad = jnp.zeros((D_MODEL, LANES), F32).at[:, :N_EXPERTS].set(router_w[j])
            outs = [_moe(t, g_pf, g_qf, mod, wr_pad, *ffn_w, layer=i, ctx=is_ctx)
                    for t, is_ctx, _ in streams]
        x = outs[0]
        if not last:
            y = outs[1]
    return x
```

```python
import functools
import math

import jax
import jax.numpy as jnp
from jax import lax
from jax.experimental import pallas as pl
from jax.experimental.pallas import tpu as pltpu
from jax.experimental.pallas import tpu_sc as plsc

F32 = jnp.float32
BF16 = jnp.bfloat16

D_MODEL = 1024
GRID_W = 64
N_HEADS = 4
HEAD_QK = 64
HEAD_V = 128
ROPE_BASE = 10000.0
ROPE_FREQS = HEAD_QK // 4
CONV_K = 3
POOL_WINDOWS = (2, 4, 8, 16)
POOL_GW = 128
N_BRANCH = 3
BRANCH_W = 512
GATES_W = N_BRANCH * D_MODEL
IN_W = GATES_W + 7 * BRANCH_W
D_FF = 2816
N_EXPERTS = 8
EPS = 1e-6

LANES = 128
BF16_SUBLANES = 16
COL_BLK = 512
Q_COL, K_COL, V_COL, CB_COL, CC_COL, CX_COL, PIN_COL = range(6, 13)
CONV_OUT, POOL_OUT = CB_COL, CB_COL + 1
HEAD_BLKS = COL_BLK // 128
MOD_ROWS = 16
CTX_MOD_ROW = 8
HALO = 8
FF_CHUNK = 256
KEY_CHUNK = 1024
LOG2_E = 1.4426950408889634
MIN_ROW_SUM = 2.0 ** -88
VMEM_LIMIT = 56 * 1024 * 1024


def _params(sem, vmem=VMEM_LIMIT):
    return pltpu.CompilerParams(dimension_semantics=sem, vmem_limit_bytes=vmem)


def _rms(t, g):
    return t * lax.rsqrt(jnp.mean(t * t, axis=-1, keepdims=True) + EPS) * g


def _mod_kernel(c_ref, w_ref, b_ref, o_ref):
    c = c_ref[...]
    s = c * jax.nn.sigmoid(c)
    o_ref[0] = jnp.dot(s, w_ref[0], preferred_element_type=F32,
                       precision=lax.Precision.HIGHEST) + b_ref[0]


def _modulation(cvec, w_mod, b_mod):
    depth = w_mod.shape[0]
    wcols = w_mod.shape[2]
    tn = 1536
    return pl.pallas_call(
        _mod_kernel,
        grid=(depth, wcols // tn),
        in_specs=[pl.BlockSpec((MOD_ROWS, D_MODEL), lambda l, j: (0, 0)),
                  pl.BlockSpec((1, D_MODEL, tn), lambda l, j: (l, 0, j)),
                  pl.BlockSpec((1, 1, tn), lambda l, j: (l, 0, j))],
        out_specs=pl.BlockSpec((1, MOD_ROWS, tn), lambda l, j: (l, 0, j)),
        out_shape=jax.ShapeDtypeStruct((depth, MOD_ROWS, wcols), F32),
        compiler_params=_params(("parallel", "parallel")),
        name="modulation",
    )(cvec, w_mod, b_mod.reshape(depth, 1, wcols))


def _mod_spec(layer, chunk, ctx, ngrid):
    def idx(*g):
        row = CTX_MOD_ROW if ctx else g[0]
        return (layer * MOD_ROWS + row, 0, chunk)
    del ngrid
    return pl.BlockSpec((1, 1, D_MODEL), idx)


def _inproj_kernel(*refs, rope, col_start, mix, seq_len):
    if mix:
        (x_ref, xp_ref, xn_ref, g_ref, sh_ref, sc_ref, w_ref, cos_ref, sa_ref, sb_ref,
         cw_ref, pw_ref, ps_ref, o_ref, u_scr, p_scr) = refs
        x = jnp.concatenate([x_ref[0], xp_ref[0], xn_ref[0]], axis=0)
    else:
        x_ref, g_ref, sh_ref, sc_ref, w_ref, cos_ref, sa_ref, sb_ref, o_ref = refs
        x = x_ref[0]
    tm = x_ref.shape[1]
    h = (_rms(x, g_ref[...]) * (1.0 + sc_ref[0]) + sh_ref[0]).astype(BF16)
    def mix_branches(cb, cc, cx, pin):
        i = pl.program_id(1)
        first = i == 0
        last = i == pl.num_programs(1) - 1

        def fill(scr, val):
            scr[0:HALO] = jnp.where(first, 0.0, val[tm:tm + HALO])
            scr[HALO:HALO + tm] = val[:tm]
            scr[HALO + tm:2 * HALO + tm] = jnp.where(last, 0.0, val[tm + HALO:])

        fill(u_scr, cc * cx)
        fill(p_scr, pin)
        yield
        cw = cw_ref[...]
        for cs in (slice(0, BRANCH_W // 2), slice(BRANCH_W // 2, BRANCH_W)):
            conv = (cw[0:1, cs] * u_scr[HALO - 1:HALO - 1 + tm, cs] + cw[1:2, cs] * u_scr[HALO:HALO + tm, cs]
                    + cw[2:3, cs] * u_scr[HALO + 1:HALO + 1 + tm, cs])
            o_ref[0, :, CONV_OUT * COL_BLK + cs.start:CONV_OUT * COL_BLK + cs.stop] = (
                cb[:, cs] * conv).astype(BF16)
            yield

        t = i * tm + lax.broadcasted_iota(jnp.int32, (tm, 1), 0)
        for g, w in enumerate(POOL_WINDOWS):
            cs = slice(g * POOL_GW, (g + 1) * POOL_GW)
            acc = p_scr[HALO - w // 2:HALO - w // 2 + tm, cs]
            for d in range(-w // 2 + 1, w // 2):
                acc = acc + p_scr[HALO + d:HALO + d + tm, cs]
            lo = jnp.maximum(t - w // 2, 0)
            hi = jnp.minimum(t - w // 2 + w, seq_len)
            p = acc / (hi - lo).astype(F32) - p_scr[HALO:HALO + tm, cs]
            y = jnp.dot(p.astype(BF16), pw_ref[g], preferred_element_type=F32) * ps_ref[:, cs]
            o_ref[0, :, POOL_OUT * COL_BLK + g * POOL_GW:POOL_OUT * COL_BLK + (g + 1) * POOL_GW] = (
                y.astype(BF16))
            yield

    kept = {}
    pieces = iter(())
    n_blk = w_ref.shape[1] // COL_BLK
    for j in sorted(range(n_blk), key=lambda j: not (mix and j + col_start >= CB_COL)):
        col = j + col_start
        w_blk = w_ref[:, j * COL_BLK:(j + 1) * COL_BLK]
        if mix and col >= CB_COL:
            kept[col] = jnp.dot(h[:tm] if col == CB_COL else h, w_blk, preferred_element_type=F32)
            if len(kept) == 4:
                pieces = mix_branches(*[kept[c] for c in (CB_COL, CC_COL, CX_COL, PIN_COL)])
            continue
        z = jnp.dot(h[:tm], w_blk, preferred_element_type=F32)
        next(pieces, None)
        if rope and col in (Q_COL, K_COL):
            cos, sa, sb = cos_ref[...], sa_ref[...], sb_ref[...]
            for c in range(COL_BLK // LANES):
                t = z[:, c * LANES:(c + 1) * LANES]
                r = (t * cos + pltpu.roll(t, LANES - ROPE_FREQS, 1) * sa
                     + pltpu.roll(t, ROPE_FREQS, 1) * sb)
                o_ref[0, :, j * COL_BLK + c * LANES:j * COL_BLK + (c + 1) * LANES] = r.astype(BF16)
        else:
            o_ref[0, :, j * COL_BLK:(j + 1) * COL_BLK] = z.astype(BF16)
    for _ in pieces:
        pass


def _resident(shape):
    return pl.BlockSpec(shape, lambda *_: (0,) * len(shape), pipeline_mode=pl.Buffered(1))


def _inproj(x, g, mod, w, tables, mix_w=None, *, layer, ctx, rope, col_start, tm):
    bt, L, _ = x.shape
    cos, sa, sb = tables
    mix = mix_w is not None
    out_w = (POOL_OUT + 1) * COL_BLK if mix else w.shape[1]
    tab_spec = pl.BlockSpec((tm, LANES), lambda b, i: (i, 0))
    hb, nhb = tm // HALO, L // HALO
    in_specs = [pl.BlockSpec((1, tm, D_MODEL), lambda b, i: (b, i, 0))]
    args = [x]
    if mix:
        in_specs += [pl.BlockSpec((1, HALO, D_MODEL), lambda b, i: (b, jnp.maximum(i * hb - 1, 0), 0)),
                     pl.BlockSpec((1, HALO, D_MODEL),
                                  lambda b, i: (b, jnp.minimum((i + 1) * hb, nhb - 1), 0))]
        args += [x, x]
    in_specs += [_resident((1, D_MODEL)), _mod_spec(layer, 0, ctx, 2), _mod_spec(layer, 1, ctx, 2),
                 _resident(w.shape), tab_spec, tab_spec, tab_spec]
    args += [g, mod, mod, w, cos, sa, sb]
    scratch = []
    if mix:
        in_specs += [_resident(a.shape) for a in mix_w]
        args += list(mix_w)
        scratch = [pltpu.VMEM((tm + 2 * HALO, BRANCH_W), F32)] * 2
    return pl.pallas_call(
        functools.partial(_inproj_kernel, rope=rope, col_start=col_start, mix=mix, seq_len=L),
        grid=(bt, L // tm),
        in_specs=in_specs,
        out_specs=pl.BlockSpec((1, tm, out_w), lambda b, i: (b, i, 0)),
        out_shape=jax.ShapeDtypeStruct((bt, L, out_w), BF16),
        scratch_shapes=scratch,
        compiler_params=_params(("parallel", "parallel")),
        name="inproj",
    )(*args)


def _map_sums(sq, lane):
    return [jnp.sum(jnp.where(lane < HEAD_QK, sq, 0.0), axis=-1, keepdims=True),
            jnp.sum(jnp.where(lane >= HEAD_QK, sq, 0.0), axis=-1, keepdims=True)]


def _attn_kernel(*refs, nseg, lam_init):
    q_ref, lam_ref, gs_ref = refs[0], refs[1], refs[2]
    kv_refs = refs[3:3 + 2 * nseg]
    o_ref, knorm_scr = refs[3 + 2 * nseg:]

    @pl.when(pl.program_id(2) == 0)
    def _():
        best = [jnp.zeros((1, 1), F32), jnp.zeros((1, 1), F32)]
        for s in range(nseg):
            kf = kv_refs[2 * s][0].astype(F32)
            sums = _map_sums(kf * kf, lax.broadcasted_iota(jnp.int32, kf.shape, 1))
            best = [jnp.maximum(b, jnp.max(n, axis=0, keepdims=True)) for b, n in zip(best, sums)]
        lane1 = lax.broadcasted_iota(jnp.int32, knorm_scr.shape, 1)
        knorm_scr[...] = jnp.where(lane1 == 0, best[0], jnp.where(lane1 == 1, best[1], 0.0))

    lv = lam_ref[...]
    lam = (jnp.exp(jnp.sum(lv[0:1] * lv[1:2], axis=-1, keepdims=True))
           - jnp.exp(jnp.sum(lv[2:3] * lv[3:4], axis=-1, keepdims=True)) + lam_init)

    q = q_ref[0]
    tq = q.shape[0]
    lane = lax.broadcasted_iota(jnp.int32, q.shape, 1)
    qs = (q.astype(F32) * (HEAD_QK ** -0.5 * LOG2_E)).astype(BF16)
    qmap = [jnp.where(lane < HEAD_QK, qs, jnp.zeros_like(qs)),
            jnp.where(lane >= HEAD_QK, qs, jnp.zeros_like(qs))]
    qf = qs.astype(F32)
    knorm = knorm_scr[...]
    bound = [jnp.sqrt(qn * knorm[:, j:j + 1]) for j, qn in enumerate(_map_sums(qf * qf, lane))]
    dn = (((1,), (1,)), ((), ()))

    def chunks():
        for s in range(nseg):
            k_ref, v_ref = kv_refs[2 * s], kv_refs[2 * s + 1]
            for c0 in range(0, k_ref.shape[1], KEY_CHUNK):
                ck = min(KEY_CHUNK, k_ref.shape[1] - c0)
                v = v_ref[0, c0:c0 + ck, :]
                yield k_ref[0, c0:c0 + ck, :], jnp.concatenate([v, jnp.ones_like(v)], axis=1)

    def finish(acc):
        o = (acc[0][:, :HEAD_V] / acc[0][:, HEAD_V:]
             - acc[1][:, :HEAD_V] * (lam / acc[1][:, HEAD_V:]))
        o_ref[0] = (_rms(o, gs_ref[...]) * (1.0 - lam_init)).astype(BF16)

    acc = [jnp.zeros((tq, 2 * HEAD_V), F32) for _ in range(2)]
    for k, v_aug in chunks():
        for j in range(2):
            sc = lax.dot_general(qmap[j], k, dn, preferred_element_type=F32)
            p = jnp.exp2(sc - bound[j]).astype(BF16)
            acc[j] = acc[j] + jnp.dot(p, v_aug, preferred_element_type=F32)
    finish(acc)

    smallest = jnp.min(jnp.minimum(acc[0][:, HEAD_V:], acc[1][:, HEAD_V:]))

    @pl.when(jnp.logical_not(smallest >= MIN_ROW_SUM))
    def _():
        m = [jnp.full((tq, 1), -jnp.inf, F32) for _ in range(2)]
        acc = [jnp.zeros((tq, 2 * HEAD_V), F32) for _ in range(2)]
        for k, v_aug in chunks():
            for j in range(2):
                sc = lax.dot_general(qmap[j], k, dn, preferred_element_type=F32)
                m_new = jnp.maximum(m[j], jnp.max(sc, axis=-1, keepdims=True))
                p = jnp.exp2(sc - m_new).astype(BF16)
                acc[j] = acc[j] * jnp.exp2(m[j] - m_new) + jnp.dot(p, v_aug,
                                                                  preferred_element_type=F32)
                m[j] = m_new
        finish(acc)


def _attention(zq, segs, lam_vecs, g_sub, *, lam_init, tq):
    bt, lq, _ = zq.shape
    q0 = Q_COL * HEAD_BLKS
    in_specs = [pl.BlockSpec((1, tq, HEAD_V), lambda b, h, i: (b, i, q0 + h)),
                pl.BlockSpec((4, HEAD_QK), lambda b, h, i: (0, 0)),
                pl.BlockSpec((1, HEAD_V), lambda b, h, i: (0, 0))]
    args = [zq, lam_vecs, g_sub]
    for arr, kc, vc in segs:
        t = arr.shape[1]
        in_specs.append(pl.BlockSpec((1, t, HEAD_V), lambda b, h, i, kc=kc: (b, 0, kc + h)))
        in_specs.append(pl.BlockSpec((1, t, HEAD_V), lambda b, h, i, vc=vc: (b, 0, vc + h)))
        args += [arr, arr]
    return pl.pallas_call(
        functools.partial(_attn_kernel, nseg=len(segs), lam_init=lam_init),
        grid=(bt, N_HEADS, lq // tq),
        in_specs=in_specs,
        out_specs=pl.BlockSpec((1, tq, HEAD_V), lambda b, h, i: (b, i, h)),
        out_shape=jax.ShapeDtypeStruct((bt, lq, N_HEADS * HEAD_V), BF16),
        scratch_shapes=[pltpu.VMEM((1, LANES), F32)],
        compiler_params=_params(("parallel", "parallel", "arbitrary")),
        name="diff_attention",
    )(*args)


def _merge_kernel(attn_ref, conv_ref, pool_ref, g0_ref, g1_ref, g2_ref, x_ref, gate_ref, gpost_ref,
                  wb_ref, wo_ref, o_ref):
    def gate2(ref):
        return 1.0 + jnp.tanh(ref[0].astype(F32))

    merged2 = (gate2(g0_ref) * jnp.dot(attn_ref[0], wb_ref[0], preferred_element_type=F32)
               + gate2(g1_ref) * jnp.dot(conv_ref[0], wb_ref[1], preferred_element_type=F32)
               + gate2(g2_ref) * jnp.dot(pool_ref[0], wb_ref[2], preferred_element_type=F32))
    mix = jnp.dot(merged2.astype(BF16), wo_ref[...], preferred_element_type=F32)
    o_ref[0] = x_ref[0] + gate_ref[0] * _rms(mix, gpost_ref[...])


def _merge(attn, z, x, mod, g_post, w_branch, w_out, *, layer, ctx, tm):
    bt, L, _ = x.shape

    def col(c, width=COL_BLK):
        return pl.BlockSpec((1, tm, width), lambda b, i, c=c: (b, i, c))

    in_specs = [col(0), col(CONV_OUT), col(POOL_OUT)]
    in_specs += [col(k, D_MODEL) for k in range(N_BRANCH)]
    in_specs += [col(0, D_MODEL), _mod_spec(layer, 2, ctx, 2),
                 _resident(g_post.shape), _resident(w_branch.shape), _resident(w_out.shape)]
    return pl.pallas_call(
        _merge_kernel,
        grid=(bt, L // tm),
        in_specs=in_specs,
        out_specs=col(0, D_MODEL),
        out_shape=jax.ShapeDtypeStruct((bt, L, D_MODEL), F32),
        compiler_params=_params(("parallel", "parallel")),
        name="mixer_merge",
    )(attn, *([z] * 5), x, mod, g_post, w_branch, w_out)


def _swiglu(h, wg_ref, wu_ref, wd_ref):
    f = None
    for c in range(D_FF // FF_CHUNK):
        cs = slice(c * FF_CHUNK, (c + 1) * FF_CHUNK)
        gt = jnp.dot(h, wg_ref[0, :, cs], preferred_element_type=F32)
        up = jnp.dot(h, wu_ref[0, :, cs], preferred_element_type=F32)
        a = (gt * jax.nn.sigmoid(gt) * up).astype(BF16)
        part = jnp.dot(a, wd_ref[0, cs, :], preferred_element_type=F32)
        f = part if f is None else f + part
    return f


def _ffn_kernel(x_ref, g_ref, sh_ref, sc_ref, gate_ref, gpost_ref, wg_ref, wu_ref, wd_ref, o_ref):
    h = (_rms(x_ref[0], g_ref[...]) * (1.0 + sc_ref[0]) + sh_ref[0]).astype(BF16)
    f = _swiglu(h, wg_ref, wu_ref, wd_ref)
    o_ref[0] = x_ref[0] + gate_ref[0] * _rms(f, gpost_ref[...])


def _ffn(x, g_pre, g_post, mod, wg, wu, wd, *, layer, ctx, tm):
    bt, L, _ = x.shape
    vec = pl.BlockSpec((1, D_MODEL), lambda b, i: (0, 0))
    return pl.pallas_call(
        _ffn_kernel,
        grid=(bt, L // tm),
        in_specs=[pl.BlockSpec((1, tm, D_MODEL), lambda b, i: (b, i, 0)),
                  vec,
                  _mod_spec(layer, 3, ctx, 2),
                  _mod_spec(layer, 4, ctx, 2),
                  _mod_spec(layer, 5, ctx, 2),
                  vec,
                  _resident(wg.shape), _resident(wu.shape), _resident(wd.shape)],
        out_specs=pl.BlockSpec((1, tm, D_MODEL), lambda b, i: (b, i, 0)),
        out_shape=jax.ShapeDtypeStruct((bt, L, D_MODEL), F32),
        compiler_params=_params(("parallel", "parallel")),
        name="channel_mixer",
    )(x, g_pre, mod, mod, mod, g_post, wg, wu, wd)


ROUTE_TM = 512
GROUP_TM = 512
TOP_K = 2
MOE_PARTS = 2
ROW_TILE = (8, LANES)
SC_CORES, SC_SUBCORES = 2, 16
SC_WORKERS = SC_CORES * SC_SUBCORES
SC_CHUNK = 32
RANK_LANE, PROB_LANE, EXPERT_LANE = 0, 2, 4


def _flat_mod_spec(layer, chunk, ctx, tm, seq_len, tile0=0):
    def idx(i):
        row = CTX_MOD_ROW if ctx else ((i + tile0) * tm) // seq_len
        return (layer * MOD_ROWS + row, 0, chunk)
    return pl.BlockSpec((1, 1, D_MODEL), idx)


def _lane_pick(v, lane, k):
    return jnp.sum(jnp.where(lane == k, v, 0.0), axis=-1, keepdims=True)


def _route_kernel(x_ref, g_ref, sh_ref, sc_ref, wr_ref, h_ref, info_ref, cnt_ref, run_scr):
    @pl.when(pl.program_id(0) == 0)
    def _():
        run_scr[...] = jnp.zeros_like(run_scr)

    h = _rms(x_ref[...], g_ref[...]) * (1.0 + sc_ref[0]) + sh_ref[0]
    h_ref[...] = h.reshape(h_ref.shape)
    w = wr_ref[...]
    h_hi, w_hi = h.astype(BF16), w.astype(BF16)
    h_lo = (h - h_hi.astype(F32)).astype(BF16)
    w_lo = (w - w_hi.astype(F32)).astype(BF16)
    logits = (jnp.dot(h_hi, w_hi, preferred_element_type=F32)
              + jnp.dot(h_lo, w_hi, preferred_element_type=F32)
              + jnp.dot(h_hi, w_lo, preferred_element_type=F32))
    lane = lax.broadcasted_iota(jnp.int32, logits.shape, 1)
    neg = jnp.float32(-jnp.inf)
    l1 = jnp.where(lane < N_EXPERTS, logits, neg)
    m1 = jnp.max(l1, axis=-1, keepdims=True)
    i1 = jnp.min(jnp.where(l1 == m1, lane, LANES), axis=-1, keepdims=True)
    l2 = jnp.where(lane == i1, neg, l1)
    m2 = jnp.max(l2, axis=-1, keepdims=True)
    i2 = jnp.min(jnp.where(l2 == m2, lane, LANES), axis=-1, keepdims=True)
    e2 = jnp.exp(m2 - m1)
    den = 1.0 + e2

    chosen = jnp.where(jnp.logical_or(lane == i1, lane == i2), 1.0, 0.0)
    tm = chosen.shape[0]
    earlier = (lax.broadcasted_iota(jnp.int32, (tm, tm), 1)
               < lax.broadcasted_iota(jnp.int32, (tm, tm), 0))
    before = jnp.dot(jnp.where(earlier, 1.0, 0.0).astype(BF16), chosen.astype(BF16),
                     preferred_element_type=F32)
    rank = before + run_scr[...]
    rec = [_lane_pick(rank, lane, i1), _lane_pick(rank, lane, i2), 1.0 / den, e2 / den,
           i1.astype(F32), i2.astype(F32)]
    info = jnp.zeros_like(logits)
    for k, v in enumerate(rec):
        info = jnp.where(lane == k, v, info)
    info_ref[...] = info
    run_scr[...] += jnp.sum(chosen, axis=0, keepdims=True)
    cnt_ref[...] = run_scr[...]


def _route(x2, g, mod, wr_pad, *, layer, ctx, seq_len, tile0, n):
    tm = ROUTE_TM
    vec = pl.BlockSpec((1, D_MODEL), lambda i: (0, 0))
    return pl.pallas_call(
        _route_kernel,
        grid=(n // tm,),
        in_specs=[pl.BlockSpec((tm, D_MODEL), lambda i: (i + tile0, 0)),
                  vec,
                  _flat_mod_spec(layer, 3, ctx, tm, seq_len, tile0),
                  _flat_mod_spec(layer, 4, ctx, tm, seq_len, tile0),
                  pl.BlockSpec((D_MODEL, LANES), lambda i: (0, 0))],
        out_specs=[pl.BlockSpec((tm,) + ROW_TILE, lambda i: (i, 0, 0)),
                   pl.BlockSpec((tm, LANES), lambda i: (i, 0)),
                   pl.BlockSpec((1, LANES), lambda i: (0, 0))],
        out_shape=[jax.ShapeDtypeStruct((n,) + ROW_TILE, F32),
                   jax.ShapeDtypeStruct((n, LANES), F32),
                   jax.ShapeDtypeStruct((1, LANES), F32)],
        scratch_shapes=[pltpu.VMEM((1, LANES), F32)],
        compiler_params=_params(("arbitrary",)),
        name="route",
    )(x2, g, mod, mod, wr_pad)


def _sc_gather(table, idx):
    n_out = idx.shape[0]
    per_worker = n_out // SC_WORKERS
    n_chunks = per_worker // SC_CHUNK
    mesh = plsc.VectorSubcoreMesh(core_axis_name="c", subcore_axis_name="s")

    @functools.partial(
        pl.kernel, mesh=mesh,
        out_type=jax.ShapeDtypeStruct((n_out,) + table.shape[1:], table.dtype),
        scratch_types=[pltpu.VMEM((per_worker,), jnp.int32),
                       pltpu.VMEM((SC_CHUNK,) + table.shape[1:], table.dtype),
                       pltpu.SemaphoreType.DMA],
        name="sc_row_gather")
    def gather(table_hbm, idx_hbm, out_hbm, idx_v, rows_v, sem):
        wid = lax.axis_index("s") * SC_CORES + lax.axis_index("c")
        base = wid * per_worker
        pltpu.sync_copy(idx_hbm.at[pl.ds(base, per_worker)], idx_v)

        @pl.loop(0, n_chunks)
        def _(c):
            off = c * SC_CHUNK
            pltpu.async_copy(table_hbm.at[idx_v.at[pl.ds(off, SC_CHUNK)]], rows_v, sem).wait()
            pltpu.sync_copy(rows_v, out_hbm.at[pl.ds(base + off, SC_CHUNK)])

    return gather(table, idx)


def _sc_scatter(rows, dest):
    n = rows.shape[0]
    n_out = dest.size
    per_worker = n_out // SC_WORKERS
    n_chunks = per_worker // SC_CHUNK
    mesh = plsc.VectorSubcoreMesh(core_axis_name="c", subcore_axis_name="s")

    @functools.partial(
        pl.kernel, mesh=mesh,
        out_type=jax.ShapeDtypeStruct((n_out,) + rows.shape[1:], rows.dtype),
        scratch_types=[pltpu.VMEM((n_chunks, SC_CHUNK), jnp.int32),
                       pltpu.VMEM((SC_CHUNK,) + rows.shape[1:], rows.dtype),
                       pltpu.SemaphoreType.DMA],
        name="sc_row_scatter")
    def scatter(rows_hbm, dest_hbm, out_hbm, dest_v, rows_v, sem):
        wid = lax.axis_index("s") * SC_CORES + lax.axis_index("c")
        src_base = lax.rem(wid * per_worker, n)
        pltpu.sync_copy(dest_hbm.at[pl.ds(wid * n_chunks, n_chunks)], dest_v)

        @pl.loop(0, n_chunks)
        def _(c):
            pltpu.sync_copy(rows_hbm.at[pl.ds(src_base + c * SC_CHUNK, SC_CHUNK)], rows_v)
            pltpu.async_copy(rows_v, out_hbm.at[dest_v.at[c]], sem).wait()

    return scatter(rows, dest)


def _group_kernel(vblk_ref, vexp_ref, vlo_ref, vhi_ref, vfirst_ref, vvalid_ref,
                  h_ref, wg_ref, wu_ref, wd_ref, y_ref):
    del vblk_ref, vexp_ref
    v = pl.program_id(0)

    @pl.when(vvalid_ref[v] == 1)
    def _():
        tg = h_ref.shape[0]
        h = h_ref[...].reshape(tg, D_MODEL).astype(BF16)
        f = _swiglu(h, wg_ref, wu_ref, wd_ref).reshape(y_ref.shape)

        @pl.when(vfirst_ref[v] == 1)
        def _():
            y_ref[...] = f

        @pl.when(vfirst_ref[v] == 0)
        def _():
            row = lax.broadcasted_iota(jnp.int32, (tg, 1, 1), 0)
            mine = jnp.logical_and(row >= vlo_ref[v], row < vhi_ref[v])
            y_ref[...] = jnp.where(mine, f, y_ref[...])


def _grouped_experts(hs, visits, wg, wu, wd):
    n_rows = hs.shape[0]
    tg = GROUP_TM
    nv = visits[0].shape[0]

    def wspec(shape):
        return pl.BlockSpec((1,) + shape, lambda v, blk, exp, *_: (exp[v], 0, 0))

    row_spec = pl.BlockSpec((tg,) + ROW_TILE, lambda v, blk, *_: (blk[v], 0, 0))
    return pl.pallas_call(
        _group_kernel,
        grid_spec=pltpu.PrefetchScalarGridSpec(
            num_scalar_prefetch=len(visits),
            grid=(nv,),
            in_specs=[row_spec, wspec((D_MODEL, D_FF)), wspec((D_MODEL, D_FF)),
                      wspec((D_FF, D_MODEL))],
            out_specs=row_spec),
        out_shape=jax.ShapeDtypeStruct((n_rows,) + ROW_TILE, F32),
        compiler_params=_params(("arbitrary",)),
        name="grouped_experts",
    )(*visits, hs, wg, wu, wd)


def _visit_tables(counts, n_rows):
    tg = GROUP_TM
    nv = n_rows // tg + N_EXPERTS - 1
    ends = jnp.cumsum(counts)
    starts = ends - counts
    first_tile = starts // tg
    last_tile = jnp.maximum(ends - 1, 0) // tg
    nvis = jnp.where(counts > 0, last_tile - first_tile + 1, 0)
    vend = jnp.cumsum(nvis)
    total = vend[-1]
    v = jnp.minimum(jnp.arange(nv, dtype=jnp.int32), total - 1)
    exp = jnp.sum(v[:, None] >= vend[None, :], axis=1).astype(jnp.int32)
    tile = first_tile[exp] + v - (vend[exp] - nvis[exp])
    lo = jnp.maximum(starts[exp], tile * tg) - tile * tg
    hi = jnp.minimum(ends[exp], (tile + 1) * tg) - tile * tg
    first = jnp.concatenate([jnp.ones((1,), jnp.int32), (tile[1:] != tile[:-1]).astype(jnp.int32)])
    valid = (jnp.arange(nv) < total).astype(jnp.int32)
    return tuple(a.astype(jnp.int32) for a in (tile, exp, lo, hi, first, valid)), starts


def _combine_kernel(x_ref, y1_ref, y2_ref, info_ref, gate_ref, gpost_ref, o_ref):
    info = info_ref[...]
    lane = lax.broadcasted_iota(jnp.int32, info.shape, 1)
    f = (_lane_pick(info, lane, PROB_LANE) * y1_ref[...].reshape(x_ref.shape)
         + _lane_pick(info, lane, PROB_LANE + 1) * y2_ref[...].reshape(x_ref.shape))
    o_ref[...] = x_ref[...] + gate_ref[0] * _rms(f, gpost_ref[...])


def _combine_into_kernel(x_ref, y1_ref, y2_ref, info_ref, gate_ref, gpost_ref, prev_ref, o_ref):
    del prev_ref
    _combine_kernel(x_ref, y1_ref, y2_ref, info_ref, gate_ref, gpost_ref, o_ref)


def _combine(x2, y12, info, mod, g_post, prev, *, layer, ctx, seq_len, tile0):
    n = x2.shape[0]
    tm = ROUTE_TM
    nt = info.shape[0] // tm
    row = pl.BlockSpec((tm, D_MODEL), lambda i: (i + tile0, 0))
    in_specs = [row, pl.BlockSpec((tm,) + ROW_TILE, lambda i: (i, 0, 0)),
                pl.BlockSpec((tm,) + ROW_TILE, lambda i: (nt + i, 0, 0)),
                pl.BlockSpec((tm, LANES), lambda i: (i, 0)),
                _flat_mod_spec(layer, 5, ctx, tm, seq_len, tile0),
                pl.BlockSpec((1, D_MODEL), lambda i: (0, 0))]
    args = [x2, y12, y12, info, mod, g_post]
    if prev is not None:
        in_specs.append(pl.BlockSpec(memory_space=pl.ANY))
        args.append(prev)
    return pl.pallas_call(
        _combine_kernel if prev is None else _combine_into_kernel,
        grid=(nt,),
        in_specs=in_specs,
        out_specs=row,
        out_shape=jax.ShapeDtypeStruct((n, D_MODEL), F32),
        input_output_aliases={} if prev is None else {len(args) - 1: 0},
        compiler_params=_params(("parallel",)),
        name="combine",
    )(*args)


def _moe(x, g_pre, g_post, mod, wr_pad, wg, wu, wd, *, layer, ctx):
    bt, seq_len, _ = x.shape
    n = bt * seq_len
    x2 = x.reshape(n, D_MODEL)
    parts = MOE_PARTS if n % (MOE_PARTS * SC_WORKERS * SC_CHUNK) == 0 else 1
    n_part = n // parts
    tiles_part = n_part // ROUTE_TM

    routed = [_route(x2, g_pre, mod, wr_pad, layer=layer, ctx=ctx, seq_len=seq_len,
                     tile0=p * tiles_part, n=n_part) for p in range(parts)]
    sorted_rows = []
    for h, info, cnt in routed:
        counts = cnt[0, :N_EXPERTS].astype(jnp.int32)
        visits, starts = _visit_tables(counts, TOP_K * n_part)
        expert = info[:, EXPERT_LANE:EXPERT_LANE + TOP_K].astype(jnp.int32)
        rank = info[:, RANK_LANE:RANK_LANE + TOP_K].astype(jnp.int32)
        pos = (starts[expert] + rank).T.reshape(-1)
        sorted_rows.append((_sc_scatter(h, pos.reshape(-1, SC_CHUNK)), visits, pos))
    gathered = [_sc_gather(_grouped_experts(hs, visits, wg, wu, wd), pos)
                for hs, visits, pos in sorted_rows]
    out = None
    for p, (y12, (_, info, _)) in enumerate(zip(gathered, routed)):
        out = _combine(x2, y12, info, mod, g_post, out, layer=layer, ctx=ctx, seq_len=seq_len,
                       tile0=p * tiles_part)
    return out.reshape(bt, seq_len, D_MODEL)


def _rope_tables(seq_len):
    rows = jnp.repeat(jnp.arange(seq_len // GRID_W), GRID_W).astype(F32)
    cols = jnp.tile(jnp.arange(GRID_W), seq_len // GRID_W).astype(F32)
    inv = ROPE_BASE ** (-jnp.arange(ROPE_FREQS, dtype=F32) / ROPE_FREQS)
    ang = jnp.concatenate([rows[:, None] * inv, rows[:, None] * inv,
                           cols[:, None] * inv, cols[:, None] * inv], axis=1)
    ang = jnp.tile(ang, (1, LANES // HEAD_QK))
    cos, sin = jnp.cos(ang), jnp.sin(ang)
    low = (jnp.arange(LANES) % (2 * ROPE_FREQS)) < ROPE_FREQS
    return cos, jnp.where(low, -sin, 0.0), jnp.where(low, 0.0, sin)


def kernel(x, c, ctx, c_ctx, w_mod, b_mod, g_pre_mix, g_post_mix, g_pre_ffn, g_post_ffn,
           w_in, lambda_q1, lambda_k1, lambda_q2, lambda_k2, g_subln, conv_w, pool_w,
           pool_scale, w_branch, w_out, ffn_w_gate, ffn_w_up, ffn_w_down, router_w,
           moe_w_gate, moe_w_up, moe_w_down):
    depth = w_in.shape[0]
    bsz, seq, _ = x.shape
    ctx_len = ctx.shape[1]

    cvec = jnp.zeros((MOD_ROWS, D_MODEL), F32).at[:bsz].set(c).at[CTX_MOD_ROW].set(c_ctx)
    mod = _modulation(cvec, w_mod, b_mod).reshape(depth * MOD_ROWS, 1, 6 * D_MODEL)
    tables = _rope_tables(seq)
    no_tables = tuple(t[:ctx_len] for t in tables)

    y = ctx
    for i in range(depth):
        last = i == depth - 1
        lam_init = 0.8 - 0.6 * math.exp(-0.3 * i)
        split = w_in.shape[2] - GATES_W
        w_in_i = jnp.concatenate([0.5 * w_in[i, :, split:], w_in[i, :, :split]], axis=1).astype(BF16)
        lam_vecs = jnp.stack([lambda_q1[i], lambda_k1[i], lambda_q2[i], lambda_k2[i]]).astype(F32)
        g_sub = g_subln[i].reshape(1, HEAD_V)
        g_pm, g_qm = g_pre_mix[i].reshape(1, D_MODEL), g_post_mix[i].reshape(1, D_MODEL)
        g_pf, g_qf = g_pre_ffn[i].reshape(1, D_MODEL), g_post_ffn[i].reshape(1, D_MODEL)
        mix_w = (conv_w[i], pool_w[i].astype(BF16), pool_scale[i].reshape(1, BRANCH_W))
        out_w = (w_branch[i].astype(BF16), (0.5 * w_out[i]).astype(BF16))

        z = _inproj(x, g_pm, mod, w_in_i, tables, mix_w, layer=i, ctx=False, rope=True,
                    col_start=0, tm=512)
        kv_full = (K_COL * HEAD_BLKS, V_COL * HEAD_BLKS)
        if last:
            w_kv = w_in_i[:, K_COL * COL_BLK:(V_COL + 1) * COL_BLK]
            zc = _inproj(y, g_pm, mod, w_kv, no_tables, layer=i, ctx=True, rope=False,
                         col_start=K_COL, tm=ctx_len)
            segs = [(z,) + kv_full, (zc, 0, HEAD_BLKS)]
        else:
            zc = _inproj(y, g_pm, mod, w_in_i, no_tables, mix_w, layer=i, ctx=True, rope=False,
                         col_start=0, tm=ctx_len)
            segs = [(z,) + kv_full, (zc,) + kv_full]
            attn_c = _attention(zc, [(zc,) + kv_full], lam_vecs, g_sub, lam_init=lam_init,
                                tq=ctx_len)
            y = _merge(attn_c, zc, y, mod, g_qm, *out_w, layer=i, ctx=True, tm=ctx_len)
        attn_l = _attention(z, segs, lam_vecs, g_sub, lam_init=lam_init, tq=1024)
        x = _merge(attn_l, z, x, mod, g_qm, *out_w, layer=i, ctx=False, tm=512)

        j = i // 2
        streams = [(x, False, 1024)] + ([] if last else [(y, True, ctx_len)])
        if i % 2 == 0:
            ffn_w = (ffn_w_gate[j:j + 1].astype(BF16), ffn_w_up[j:j + 1].astype(BF16),
                     ffn_w_down[j:j + 1].astype(BF16))
            outs = [_ffn(t, g_pf, g_qf, mod, *ffn_w, layer=i, ctx=is_ctx, tm=tm)
                    for t, is_ctx, tm in streams]
        else:
            ffn_w = (moe_w_gate[j].astype(BF16), moe_w_up[j].astype(BF16), moe_w_down[j].astype(BF16))
            wr_pad = jnp.zeros((D_MODEL, LANES), F32).at[:, :N_EXPERTS].set(router_w[j])
            outs = [_moe(t, g_pf, g_qf, mod, wr_pad, *ffn_w, layer=i, ctx=is_ctx)
                    for t, is_ctx, _ in streams]
        x = outs[0]
        if not last:
            y = outs[1]
    return x
```

```python
import functools
import math

import jax
import jax.numpy as jnp
from jax import lax
from jax.experimental import pallas as pl
from jax.experimental.pallas import tpu as pltpu
from jax.experimental.pallas import tpu_sc as plsc

F32 = jnp.float32
BF16 = jnp.bfloat16

D_MODEL = 1024
GRID_W = 64
N_HEADS = 4
HEAD_QK = 64
HEAD_V = 128
ROPE_BASE = 10000.0
ROPE_FREQS = HEAD_QK // 4
CONV_K = 3
POOL_WINDOWS = (2, 4, 8, 16)
POOL_GW = 128
N_BRANCH = 3
BRANCH_W = 512
GATES_W = N_BRANCH * D_MODEL
IN_W = GATES_W + 7 * BRANCH_W
D_FF = 2816
N_EXPERTS = 8
EPS = 1e-6

LANES = 128
BF16_SUBLANES = 16
COL_BLK = 512
Q_COL, K_COL, V_COL, CB_COL, CC_COL, CX_COL, PIN_COL = range(6, 13)
CONV_OUT, POOL_OUT = CB_COL, CB_COL + 1
HEAD_BLKS = COL_BLK // 128
MOD_ROWS = 16
CTX_MOD_ROW = 8
HALO = 8
FF_CHUNK = 256
KEY_CHUNK = 1024
LOG2_E = 1.4426950408889634
MIN_ROW_SUM = 2.0 ** -88
VMEM_LIMIT = 56 * 1024 * 1024


def _params(sem, vmem=VMEM_LIMIT):
    return pltpu.CompilerParams(dimension_semantics=sem, vmem_limit_bytes=vmem)


def _rms(t, g):
    return t * lax.rsqrt(jnp.mean(t * t, axis=-1, keepdims=True) + EPS) * g


def _mod_kernel(c_ref, w_ref, b_ref, o_ref):
    c = c_ref[...]
    s = c * jax.nn.sigmoid(c)
    o_ref[0] = jnp.dot(s, w_ref[0], preferred_element_type=F32,
                       precision=lax.Precision.HIGHEST) + b_ref[0]


def _modulation(cvec, w_mod, b_mod):
    depth = w_mod.shape[0]
    wcols = w_mod.shape[2]
    tn = 1536
    return pl.pallas_call(
        _mod_kernel,
        grid=(depth, wcols // tn),
        in_specs=[pl.BlockSpec((MOD_ROWS, D_MODEL), lambda l, j: (0, 0)),
                  pl.BlockSpec((1, D_MODEL, tn), lambda l, j: (l, 0, j)),
                  pl.BlockSpec((1, 1, tn), lambda l, j: (l, 0, j))],
        out_specs=pl.BlockSpec((1, MOD_ROWS, tn), lambda l, j: (l, 0, j)),
        out_shape=jax.ShapeDtypeStruct((depth, MOD_ROWS, wcols), F32),
        compiler_params=_params(("parallel", "parallel")),
        name="modulation",
    )(cvec, w_mod, b_mod.reshape(depth, 1, wcols))


def _mod_spec(layer, chunk, ctx, ngrid):
    def idx(*g):
        row = CTX_MOD_ROW if ctx else g[0]
        return (layer * MOD_ROWS + row, 0, chunk)
    del ngrid
    return pl.BlockSpec((1, 1, D_MODEL), idx)


def _inproj_kernel(*refs, rope, col_start, mix, seq_len):
    if mix:
        (x_ref, xp_ref, xn_ref, g_ref, sh_ref, sc_ref, w_ref, cos_ref, sa_ref, sb_ref,
         cw_ref, pw_ref, ps_ref, o_ref, u_scr, p_scr) = refs
        x = jnp.concatenate([x_ref[0], xp_ref[0], xn_ref[0]], axis=0)
    else:
        x_ref, g_ref, sh_ref, sc_ref, w_ref, cos_ref, sa_ref, sb_ref, o_ref = refs
        x = x_ref[0]
    tm = x_ref.shape[1]
    h = (_rms(x, g_ref[...]) * (1.0 + sc_ref[0]) + sh_ref[0]).astype(BF16)
    def mix_branches(cb, cc, cx, pin):
        i = pl.program_id(1)
        first = i == 0
        last = i == pl.num_programs(1) - 1

        def fill(scr, val):
            scr[0:HALO] = jnp.where(first, 0.0, val[tm:tm + HALO])
            scr[HALO:HALO + tm] = val[:tm]
            scr[HALO + tm:2 * HALO + tm] = jnp.where(last, 0.0, val[tm + HALO:])

        fill(u_scr, cc * cx)
        fill(p_scr, pin)
        yield
        cw = cw_ref[...]
        for cs in (slice(0, BRANCH_W // 2), slice(BRANCH_W // 2, BRANCH_W)):
            conv = (cw[0:1, cs] * u_scr[HALO - 1:HALO - 1 + tm, cs] + cw[1:2, cs] * u_scr[HALO:HALO + tm, cs]
                    + cw[2:3, cs] * u_scr[HALO + 1:HALO + 1 + tm, cs])
            o_ref[0, :, CONV_OUT * COL_BLK + cs.start:CONV_OUT * COL_BLK + cs.stop] = (
                cb[:, cs] * conv).astype(BF16)
            yield

        t = i * tm + lax.broadcasted_iota(jnp.int32, (tm, 1), 0)
        for g, w in enumerate(POOL_WINDOWS):
            cs = slice(g * POOL_GW, (g + 1) * POOL_GW)
            acc = p_scr[HALO - w // 2:HALO - w // 2 + tm, cs]
            for d in range(-w // 2 + 1, w // 2):
                acc = acc + p_scr[HALO + d:HALO + d + tm, cs]
            lo = jnp.maximum(t - w // 2, 0)
            hi = jnp.minimum(t - w // 2 + w, seq_len)
            p = acc / (hi - lo).astype(F32) - p_scr[HALO:HALO + tm, cs]
            y = jnp.dot(p.astype(BF16), pw_ref[g], preferred_element_type=F32) * ps_ref[:, cs]
            o_ref[0, :, POOL_OUT * COL_BLK + g * POOL_GW:POOL_OUT * COL_BLK + (g + 1) * POOL_GW] = (
                y.astype(BF16))
            yield

    kept = {}
    pieces = iter(())
    n_blk = w_ref.shape[1] // COL_BLK
    for j in sorted(range(n_blk), key=lambda j: not (mix and j + col_start >= CB_COL)):
        col = j + col_start
        w_blk = w_ref[:, j * COL_BLK:(j + 1) * COL_BLK]
        if mix and col >= CB_COL:
            kept[col] = jnp.dot(h[:tm] if col == CB_COL else h, w_blk, preferred_element_type=F32)
            if len(kept) == 4:
                pieces = mix_branches(*[kept[c] for c in (CB_COL, CC_COL, CX_COL, PIN_COL)])
            continue
        z = jnp.dot(h[:tm], w_blk, preferred_element_type=F32)
        next(pieces, None)
        if rope and col in (Q_COL, K_COL):
            cos, sa, sb = cos_ref[...], sa_ref[...], sb_ref[...]
            for c in range(COL_BLK // LANES):
                t = z[:, c * LANES:(c + 1) * LANES]
                r = (t * cos + pltpu.roll(t, LANES - ROPE_FREQS, 1) * sa
                     + pltpu.roll(t, ROPE_FREQS, 1) * sb)
                o_ref[0, :, j * COL_BLK + c * LANES:j * COL_BLK + (c + 1) * LANES] = r.astype(BF16)
        else:
            o_ref[0, :, j * COL_BLK:(j + 1) * COL_BLK] = z.astype(BF16)
    for _ in pieces:
        pass


def _resident(shape):
    return pl.BlockSpec(shape, lambda *_: (0,) * len(shape), pipeline_mode=pl.Buffered(1))


def _inproj(x, g, mod, w, tables, mix_w=None, *, layer, ctx, rope, col_start, tm):
    bt, L, _ = x.shape
    cos, sa, sb = tables
    mix = mix_w is not None
    out_w = (POOL_OUT + 1) * COL_BLK if mix else w.shape[1]
    tab_spec = pl.BlockSpec((tm, LANES), lambda b, i: (i, 0))
    hb, nhb = tm // HALO, L // HALO
    in_specs = [pl.BlockSpec((1, tm, D_MODEL), lambda b, i: (b, i, 0))]
    args = [x]
    if mix:
        in_specs += [pl.BlockSpec((1, HALO, D_MODEL), lambda b, i: (b, jnp.maximum(i * hb - 1, 0), 0)),
                     pl.BlockSpec((1, HALO, D_MODEL),
                                  lambda b, i: (b, jnp.minimum((i + 1) * hb, nhb - 1), 0))]
        args += [x, x]
    in_specs += [_resident((1, D_MODEL)), _mod_spec(layer, 0, ctx, 2), _mod_spec(layer, 1, ctx, 2),
                 _resident(w.shape), tab_spec, tab_spec, tab_spec]
    args += [g, mod, mod, w, cos, sa, sb]
    scratch = []
    if mix:
        in_specs += [_resident(a.shape) for a in mix_w]
        args += list(mix_w)
        scratch = [pltpu.VMEM((tm + 2 * HALO, BRANCH_W), F32)] * 2
    return pl.pallas_call(
        functools.partial(_inproj_kernel, rope=rope, col_start=col_start, mix=mix, seq_len=L),
        grid=(bt, L // tm),
        in_specs=in_specs,
        out_specs=pl.BlockSpec((1, tm, out_w), lambda b, i: (b, i, 0)),
        out_shape=jax.ShapeDtypeStruct((bt, L, out_w), BF16),
        scratch_shapes=scratch,
        compiler_params=_params(("parallel", "parallel")),
        name="inproj",
    )(*args)


def _map_sums(sq, lane):
    return [jnp.sum(jnp.where(lane < HEAD_QK, sq, 0.0), axis=-1, keepdims=True),
            jnp.sum(jnp.where(lane >= HEAD_QK, sq, 0.0), axis=-1, keepdims=True)]


def _attn_kernel(*refs, nseg, lam_init):
    q_ref, lam_ref, gs_ref = refs[0], refs[1], refs[2]
    kv_refs = refs[3:3 + 2 * nseg]
    o_ref, knorm_scr = refs[3 + 2 * nseg:]

    @pl.when(pl.program_id(2) == 0)
    def _():
        best = [jnp.zeros((1, 1), F32), jnp.zeros((1, 1), F32)]
        for s in range(nseg):
            kf = kv_refs[2 * s][0].astype(F32)
            sums = _map_sums(kf * kf, lax.broadcasted_iota(jnp.int32, kf.shape, 1))
            best = [jnp.maximum(b, jnp.max(n, axis=0, keepdims=True)) for b, n in zip(best, sums)]
        lane1 = lax.broadcasted_iota(jnp.int32, knorm_scr.shape, 1)
        knorm_scr[...] = jnp.where(lane1 == 0, best[0], jnp.where(lane1 == 1, best[1], 0.0))

    lv = lam_ref[...]
    lam = (jnp.exp(jnp.sum(lv[0:1] * lv[1:2], axis=-1, keepdims=True))
           - jnp.exp(jnp.sum(lv[2:3] * lv[3:4], axis=-1, keepdims=True)) + lam_init)

    q = q_ref[0]
    tq = q.shape[0]
    lane = lax.broadcasted_iota(jnp.int32, q.shape, 1)
    qs = (q.astype(F32) * (HEAD_QK ** -0.5 * LOG2_E)).astype(BF16)
    qmap = [jnp.where(lane < HEAD_QK, qs, jnp.zeros_like(qs)),
            jnp.where(lane >= HEAD_QK, qs, jnp.zeros_like(qs))]
    qf = qs.astype(F32)
    knorm = knorm_scr[...]
    bound = [jnp.sqrt(qn * knorm[:, j:j + 1]) for j, qn in enumerate(_map_sums(qf * qf, lane))]
    dn = (((1,), (1,)), ((), ()))

    def chunks():
        for s in range(nseg):
            k_ref, v_ref = kv_refs[2 * s], kv_refs[2 * s + 1]
            for c0 in range(0, k_ref.shape[1], KEY_CHUNK):
                ck = min(KEY_CHUNK, k_ref.shape[1] - c0)
                v = v_ref[0, c0:c0 + ck, :]
                yield k_ref[0, c0:c0 + ck, :], jnp.concatenate([v, jnp.ones_like(v)], axis=1)

    def finish(acc):
        o = (acc[0][:, :HEAD_V] / acc[0][:, HEAD_V:]
             - acc[1][:, :HEAD_V] * (lam / acc[1][:, HEAD_V:]))
        o_ref[0] = (_rms(o, gs_ref[...]) * (1.0 - lam_init)).astype(BF16)

    acc = [jnp.zeros((tq, 2 * HEAD_V), F32) for _ in range(2)]
    for k, v_aug in chunks():
        for j in range(2):
            sc = lax.dot_general(qmap[j], k, dn, preferred_element_type=F32)
            p = jnp.exp2(sc - bound[j]).astype(BF16)
            acc[j] = acc[j] + jnp.dot(p, v_aug, preferred_element_type=F32)
    finish(acc)

    smallest = jnp.min(jnp.minimum(acc[0][:, HEAD_V:], acc[1][:, HEAD_V:]))

    @pl.when(jnp.logical_not(smallest >= MIN_ROW_SUM))
    def _():
        m = [jnp.full((tq, 1), -jnp.inf, F32) for _ in range(2)]
        acc = [jnp.zeros((tq, 2 * HEAD_V), F32) for _ in range(2)]
        for k, v_aug in chunks():
            for j in range(2):
                sc = lax.dot_general(qmap[j], k, dn, preferred_element_type=F32)
                m_new = jnp.maximum(m[j], jnp.max(sc, axis=-1, keepdims=True))
                p = jnp.exp2(sc - m_new).astype(BF16)
                acc[j] = acc[j] * jnp.exp2(m[j] - m_new) + jnp.dot(p, v_aug,
                                                                  preferred_element_type=F32)
                m[j] = m_new
        finish(acc)


def _attention(zq, segs, lam_vecs, g_sub, *, lam_init, tq):
    bt, lq, _ = zq.shape
    q0 = Q_COL * HEAD_BLKS
    in_specs = [pl.BlockSpec((1, tq, HEAD_V), lambda b, h, i: (b, i, q0 + h)),
                pl.BlockSpec((4, HEAD_QK), lambda b, h, i: (0, 0)),
                pl.BlockSpec((1, HEAD_V), lambda b, h, i: (0, 0))]
    args = [zq, lam_vecs, g_sub]
    for arr, kc, vc in segs:
        t = arr.shape[1]
        in_specs.append(pl.BlockSpec((1, t, HEAD_V), lambda b, h, i, kc=kc: (b, 0, kc + h)))
        in_specs.append(pl.BlockSpec((1, t, HEAD_V), lambda b, h, i, vc=vc: (b, 0, vc + h)))
        args += [arr, arr]
    return pl.pallas_call(
        functools.partial(_attn_kernel, nseg=len(segs), lam_init=lam_init),
        grid=(bt, N_HEADS, lq // tq),
        in_specs=in_specs,
        out_specs=pl.BlockSpec((1, tq, HEAD_V), lambda b, h, i: (b, i, h)),
        out_shape=jax.ShapeDtypeStruct((bt, lq, N_HEADS * HEAD_V), BF16),
        scratch_shapes=[pltpu.VMEM((1, LANES), F32)],
        compiler_params=_params(("parallel", "parallel", "arbitrary")),
        name="diff_attention",
    )(*args)


def _merge_kernel(attn_ref, conv_ref, pool_ref, g0_ref, g1_ref, g2_ref, x_ref, gate_ref, gpost_ref,
                  wb_ref, wo_ref, o_ref):
    def gate2(ref):
        return 1.0 + jnp.tanh(ref[0].astype(F32))

    merged2 = (gate2(g0_ref) * jnp.dot(attn_ref[0], wb_ref[0], preferred_element_type=F32)
               + gate2(g1_ref) * jnp.dot(conv_ref[0], wb_ref[1], preferred_element_type=F32)
               + gate2(g2_ref) * jnp.dot(pool_ref[0], wb_ref[2], preferred_element_type=F32))
    mix = jnp.dot(merged2.astype(BF16), wo_ref[...], preferred_element_type=F32)
    o_ref[0] = x_ref[0] + gate_ref[0] * _rms(mix, gpost_ref[...])


def _merge(attn, z, x, mod, g_post, w_branch, w_out, *, layer, ctx, tm):
    bt, L, _ = x.shape

    def col(c, width=COL_BLK):
        return pl.BlockSpec((1, tm, width), lambda b, i, c=c: (b, i, c))

    in_specs = [col(0), col(CONV_OUT), col(POOL_OUT)]
    in_specs += [col(k, D_MODEL) for k in range(N_BRANCH)]
    in_specs += [col(0, D_MODEL), _mod_spec(layer, 2, ctx, 2),
                 _resident(g_post.shape), _resident(w_branch.shape), _resident(w_out.shape)]
    return pl.pallas_call(
        _merge_kernel,
        grid=(bt, L // tm),
        in_specs=in_specs,
        out_specs=col(0, D_MODEL),
        out_shape=jax.ShapeDtypeStruct((bt, L, D_MODEL), F32),
        compiler_params=_params(("parallel", "parallel")),
        name="mixer_merge",
    )(attn, *([z] * 5), x, mod, g_post, w_branch, w_out)


def _swiglu(h, wg_ref, wu_ref, wd_ref):
    f = None
    for c in range(D_FF // FF_CHUNK):
        cs = slice(c * FF_CHUNK, (c + 1) * FF_CHUNK)
        gt = jnp.dot(h, wg_ref[0, :, cs], preferred_element_type=F32)
        up = jnp.dot(h, wu_ref[0, :, cs], preferred_element_type=F32)
        a = (gt * jax.nn.sigmoid(gt) * up).astype(BF16)
        part = jnp.dot(a, wd_ref[0, cs, :], preferred_element_type=F32)
        f = part if f is None else f + part
    return f


def _ffn_kernel(x_ref, g_ref, sh_ref, sc_ref, gate_ref, gpost_ref, wg_ref, wu_ref, wd_ref, o_ref):
    h = (_rms(x_ref[0], g_ref[...]) * (1.0 + sc_ref[0]) + sh_ref[0]).astype(BF16)
    f = _swiglu(h, wg_ref, wu_ref, wd_ref)
    o_ref[0] = x_ref[0] + gate_ref[0] * _rms(f, gpost_ref[...])


def _ffn(x, g_pre, g_post, mod, wg, wu, wd, *, layer, ctx, tm):
    bt, L, _ = x.shape
    vec = pl.BlockSpec((1, D_MODEL), lambda b, i: (0, 0))
    return pl.pallas_call(
        _ffn_kernel,
        grid=(bt, L // tm),
        in_specs=[pl.BlockSpec((1, tm, D_MODEL), lambda b, i: (b, i, 0)),
                  vec,
                  _mod_spec(layer, 3, ctx, 2),
                  _mod_spec(layer, 4, ctx, 2),
                  _mod_spec(layer, 5, ctx, 2),
                  vec,
                  _resident(wg.shape), _resident(wu.shape), _resident(wd.shape)],
        out_specs=pl.BlockSpec((1, tm, D_MODEL), lambda b, i: (b, i, 0)),
        out_shape=jax.ShapeDtypeStruct((bt, L, D_MODEL), F32),
        compiler_params=_params(("parallel", "parallel")),
        name="channel_mixer",
    )(x, g_pre, mod, mod, mod, g_post, wg, wu, wd)


ROUTE_TM = 512
GROUP_TM = 512
TOP_K = 2
MOE_PARTS = 2
ROW_TILE = (8, LANES)
SC_CORES, SC_SUBCORES = 2, 16
SC_WORKERS = SC_CORES * SC_SUBCORES
SC_CHUNK = 32
RANK_LANE, PROB_LANE, EXPERT_LANE = 0, 2, 4


def _flat_mod_spec(layer, chunk, ctx, tm, seq_len, tile0=0):
    def idx(i):
        row = CTX_MOD_ROW if ctx else ((i + tile0) * tm) // seq_len
        return (layer * MOD_ROWS + row, 0, chunk)
    return pl.BlockSpec((1, 1, D_MODEL), idx)


def _lane_pick(v, lane, k):
    return jnp.sum(jnp.where(lane == k, v, 0.0), axis=-1, keepdims=True)


def _route_kernel(x_ref, g_ref, sh_ref, sc_ref, wr_ref, h_ref, info_ref, cnt_ref, run_scr):
    @pl.when(pl.program_id(0) == 0)
    def _():
        run_scr[...] = jnp.zeros_like(run_scr)

    h = _rms(x_ref[...], g_ref[...]) * (1.0 + sc_ref[0]) + sh_ref[0]
    h_ref[...] = h.reshape(h_ref.shape)
    w = wr_ref[...]
    h_hi, w_hi = h.astype(BF16), w.astype(BF16)
    h_lo = (h - h_hi.astype(F32)).astype(BF16)
    w_lo = (w - w_hi.astype(F32)).astype(BF16)
    logits = (jnp.dot(h_hi, w_hi, preferred_element_type=F32)
              + jnp.dot(h_lo, w_hi, preferred_element_type=F32)
              + jnp.dot(h_hi, w_lo, preferred_element_type=F32))
    lane = lax.broadcasted_iota(jnp.int32, logits.shape, 1)
    neg = jnp.float32(-jnp.inf)
    l1 = jnp.where(lane < N_EXPERTS, logits, neg)
    m1 = jnp.max(l1, axis=-1, keepdims=True)
    i1 = jnp.min(jnp.where(l1 == m1, lane, LANES), axis=-1, keepdims=True)
    l2 = jnp.where(lane == i1, neg, l1)
    m2 = jnp.max(l2, axis=-1, keepdims=True)
    i2 = jnp.min(jnp.where(l2 == m2, lane, LANES), axis=-1, keepdims=True)
    e2 = jnp.exp(m2 - m1)
    den = 1.0 + e2

    chosen = jnp.where(jnp.logical_or(lane == i1, lane == i2), 1.0, 0.0)
    tm = chosen.shape[0]
    earlier = (lax.broadcasted_iota(jnp.int32, (tm, tm), 1)
               < lax.broadcasted_iota(jnp.int32, (tm, tm), 0))
    before = jnp.dot(jnp.where(earlier, 1.0, 0.0).astype(BF16), chosen.astype(BF16),
                     preferred_element_type=F32)
    rank = before + run_scr[...]
    rec = [_lane_pick(rank, lane, i1), _lane_pick(rank, lane, i2), 1.0 / den, e2 / den,
           i1.astype(F32), i2.astype(F32)]
    info = jnp.zeros_like(logits)
    for k, v in enumerate(rec):
        info = jnp.where(lane == k, v, info)
    info_ref[...] = info
    run_scr[...] += jnp.sum(chosen, axis=0, keepdims=True)
    cnt_ref[...] = run_scr[...]


def _route(x2, g, mod, wr_pad, *, layer, ctx, seq_len, tile0, n):
    tm = ROUTE_TM
    vec = pl.BlockSpec((1, D_MODEL), lambda i: (0, 0))
    return pl.pallas_call(
        _route_kernel,
        grid=(n // tm,),
        in_specs=[pl.BlockSpec((tm, D_MODEL), lambda i: (i + tile0, 0)),
                  vec,
                  _flat_mod_spec(layer, 3, ctx, tm, seq_len, tile0),
                  _flat_mod_spec(layer, 4, ctx, tm, seq_len, tile0),
                  pl.BlockSpec((D_MODEL, LANES), lambda i: (0, 0))],
        out_specs=[pl.BlockSpec((tm,) + ROW_TILE, lambda i: (i, 0, 0)),
                   pl.BlockSpec((tm, LANES), lambda i: (i, 0)),
                   pl.BlockSpec((1, LANES), lambda i: (0, 0))],
        out_shape=[jax.ShapeDtypeStruct((n,) + ROW_TILE, F32),
                   jax.ShapeDtypeStruct((n, LANES), F32),
                   jax.ShapeDtypeStruct((1, LANES), F32)],
        scratch_shapes=[pltpu.VMEM((1, LANES), F32)],
        compiler_params=_params(("arbitrary",)),
        name="route",
    )(x2, g, mod, mod, wr_pad)


def _sc_gather(table, idx):
    n_out = idx.shape[0]
    per_worker = n_out // SC_WORKERS
    n_chunks = per_worker // SC_CHUNK
    mesh = plsc.VectorSubcoreMesh(core_axis_name="c", subcore_axis_name="s")

    @functools.partial(
        pl.kernel, mesh=mesh,
        out_type=jax.ShapeDtypeStruct((n_out,) + table.shape[1:], table.dtype),
        scratch_types=[pltpu.VMEM((per_worker,), jnp.int32),
                       pltpu.VMEM((SC_CHUNK,) + table.shape[1:], table.dtype),
                       pltpu.SemaphoreType.DMA],
        name="sc_row_gather")
    def gather(table_hbm, idx_hbm, out_hbm, idx_v, rows_v, sem):
        wid = lax.axis_index("s") * SC_CORES + lax.axis_index("c")
        base = wid * per_worker
        pltpu.sync_copy(idx_hbm.at[pl.ds(base, per_worker)], idx_v)

        @pl.loop(0, n_chunks)
        def _(c):
            off = c * SC_CHUNK
            pltpu.async_copy(table_hbm.at[idx_v.at[pl.ds(off, SC_CHUNK)]], rows_v, sem).wait()
            pltpu.sync_copy(rows_v, out_hbm.at[pl.ds(base + off, SC_CHUNK)])

    return gather(table, idx)


def _sc_scatter(rows, dest):
    n = rows.shape[0]
    n_out = dest.size
    per_worker = n_out // SC_WORKERS
    n_chunks = per_worker // SC_CHUNK
    mesh = plsc.VectorSubcoreMesh(core_axis_name="c", subcore_axis_name="s")

    @functools.partial(
        pl.kernel, mesh=mesh,
        out_type=jax.ShapeDtypeStruct((n_out,) + rows.shape[1:], rows.dtype),
        scratch_types=[pltpu.VMEM((n_chunks, SC_CHUNK), jnp.int32),
                       pltpu.VMEM((SC_CHUNK,) + rows.shape[1:], rows.dtype),
                       pltpu.SemaphoreType.DMA],
        name="sc_row_scatter")
    def scatter(rows_hbm, dest_hbm, out_hbm, dest_v, rows_v, sem):
        wid = lax.axis_index("s") * SC_CORES + lax.axis_index("c")
        src_base = lax.rem(wid * per_worker, n)
        pltpu.sync_copy(dest_hbm.at[pl.ds(wid * n_chunks, n_chunks)], dest_v)

        @pl.loop(0, n_chunks)
        def _(c):
            pltpu.sync_copy(rows_hbm.at[pl.ds(src_base + c * SC_CHUNK, SC_CHUNK)], rows_v)
            pltpu.async_copy(rows_v, out_hbm.at[dest_v.at[c]], sem).wait()

    return scatter(rows, dest)


def _group_kernel(vblk_ref, vexp_ref, vlo_ref, vhi_ref, vfirst_ref, vvalid_ref,
                  h_ref, wg_ref, wu_ref, wd_ref, y_ref):
    del vblk_ref, vexp_ref
    v = pl.program_id(0)

    @pl.when(vvalid_ref[v] == 1)
    def _():
        tg = h_ref.shape[0]
        h = h_ref[...].reshape(tg, D_MODEL).astype(BF16)
        f = _swiglu(h, wg_ref, wu_ref, wd_ref).reshape(y_ref.shape)

        @pl.when(vfirst_ref[v] == 1)
        def _():
            y_ref[...] = f

        @pl.when(vfirst_ref[v] == 0)
        def _():
            row = lax.broadcasted_iota(jnp.int32, (tg, 1, 1), 0)
            mine = jnp.logical_and(row >= vlo_ref[v], row < vhi_ref[v])
            y_ref[...] = jnp.where(mine, f, y_ref[...])


def _grouped_experts(hs, visits, wg, wu, wd):
    n_rows = hs.shape[0]
    tg = GROUP_TM
    nv = visits[0].shape[0]

    def wspec(shape):
        return pl.BlockSpec((1,) + shape, lambda v, blk, exp, *_: (exp[v], 0, 0))

    row_spec = pl.BlockSpec((tg,) + ROW_TILE, lambda v, blk, *_: (blk[v], 0, 0))
    return pl.pallas_call(
        _group_kernel,
        grid_spec=pltpu.PrefetchScalarGridSpec(
            num_scalar_prefetch=len(visits),
            grid=(nv,),
            in_specs=[row_spec, wspec((D_MODEL, D_FF)), wspec((D_MODEL, D_FF)),
                      wspec((D_FF, D_MODEL))],
            out_specs=row_spec),
        out_shape=jax.ShapeDtypeStruct((n_rows,) + ROW_TILE, F32),
        compiler_params=_params(("arbitrary",)),
        name="grouped_experts",
    )(*visits, hs, wg, wu, wd)


def _visit_tables(counts, n_rows):
    tg = GROUP_TM
    nv = n_rows // tg + N_EXPERTS - 1
    ends = jnp.cumsum(counts)
    starts = ends - counts
    first_tile = starts // tg
    last_tile = jnp.maximum(ends - 1, 0) // tg
    nvis = jnp.where(counts > 0, last_tile - first_tile + 1, 0)
    vend = jnp.cumsum(nvis)
    total = vend[-1]
    v = jnp.minimum(jnp.arange(nv, dtype=jnp.int32), total - 1)
    mine = jnp.logical_and(v[:, None] >= (vend - nvis)[None, :], v[:, None] < vend[None, :])

    def pick(per_expert):
        return jnp.sum(jnp.where(mine, per_expert[None, :], 0), axis=1).astype(jnp.int32)

    exp = pick(jnp.arange(N_EXPERTS, dtype=jnp.int32))
    tile = v + pick(first_tile - (vend - nvis))
    lo = jnp.maximum(pick(starts), tile * tg) - tile * tg
    hi = jnp.minimum(pick(ends), (tile + 1) * tg) - tile * tg
    first = jnp.concatenate([jnp.ones((1,), jnp.int32), (tile[1:] != tile[:-1]).astype(jnp.int32)])
    valid = (jnp.arange(nv) < total).astype(jnp.int32)
    return tuple(a.astype(jnp.int32) for a in (tile, exp, lo, hi, first, valid)), starts


def _combine_kernel(x_ref, y1_ref, y2_ref, info_ref, gate_ref, gpost_ref, o_ref):
    info = info_ref[...]
    lane = lax.broadcasted_iota(jnp.int32, info.shape, 1)
    f = (_lane_pick(info, lane, PROB_LANE) * y1_ref[...].reshape(x_ref.shape)
         + _lane_pick(info, lane, PROB_LANE + 1) * y2_ref[...].reshape(x_ref.shape))
    o_ref[...] = x_ref[...] + gate_ref[0] * _rms(f, gpost_ref[...])


def _combine_into_kernel(x_ref, y1_ref, y2_ref, info_ref, gate_ref, gpost_ref, prev_ref, o_ref):
    del prev_ref
    _combine_kernel(x_ref, y1_ref, y2_ref, info_ref, gate_ref, gpost_ref, o_ref)


def _combine(x2, y12, info, mod, g_post, prev, *, layer, ctx, seq_len, tile0):
    n = x2.shape[0]
    tm = ROUTE_TM
    nt = info.shape[0] // tm
    row = pl.BlockSpec((tm, D_MODEL), lambda i: (i + tile0, 0))
    in_specs = [row, pl.BlockSpec((tm,) + ROW_TILE, lambda i: (i, 0, 0)),
                pl.BlockSpec((tm,) + ROW_TILE, lambda i: (nt + i, 0, 0)),
                pl.BlockSpec((tm, LANES), lambda i: (i, 0)),
                _flat_mod_spec(layer, 5, ctx, tm, seq_len, tile0),
                pl.BlockSpec((1, D_MODEL), lambda i: (0, 0))]
    args = [x2, y12, y12, info, mod, g_post]
    if prev is not None:
        in_specs.append(pl.BlockSpec(memory_space=pl.ANY))
        args.append(prev)
    return pl.pallas_call(
        _combine_kernel if prev is None else _combine_into_kernel,
        grid=(nt,),
        in_specs=in_specs,
        out_specs=row,
        out_shape=jax.ShapeDtypeStruct((n, D_MODEL), F32),
        input_output_aliases={} if prev is None else {len(args) - 1: 0},
        compiler_params=_params(("parallel",)),
        name="combine",
    )(*args)


def _moe(x, g_pre, g_post, mod, wr_pad, wg, wu, wd, *, layer, ctx):
    bt, seq_len, _ = x.shape
    n = bt * seq_len
    x2 = x.reshape(n, D_MODEL)
    parts = MOE_PARTS if n % (MOE_PARTS * SC_WORKERS * SC_CHUNK) == 0 else 1
    n_part = n // parts
    tiles_part = n_part // ROUTE_TM

    routed = [_route(x2, g_pre, mod, wr_pad, layer=layer, ctx=ctx, seq_len=seq_len,
                     tile0=p * tiles_part, n=n_part) for p in range(parts)]
    sorted_rows = []
    for h, info, cnt in routed:
        counts = cnt[0, :N_EXPERTS].astype(jnp.int32)
        visits, starts = _visit_tables(counts, TOP_K * n_part)
        pos = []
        for k in range(TOP_K):
            expert = info[:, EXPERT_LANE + k].astype(jnp.int32)
            first_row = sum(jnp.where(expert == e, starts[e], 0) for e in range(N_EXPERTS))
            pos.append(first_row + info[:, RANK_LANE + k].astype(jnp.int32))
        pos = jnp.concatenate(pos)
        sorted_rows.append((_sc_scatter(h, pos.reshape(-1, SC_CHUNK)), visits, pos))
    gathered = [_sc_gather(_grouped_experts(hs, visits, wg, wu, wd), pos)
                for hs, visits, pos in sorted_rows]
    out = None
    for p, (y12, (_, info, _)) in enumerate(zip(gathered, routed)):
        out = _combine(x2, y12, info, mod, g_post, out, layer=layer, ctx=ctx, seq_len=seq_len,
                       tile0=p * tiles_part)
    return out.reshape(bt, seq_len, D_MODEL)


def _cast_kernel(x_ref, o_ref):
    o_ref[...] = x_ref[...].astype(BF16)


def _to_bf16(w, rows):
    e, r, c = w.shape
    spec = pl.BlockSpec((1, rows, c), lambda i, j: (i, j, 0))
    return pl.pallas_call(
        _cast_kernel, grid=(e, r // rows), in_specs=[spec], out_specs=spec,
        out_shape=jax.ShapeDtypeStruct(w.shape, BF16),
        compiler_params=_params(("parallel", "parallel")), name="to_bf16")(w)


def _rope_tables(seq_len):
    rows = jnp.repeat(jnp.arange(seq_len // GRID_W), GRID_W).astype(F32)
    cols = jnp.tile(jnp.arange(GRID_W), seq_len // GRID_W).astype(F32)
    inv = ROPE_BASE ** (-jnp.arange(ROPE_FREQS, dtype=F32) / ROPE_FREQS)
    ang = jnp.concatenate([rows[:, None] * inv, rows[:, None] * inv,
                           cols[:, None] * inv, cols[:, None] * inv], axis=1)
    ang = jnp.tile(ang, (1, LANES // HEAD_QK))
    cos, sin = jnp.cos(ang), jnp.sin(ang)
    low = (jnp.arange(LANES) % (2 * ROPE_FREQS)) < ROPE_FREQS
    return cos, jnp.where(low, -sin, 0.0), jnp.where(low, 0.0, sin)


def kernel(x, c, ctx, c_ctx, w_mod, b_mod, g_pre_mix, g_post_mix, g_pre_ffn, g_post_ffn,
           w_in, lambda_q1, lambda_k1, lambda_q2, lambda_k2, g_subln, conv_w, pool_w,
           pool_scale, w_branch, w_out, ffn_w_gate, ffn_w_up, ffn_w_down, router_w,
           moe_w_gate, moe_w_up, moe_w_down):
    depth = w_in.shape[0]
    bsz, seq, _ = x.shape
    ctx_len = ctx.shape[1]

    cvec = jnp.zeros((MOD_ROWS, D_MODEL), F32).at[:bsz].set(c).at[CTX_MOD_ROW].set(c_ctx)
    mod = _modulation(cvec, w_mod, b_mod).reshape(depth * MOD_ROWS, 1, 6 * D_MODEL)
    tables = _rope_tables(seq)
    no_tables = tuple(t[:ctx_len] for t in tables)

    y = ctx
    for i in range(depth):
        last = i == depth - 1
        lam_init = 0.8 - 0.6 * math.exp(-0.3 * i)
        split = w_in.shape[2] - GATES_W
        w_in_i = jnp.concatenate([0.5 * w_in[i, :, split:], w_in[i, :, :split]], axis=1).astype(BF16)
        lam_vecs = jnp.stack([lambda_q1[i], lambda_k1[i], lambda_q2[i], lambda_k2[i]]).astype(F32)
        g_sub = g_subln[i].reshape(1, HEAD_V)
        g_pm, g_qm = g_pre_mix[i].reshape(1, D_MODEL), g_post_mix[i].reshape(1, D_MODEL)
        g_pf, g_qf = g_pre_ffn[i].reshape(1, D_MODEL), g_post_ffn[i].reshape(1, D_MODEL)
        mix_w = (conv_w[i], pool_w[i].astype(BF16), pool_scale[i].reshape(1, BRANCH_W))
        out_w = (w_branch[i].astype(BF16), (0.5 * w_out[i]).astype(BF16))

        z = _inproj(x, g_pm, mod, w_in_i, tables, mix_w, layer=i, ctx=False, rope=True,
                    col_start=0, tm=512)
        kv_full = (K_COL * HEAD_BLKS, V_COL * HEAD_BLKS)
        if last:
            w_kv = w_in_i[:, K_COL * COL_BLK:(V_COL + 1) * COL_BLK]
            zc = _inproj(y, g_pm, mod, w_kv, no_tables, layer=i, ctx=True, rope=False,
                         col_start=K_COL, tm=ctx_len)
            segs = [(z,) + kv_full, (zc, 0, HEAD_BLKS)]
        else:
            zc = _inproj(y, g_pm, mod, w_in_i, no_tables, mix_w, layer=i, ctx=True, rope=False,
                         col_start=0, tm=ctx_len)
            segs = [(z,) + kv_full, (zc,) + kv_full]
            attn_c = _attention(zc, [(zc,) + kv_full], lam_vecs, g_sub, lam_init=lam_init,
                                tq=ctx_len)
            y = _merge(attn_c, zc, y, mod, g_qm, *out_w, layer=i, ctx=True, tm=ctx_len)
        attn_l = _attention(z, segs, lam_vecs, g_sub, lam_init=lam_init, tq=1024)
        x = _merge(attn_l, z, x, mod, g_qm, *out_w, layer=i, ctx=False, tm=512)

        j = i // 2
        streams = [(x, False, 1024)] + ([] if last else [(y, True, ctx_len)])
        if i % 2 == 0:
            ffn_w = (ffn_w_gate[j:j + 1].astype(BF16), ffn_w_up[j:j + 1].astype(BF16),
                     ffn_w_down[j:j + 1].astype(BF16))
            outs = [_ffn(t, g_pf, g_qf, mod, *ffn_w, layer=i, ctx=is_ctx, tm=tm)
                    for t, is_ctx, tm in streams]
        else:
            ffn_w = (moe_w_gate[j].astype(BF16), moe_w_up[j].astype(BF16),
                     _to_bf16(moe_w_down[j], D_FF // 4))
            wr_pad = jnp.zeros((D_MODEL, LANES), F32).at[:, :N_EXPERTS].set(router_w[j])
            outs = [_moe(t, g_pf, g_qf, mod, wr_pad, *ffn_w, layer=i, ctx=is_ctx)
                    for t, is_ctx, _ in streams]
        x = outs[0]
        if not last:
            y = outs[1]
    return x
```

```python
import functools
import itertools
import math

import jax
import jax.numpy as jnp
from jax import lax
from jax.experimental import pallas as pl
from jax.experimental.pallas import tpu as pltpu
from jax.experimental.pallas import tpu_sc as plsc

F32 = jnp.float32
BF16 = jnp.bfloat16

D_MODEL = 1024
GRID_W = 64
N_HEADS = 4
HEAD_QK = 64
HEAD_V = 128
ROPE_BASE = 10000.0
ROPE_FREQS = HEAD_QK // 4
CONV_K = 3
POOL_WINDOWS = (2, 4, 8, 16)
POOL_GW = 128
N_BRANCH = 3
BRANCH_W = 512
GATES_W = N_BRANCH * D_MODEL
IN_W = GATES_W + 7 * BRANCH_W
D_FF = 2816
N_EXPERTS = 8
EPS = 1e-6

LANES = 128
BF16_SUBLANES = 16
COL_BLK = 512
Q_COL, K_COL, V_COL, CB_COL, CC_COL, CX_COL, PIN_COL = range(6, 13)
CONV_OUT, POOL_OUT = CB_COL, CB_COL + 1
HEAD_BLKS = COL_BLK // 128
MOD_ROWS = 16
CTX_MOD_ROW = 8
HALO = 8
FF_CHUNK = 256
KEY_CHUNK = 1024
LOG2_E = 1.4426950408889634
MIN_ROW_SUM = 2.0 ** -88
VMEM_LIMIT = 56 * 1024 * 1024


def _params(sem, vmem=VMEM_LIMIT):
    return pltpu.CompilerParams(dimension_semantics=sem, vmem_limit_bytes=vmem)


def _rms(t, g):
    return t * lax.rsqrt(jnp.mean(t * t, axis=-1, keepdims=True) + EPS) * g


def _mod_kernel(c_ref, w_ref, b_ref, o_ref):
    c = c_ref[...]
    s = c * jax.nn.sigmoid(c)
    o_ref[0] = jnp.dot(s, w_ref[0], preferred_element_type=F32,
                       precision=lax.Precision.HIGHEST) + b_ref[0]


def _modulation(cvec, w_mod, b_mod):
    depth = w_mod.shape[0]
    wcols = w_mod.shape[2]
    tn = 1536
    return pl.pallas_call(
        _mod_kernel,
        grid=(depth, wcols // tn),
        in_specs=[pl.BlockSpec((MOD_ROWS, D_MODEL), lambda l, j: (0, 0)),
                  pl.BlockSpec((1, D_MODEL, tn), lambda l, j: (l, 0, j)),
                  pl.BlockSpec((1, 1, tn), lambda l, j: (l, 0, j))],
        out_specs=pl.BlockSpec((1, MOD_ROWS, tn), lambda l, j: (l, 0, j)),
        out_shape=jax.ShapeDtypeStruct((depth, MOD_ROWS, wcols), F32),
        compiler_params=_params(("parallel", "parallel")),
        name="modulation",
    )(cvec, w_mod, b_mod.reshape(depth, 1, wcols))


def _mod_spec(layer, chunk, ctx, ngrid):
    def idx(*g):
        row = CTX_MOD_ROW if ctx else g[0]
        return (layer * MOD_ROWS + row, 0, chunk)
    del ngrid
    return pl.BlockSpec((1, 1, D_MODEL), idx)


def _inproj_kernel(*refs, rope, col_start, mix, seq_len):
    if mix:
        (x_ref, xp_ref, xn_ref, g_ref, sh_ref, sc_ref, w_ref, cos_ref, sa_ref, sb_ref,
         cw_ref, pw_ref, ps_ref, o_ref, kn_ref, u_scr, p_scr) = refs
        x = jnp.concatenate([x_ref[0], xp_ref[0], xn_ref[0]], axis=0)
    else:
        x_ref, g_ref, sh_ref, sc_ref, w_ref, cos_ref, sa_ref, sb_ref, o_ref, kn_ref = refs
        x = x_ref[0]
    tm = x_ref.shape[1]
    h = (_rms(x, g_ref[...]) * (1.0 + sc_ref[0]) + sh_ref[0]).astype(BF16)
    def mix_branches(cb, cc, cx, pin):
        i = pl.program_id(1)
        first = i == 0
        last = i == pl.num_programs(1) - 1

        def fill(scr, val):
            scr[0:HALO] = jnp.where(first, 0.0, val[tm:tm + HALO])
            scr[HALO:HALO + tm] = val[:tm]
            scr[HALO + tm:2 * HALO + tm] = jnp.where(last, 0.0, val[tm + HALO:])

        fill(u_scr, cc * cx)
        fill(p_scr, pin)
        yield
        cw = cw_ref[...]
        for cs in (slice(0, BRANCH_W // 2), slice(BRANCH_W // 2, BRANCH_W)):
            conv = (cw[0:1, cs] * u_scr[HALO - 1:HALO - 1 + tm, cs] + cw[1:2, cs] * u_scr[HALO:HALO + tm, cs]
                    + cw[2:3, cs] * u_scr[HALO + 1:HALO + 1 + tm, cs])
            o_ref[0, :, CONV_OUT * COL_BLK + cs.start:CONV_OUT * COL_BLK + cs.stop] = (
                cb[:, cs] * conv).astype(BF16)
            yield

        t = i * tm + lax.broadcasted_iota(jnp.int32, (tm, 1), 0)
        for g, w in enumerate(POOL_WINDOWS):
            cs = slice(g * POOL_GW, (g + 1) * POOL_GW)
            acc = p_scr[HALO - w // 2:HALO - w // 2 + tm, cs]
            for d in range(-w // 2 + 1, w // 2):
                acc = acc + p_scr[HALO + d:HALO + d + tm, cs]
            lo = jnp.maximum(t - w // 2, 0)
            hi = jnp.minimum(t - w // 2 + w, seq_len)
            p = acc / (hi - lo).astype(F32) - p_scr[HALO:HALO + tm, cs]
            y = jnp.dot(p.astype(BF16), pw_ref[g], preferred_element_type=F32) * ps_ref[:, cs]
            o_ref[0, :, POOL_OUT * COL_BLK + g * POOL_GW:POOL_OUT * COL_BLK + (g + 1) * POOL_GW] = (
                y.astype(BF16))
            yield

    kept = {}
    pieces = iter(())
    n_blk = w_ref.shape[1] // COL_BLK

    def key_norms(heads):
        lane = lax.broadcasted_iota(jnp.int32, (tm, LANES), 1)
        lane1 = lax.broadcasted_iota(jnp.int32, (1, LANES), 1)
        rec = jnp.zeros((1, LANES), F32)
        for c, t in enumerate(heads):
            tf = t.astype(F32)
            for m, n in enumerate(_map_sums(tf * tf, lane)):
                rec = jnp.where(lane1 == 2 * c + m, jnp.max(n, axis=0, keepdims=True), rec)
            yield
        kn_ref[0, 0] = rec
        yield

    def priority(j):
        col = j + col_start
        return 0 if mix and col >= CB_COL else 1 if col == K_COL else 2

    for j in sorted(range(n_blk), key=priority):
        col = j + col_start
        w_blk = w_ref[:, j * COL_BLK:(j + 1) * COL_BLK]
        if mix and col >= CB_COL:
            kept[col] = jnp.dot(h[:tm] if col == CB_COL else h, w_blk, preferred_element_type=F32)
            if len(kept) == 4:
                pieces = mix_branches(*[kept[c] for c in (CB_COL, CC_COL, CX_COL, PIN_COL)])
            continue
        z = jnp.dot(h[:tm], w_blk, preferred_element_type=F32)
        next(pieces, None)
        heads = [z[:, c * LANES:(c + 1) * LANES] for c in range(COL_BLK // LANES)]
        if rope and col in (Q_COL, K_COL):
            cos, sa, sb = cos_ref[...], sa_ref[...], sb_ref[...]
            heads = [t * cos + pltpu.roll(t, LANES - ROPE_FREQS, 1) * sa
                     + pltpu.roll(t, ROPE_FREQS, 1) * sb for t in heads]
        heads = [t.astype(BF16) for t in heads]
        for c, t in enumerate(heads):
            o_ref[0, :, j * COL_BLK + c * LANES:j * COL_BLK + (c + 1) * LANES] = t
        if col == K_COL:
            pieces = itertools.chain(pieces, key_norms(heads))
    for _ in pieces:
        pass


def _resident(shape):
    return pl.BlockSpec(shape, lambda *_: (0,) * len(shape), pipeline_mode=pl.Buffered(1))


def _inproj(x, g, mod, w, tables, mix_w=None, *, layer, ctx, rope, col_start, tm):
    bt, L, _ = x.shape
    cos, sa, sb = tables
    mix = mix_w is not None
    out_w = (POOL_OUT + 1) * COL_BLK if mix else w.shape[1]
    tab_spec = pl.BlockSpec((tm, LANES), lambda b, i: (i, 0))
    hb, nhb = tm // HALO, L // HALO
    in_specs = [pl.BlockSpec((1, tm, D_MODEL), lambda b, i: (b, i, 0))]
    args = [x]
    if mix:
        in_specs += [pl.BlockSpec((1, HALO, D_MODEL), lambda b, i: (b, jnp.maximum(i * hb - 1, 0), 0)),
                     pl.BlockSpec((1, HALO, D_MODEL),
                                  lambda b, i: (b, jnp.minimum((i + 1) * hb, nhb - 1), 0))]
        args += [x, x]
    in_specs += [_resident((1, D_MODEL)), _mod_spec(layer, 0, ctx, 2), _mod_spec(layer, 1, ctx, 2),
                 _resident(w.shape), tab_spec, tab_spec, tab_spec]
    args += [g, mod, mod, w, cos, sa, sb]
    scratch = []
    if mix:
        in_specs += [_resident(a.shape) for a in mix_w]
        args += list(mix_w)
        scratch = [pltpu.VMEM((tm + 2 * HALO, BRANCH_W), F32)] * 2
    return pl.pallas_call(
        functools.partial(_inproj_kernel, rope=rope, col_start=col_start, mix=mix, seq_len=L),
        grid=(bt, L // tm),
        in_specs=in_specs,
        out_specs=[pl.BlockSpec((1, tm, out_w), lambda b, i: (b, i, 0)),
                   pl.BlockSpec((1, 1, 1, LANES), lambda b, i: (b, i, 0, 0))],
        out_shape=[jax.ShapeDtypeStruct((bt, L, out_w), BF16),
                   jax.ShapeDtypeStruct((bt, L // tm, 1, LANES), F32)],
        scratch_shapes=scratch,
        compiler_params=_params(("parallel", "parallel")),
        name="inproj",
    )(*args)


def _map_sums(sq, lane):
    return [jnp.sum(jnp.where(lane < HEAD_QK, sq, 0.0), axis=-1, keepdims=True),
            jnp.sum(jnp.where(lane >= HEAD_QK, sq, 0.0), axis=-1, keepdims=True)]


def _attn_kernel(*refs, nseg, lam_init):
    q_ref, lam_ref, gs_ref, knorm_ref = refs[:4]
    kv_refs = refs[4:4 + 2 * nseg]
    o_ref = refs[4 + 2 * nseg]

    lv = lam_ref[...]
    lam = (jnp.exp(jnp.sum(lv[0:1] * lv[1:2], axis=-1, keepdims=True))
           - jnp.exp(jnp.sum(lv[2:3] * lv[3:4], axis=-1, keepdims=True)) + lam_init)

    q = q_ref[0]
    tq = q.shape[0]
    lane = lax.broadcasted_iota(jnp.int32, q.shape, 1)
    qs = (q.astype(F32) * (HEAD_QK ** -0.5 * LOG2_E)).astype(BF16)
    qmap = [jnp.where(lane < HEAD_QK, qs, jnp.zeros_like(qs)),
            jnp.where(lane >= HEAD_QK, qs, jnp.zeros_like(qs))]
    qf = qs.astype(F32)
    knorm = knorm_ref[0]
    bound =[jnp.sqrt(qn * knorm[:, j:j + 1]) for j, qn in enumerate(_map_sums(qf * qf, lane))]
    dn = (((1,), (1,)), ((), ()))

    def chunks():
        for s in range(nseg):
            k_ref, v_ref = kv_refs[2 * s], kv_refs[2 * s + 1]
            for c0 in range(0, k_ref.shape[1], KEY_CHUNK):
                ck = min(KEY_CHUNK, k_ref.shape[1] - c0)
                v = v_ref[0, c0:c0 + ck, :]
                yield k_ref[0, c0:c0 + ck, :], jnp.concatenate([v, jnp.ones_like(v)], axis=1)

    def finish(acc):
        o = (acc[0][:, :HEAD_V] / acc[0][:, HEAD_V:]
             - acc[1][:, :HEAD_V] * (lam / acc[1][:, HEAD_V:]))
        o_ref[0] = (_rms(o, gs_ref[...]) * (1.0 - lam_init)).astype(BF16)

    acc = [jnp.zeros((tq, 2 * HEAD_V), F32) for _ in range(2)]
    for k, v_aug in chunks():
        for j in range(2):
            sc = lax.dot_general(qmap[j], k, dn, preferred_element_type=F32)
            p = jnp.exp2(sc - bound[j]).astype(BF16)
            acc[j] = acc[j] + jnp.dot(p, v_aug, preferred_element_type=F32)
    finish(acc)

    smallest = jnp.min(jnp.minimum(acc[0][:, HEAD_V:], acc[1][:, HEAD_V:]))

    @pl.when(jnp.logical_not(smallest >= MIN_ROW_SUM))
    def _():
        m = [jnp.full((tq, 1), -jnp.inf, F32) for _ in range(2)]
        acc = [jnp.zeros((tq, 2 * HEAD_V), F32) for _ in range(2)]
        for k, v_aug in chunks():
            for j in range(2):
                sc = lax.dot_general(qmap[j], k, dn, preferred_element_type=F32)
                m_new = jnp.maximum(m[j], jnp.max(sc, axis=-1, keepdims=True))
                p = jnp.exp2(sc - m_new).astype(BF16)
                acc[j] = acc[j] * jnp.exp2(m[j] - m_new) + jnp.dot(p, v_aug,
                                                                  preferred_element_type=F32)
                m[j] = m_new
        finish(acc)


def _attention(zq, segs, key_norms, lam_vecs, g_sub, *, lam_init, tq):
    bt, lq, _ = zq.shape
    q0 = Q_COL * HEAD_BLKS
    best = functools.reduce(jnp.maximum, [jnp.max(kn, axis=(1, 2)) for kn in key_norms])
    knorm = jnp.pad(best[:, :2 * N_HEADS].reshape(bt * N_HEADS, 1, 2), ((0, 0), (0, 0), (0, LANES - 2)))
    in_specs = [pl.BlockSpec((1, tq, HEAD_V), lambda b, h, i: (b, i, q0 + h)),
                pl.BlockSpec((4, HEAD_QK), lambda b, h, i: (0, 0)),
                pl.BlockSpec((1, HEAD_V), lambda b, h, i: (0, 0)),
                pl.BlockSpec((1, 1, LANES), lambda b, h, i: (b * N_HEADS + h, 0, 0))]
    args = [zq, lam_vecs, g_sub, knorm]
    for arr, kc, vc in segs:
        t = arr.shape[1]
        in_specs.append(pl.BlockSpec((1, t, HEAD_V), lambda b, h, i, kc=kc: (b, 0, kc + h)))
        in_specs.append(pl.BlockSpec((1, t, HEAD_V), lambda b, h, i, vc=vc: (b, 0, vc + h)))
        args += [arr, arr]
    return pl.pallas_call(
        functools.partial(_attn_kernel, nseg=len(segs), lam_init=lam_init),
        grid=(bt, N_HEADS, lq // tq),
        in_specs=in_specs,
        out_specs=pl.BlockSpec((1, tq, HEAD_V), lambda b, h, i: (b, i, h)),
        out_shape=jax.ShapeDtypeStruct((bt, lq, N_HEADS * HEAD_V), BF16),
        compiler_params=_params(("parallel", "parallel", "parallel")),
        name="diff_attention",
    )(*args)


def _merge_kernel(attn_ref, conv_ref, pool_ref, g0_ref, g1_ref, g2_ref, x_ref, gate_ref, gpost_ref,
                  wb_ref, wo_ref, o_ref):
    def gate2(ref):
        return 1.0 + jnp.tanh(ref[0].astype(F32))

    merged2 = (gate2(g0_ref) * jnp.dot(attn_ref[0], wb_ref[0], preferred_element_type=F32)
               + gate2(g1_ref) * jnp.dot(conv_ref[0], wb_ref[1], preferred_element_type=F32)
               + gate2(g2_ref) * jnp.dot(pool_ref[0], wb_ref[2], preferred_element_type=F32))
    mix = jnp.dot(merged2.astype(BF16), wo_ref[...], preferred_element_type=F32)
    o_ref[0] = x_ref[0] + gate_ref[0] * _rms(mix, gpost_ref[...])


def _merge(attn, z, x, mod, g_post, w_branch, w_out, *, layer, ctx, tm):
    bt, L, _ = x.shape

    def col(c, width=COL_BLK):
        return pl.BlockSpec((1, tm, width), lambda b, i, c=c: (b, i, c))

    in_specs = [col(0), col(CONV_OUT), col(POOL_OUT)]
    in_specs += [col(k, D_MODEL) for k in range(N_BRANCH)]
    in_specs += [col(0, D_MODEL), _mod_spec(layer, 2, ctx, 2),
                 _resident(g_post.shape), _resident(w_branch.shape), _resident(w_out.shape)]
    return pl.pallas_call(
        _merge_kernel,
        grid=(bt, L // tm),
        in_specs=in_specs,
        out_specs=col(0, D_MODEL),
        out_shape=jax.ShapeDtypeStruct((bt, L, D_MODEL), F32),
        compiler_params=_params(("parallel", "parallel")),
        name="mixer_merge",
    )(attn, *([z] * 5), x, mod, g_post, w_branch, w_out)


def _swiglu(h, wg_ref, wu_ref, wd_ref):
    f = None
    for c in range(D_FF // FF_CHUNK):
        cs = slice(c * FF_CHUNK, (c + 1) * FF_CHUNK)
        gt = jnp.dot(h, wg_ref[0, :, cs], preferred_element_type=F32)
        up = jnp.dot(h, wu_ref[0, :, cs], preferred_element_type=F32)
        a = (gt * jax.nn.sigmoid(gt) * up).astype(BF16)
        part = jnp.dot(a, wd_ref[0, cs, :], preferred_element_type=F32)
        f = part if f is None else f + part
    return f


def _ffn_kernel(x_ref, g_ref, sh_ref, sc_ref, gate_ref, gpost_ref, wg_ref, wu_ref, wd_ref, o_ref):
    h = (_rms(x_ref[0], g_ref[...]) * (1.0 + sc_ref[0]) + sh_ref[0]).astype(BF16)
    f = _swiglu(h, wg_ref, wu_ref, wd_ref)
    o_ref[0] = x_ref[0] + gate_ref[0] * _rms(f, gpost_ref[...])


def _ffn(x, g_pre, g_post, mod, wg, wu, wd, *, layer, ctx, tm):
    bt, L, _ = x.shape
    vec = pl.BlockSpec((1, D_MODEL), lambda b, i: (0, 0))
    return pl.pallas_call(
        _ffn_kernel,
        grid=(bt, L // tm),
        in_specs=[pl.BlockSpec((1, tm, D_MODEL), lambda b, i: (b, i, 0)),
                  vec,
                  _mod_spec(layer, 3, ctx, 2),
                  _mod_spec(layer, 4, ctx, 2),
                  _mod_spec(layer, 5, ctx, 2),
                  vec,
                  _resident(wg.shape), _resident(wu.shape), _resident(wd.shape)],
        out_specs=pl.BlockSpec((1, tm, D_MODEL), lambda b, i: (b, i, 0)),
        out_shape=jax.ShapeDtypeStruct((bt, L, D_MODEL), F32),
        compiler_params=_params(("parallel", "parallel")),
        name="channel_mixer",
    )(x, g_pre, mod, mod, mod, g_post, wg, wu, wd)


ROUTE_TM = 512
GROUP_TM = 512
TOP_K = 2
MOE_PARTS = 2
ROW_TILE = (8, LANES)
SC_CORES, SC_SUBCORES = 2, 16
SC_WORKERS = SC_CORES * SC_SUBCORES
SC_CHUNK = 32
RANK_LANE, PROB_LANE, EXPERT_LANE = 0, 2, 4


def _flat_mod_spec(layer, chunk, ctx, tm, seq_len, tile0=0):
    def idx(i):
        row = CTX_MOD_ROW if ctx else ((i + tile0) * tm) // seq_len
        return (layer * MOD_ROWS + row, 0, chunk)
    return pl.BlockSpec((1, 1, D_MODEL), idx)


def _lane_pick(v, lane, k):
    return jnp.sum(jnp.where(lane == k, v, 0.0), axis=-1, keepdims=True)


def _route_kernel(x_ref, g_ref, sh_ref, sc_ref, wr_ref, h_ref, info_ref, cnt_ref, run_scr):
    @pl.when(pl.program_id(0) == 0)
    def _():
        run_scr[...] = jnp.zeros_like(run_scr)

    h = _rms(x_ref[...], g_ref[...]) * (1.0 + sc_ref[0]) + sh_ref[0]
    h_ref[...] = h.reshape(h_ref.shape)
    w = wr_ref[...]
    h_hi, w_hi = h.astype(BF16), w.astype(BF16)
    h_lo = (h - h_hi.astype(F32)).astype(BF16)
    w_lo = (w - w_hi.astype(F32)).astype(BF16)
    logits = (jnp.dot(h_hi, w_hi, preferred_element_type=F32)
              + jnp.dot(h_lo, w_hi, preferred_element_type=F32)
              + jnp.dot(h_hi, w_lo, preferred_element_type=F32))
    lane = lax.broadcasted_iota(jnp.int32, logits.shape, 1)
    neg = jnp.float32(-jnp.inf)
    l1 = jnp.where(lane < N_EXPERTS, logits, neg)
    m1 = jnp.max(l1, axis=-1, keepdims=True)
    i1 = jnp.min(jnp.where(l1 == m1, lane, LANES), axis=-1, keepdims=True)
    l2 = jnp.where(lane == i1, neg, l1)
    m2 = jnp.max(l2, axis=-1, keepdims=True)
    i2 = jnp.min(jnp.where(l2 == m2, lane, LANES), axis=-1, keepdims=True)
    e2 = jnp.exp(m2 - m1)
    den = 1.0 + e2

    chosen = jnp.where(jnp.logical_or(lane == i1, lane == i2), 1.0, 0.0)
    tm = chosen.shape[0]
    earlier = (lax.broadcasted_iota(jnp.int32, (tm, tm), 1)
               < lax.broadcasted_iota(jnp.int32, (tm, tm), 0))
    before = jnp.dot(jnp.where(earlier, 1.0, 0.0).astype(BF16), chosen.astype(BF16),
                     preferred_element_type=F32)
    rank = before + run_scr[...]
    rec = [_lane_pick(rank, lane, i1), _lane_pick(rank, lane, i2), 1.0 / den, e2 / den,
           i1.astype(F32), i2.astype(F32)]
    info = jnp.zeros_like(logits)
    for k, v in enumerate(rec):
        info = jnp.where(lane == k, v, info)
    info_ref[...] = info
    run_scr[...] += jnp.sum(chosen, axis=0, keepdims=True)
    cnt_ref[...] = run_scr[...]


def _route(x2, g, mod, wr_pad, *, layer, ctx, seq_len, tile0, n):
    tm = ROUTE_TM
    vec = pl.BlockSpec((1, D_MODEL), lambda i: (0, 0))
    return pl.pallas_call(
        _route_kernel,
        grid=(n // tm,),
        in_specs=[pl.BlockSpec((tm, D_MODEL), lambda i: (i + tile0, 0)),
                  vec,
                  _flat_mod_spec(layer, 3, ctx, tm, seq_len, tile0),
                  _flat_mod_spec(layer, 4, ctx, tm, seq_len, tile0),
                  pl.BlockSpec((D_MODEL, LANES), lambda i: (0, 0))],
        out_specs=[pl.BlockSpec((tm,) + ROW_TILE, lambda i: (i, 0, 0)),
                   pl.BlockSpec((tm, LANES), lambda i: (i, 0)),
                   pl.BlockSpec((1, LANES), lambda i: (0, 0))],
        out_shape=[jax.ShapeDtypeStruct((n,) + ROW_TILE, F32),
                   jax.ShapeDtypeStruct((n, LANES), F32),
                   jax.ShapeDtypeStruct((1, LANES), F32)],
        scratch_shapes=[pltpu.VMEM((1, LANES), F32)],
        compiler_params=_params(("arbitrary",)),
        name="route",
    )(x2, g, mod, mod, wr_pad)


def _sc_gather(table, idx):
    n_out = idx.shape[0]
    per_worker = n_out // SC_WORKERS
    n_chunks = per_worker // SC_CHUNK
    mesh = plsc.VectorSubcoreMesh(core_axis_name="c", subcore_axis_name="s")

    @functools.partial(
        pl.kernel, mesh=mesh,
        out_type=jax.ShapeDtypeStruct((n_out,) + table.shape[1:], table.dtype),
        scratch_types=[pltpu.VMEM((per_worker,), jnp.int32),
                       pltpu.VMEM((SC_CHUNK,) + table.shape[1:], table.dtype),
                       pltpu.SemaphoreType.DMA],
        name="sc_row_gather")
    def gather(table_hbm, idx_hbm, out_hbm, idx_v, rows_v, sem):
        wid = lax.axis_index("s") * SC_CORES + lax.axis_index("c")
        base = wid * per_worker
        pltpu.sync_copy(idx_hbm.at[pl.ds(base, per_worker)], idx_v)

        @pl.loop(0, n_chunks)
        def _(c):
            off = c * SC_CHUNK
            pltpu.async_copy(table_hbm.at[idx_v.at[pl.ds(off, SC_CHUNK)]], rows_v, sem).wait()
            pltpu.sync_copy(rows_v, out_hbm.at[pl.ds(base + off, SC_CHUNK)])

    return gather(table, idx)


def _sc_scatter(rows, dest):
    n = rows.shape[0]
    n_out = dest.size
    per_worker = n_out // SC_WORKERS
    n_chunks = per_worker // SC_CHUNK
    mesh = plsc.VectorSubcoreMesh(core_axis_name="c", subcore_axis_name="s")

    @functools.partial(
        pl.kernel, mesh=mesh,
        out_type=jax.ShapeDtypeStruct((n_out,) + rows.shape[1:], rows.dtype),
        scratch_types=[pltpu.VMEM((n_chunks, SC_CHUNK), jnp.int32),
                       pltpu.VMEM((SC_CHUNK,) + rows.shape[1:], rows.dtype),
                       pltpu.SemaphoreType.DMA],
        name="sc_row_scatter")
    def scatter(rows_hbm, dest_hbm, out_hbm, dest_v, rows_v, sem):
        wid = lax.axis_index("s") * SC_CORES + lax.axis_index("c")
        src_base = lax.rem(wid * per_worker, n)
        pltpu.sync_copy(dest_hbm.at[pl.ds(wid * n_chunks, n_chunks)], dest_v)

        @pl.loop(0, n_chunks)
        def _(c):
            pltpu.sync_copy(rows_hbm.at[pl.ds(src_base + c * SC_CHUNK, SC_CHUNK)], rows_v)
            pltpu.async_copy(rows_v, out_hbm.at[dest_v.at[c]], sem).wait()

    return scatter(rows, dest)


def _group_kernel(vblk_ref, vexp_ref, vlo_ref, vhi_ref, vfirst_ref, vvalid_ref,
                  h_ref, wg_ref, wu_ref, wd_ref, y_ref):
    del vblk_ref, vexp_ref
    v = pl.program_id(0)

    @pl.when(vvalid_ref[v] == 1)
    def _():
        tg = h_ref.shape[0]
        h = h_ref[...].reshape(tg, D_MODEL).astype(BF16)
        f = _swiglu(h, wg_ref, wu_ref, wd_ref).reshape(y_ref.shape)

        @pl.when(vfirst_ref[v] == 1)
        def _():
            y_ref[...] = f

        @pl.when(vfirst_ref[v] == 0)
        def _():
            row = lax.broadcasted_iota(jnp.int32, (tg, 1, 1), 0)
            mine = jnp.logical_and(row >= vlo_ref[v], row < vhi_ref[v])
            y_ref[...] = jnp.where(mine, f, y_ref[...])


def _grouped_experts(hs, visits, wg, wu, wd):
    n_rows = hs.shape[0]
    tg = GROUP_TM
    nv = visits[0].shape[0]

    def wspec(shape):
        return pl.BlockSpec((1,) + shape, lambda v, blk, exp, *_: (exp[v], 0, 0))

    row_spec = pl.BlockSpec((tg,) + ROW_TILE, lambda v, blk, *_: (blk[v], 0, 0))
    return pl.pallas_call(
        _group_kernel,
        grid_spec=pltpu.PrefetchScalarGridSpec(
            num_scalar_prefetch=len(visits),
            grid=(nv,),
            in_specs=[row_spec, wspec((D_MODEL, D_FF)), wspec((D_MODEL, D_FF)),
                      wspec((D_FF, D_MODEL))],
            out_specs=row_spec),
        out_shape=jax.ShapeDtypeStruct((n_rows,) + ROW_TILE, F32),
        compiler_params=_params(("arbitrary",)),
        name="grouped_experts",
    )(*visits, hs, wg, wu, wd)


def _visit_tables(counts, n_rows):
    tg = GROUP_TM
    nv = n_rows // tg + N_EXPERTS - 1
    ends = jnp.cumsum(counts)
    starts = ends - counts
    first_tile = starts // tg
    last_tile = jnp.maximum(ends - 1, 0) // tg
    nvis = jnp.where(counts > 0, last_tile - first_tile + 1, 0)
    vend = jnp.cumsum(nvis)
    total = vend[-1]
    v = jnp.minimum(jnp.arange(nv, dtype=jnp.int32), total - 1)
    mine = jnp.logical_and(v[:, None] >= (vend - nvis)[None, :], v[:, None] < vend[None, :])

    def pick(per_expert):
        return jnp.sum(jnp.where(mine, per_expert[None, :], 0), axis=1).astype(jnp.int32)

    exp = pick(jnp.arange(N_EXPERTS, dtype=jnp.int32))
    tile = v + pick(first_tile - (vend - nvis))
    lo = jnp.maximum(pick(starts), tile * tg) - tile * tg
    hi = jnp.minimum(pick(ends), (tile + 1) * tg) - tile * tg
    first = jnp.concatenate([jnp.ones((1,), jnp.int32), (tile[1:] != tile[:-1]).astype(jnp.int32)])
    valid = (jnp.arange(nv) < total).astype(jnp.int32)
    return tuple(a.astype(jnp.int32) for a in (tile, exp, lo, hi, first, valid)), starts


def _combine_kernel(x_ref, y1_ref, y2_ref, info_ref, gate_ref, gpost_ref, o_ref):
    info = info_ref[...]
    lane = lax.broadcasted_iota(jnp.int32, info.shape, 1)
    f = (_lane_pick(info, lane, PROB_LANE) * y1_ref[...].reshape(x_ref.shape)
         + _lane_pick(info, lane, PROB_LANE + 1) * y2_ref[...].reshape(x_ref.shape))
    o_ref[...] = x_ref[...] + gate_ref[0] * _rms(f, gpost_ref[...])


def _combine_into_kernel(x_ref, y1_ref, y2_ref, info_ref, gate_ref, gpost_ref, prev_ref, o_ref):
    del prev_ref
    _combine_kernel(x_ref, y1_ref, y2_ref, info_ref, gate_ref, gpost_ref, o_ref)


def _combine(x2, y12, info, mod, g_post, prev, *, layer, ctx, seq_len, tile0):
    n = x2.shape[0]
    tm = ROUTE_TM
    nt = info.shape[0] // tm
    row = pl.BlockSpec((tm, D_MODEL), lambda i: (i + tile0, 0))
    in_specs = [row, pl.BlockSpec((tm,) + ROW_TILE, lambda i: (i, 0, 0)),
                pl.BlockSpec((tm,) + ROW_TILE, lambda i: (nt + i, 0, 0)),
                pl.BlockSpec((tm, LANES), lambda i: (i, 0)),
                _flat_mod_spec(layer, 5, ctx, tm, seq_len, tile0),
                pl.BlockSpec((1, D_MODEL), lambda i: (0, 0))]
    args = [x2, y12, y12, info, mod, g_post]
    if prev is not None:
        in_specs.append(pl.BlockSpec(memory_space=pl.ANY))
        args.append(prev)
    return pl.pallas_call(
        _combine_kernel if prev is None else _combine_into_kernel,
        grid=(nt,),
        in_specs=in_specs,
        out_specs=row,
        out_shape=jax.ShapeDtypeStruct((n, D_MODEL), F32),
        input_output_aliases={} if prev is None else {len(args) - 1: 0},
        compiler_params=_params(("parallel",)),
        name="combine",
    )(*args)


def _moe(x, g_pre, g_post, mod, wr_pad, wg, wu, wd, *, layer, ctx):
    bt, seq_len, _ = x.shape
    n = bt * seq_len
    x2 = x.reshape(n, D_MODEL)
    parts = MOE_PARTS if n % (MOE_PARTS * SC_WORKERS * SC_CHUNK) == 0 else 1
    n_part = n // parts
    tiles_part = n_part // ROUTE_TM

    routed = [_route(x2, g_pre, mod, wr_pad, layer=layer, ctx=ctx, seq_len=seq_len,
                     tile0=p * tiles_part, n=n_part) for p in range(parts)]
    sorted_rows = []
    for h, info, cnt in routed:
        counts = cnt[0, :N_EXPERTS].astype(jnp.int32)
        visits, starts = _visit_tables(counts, TOP_K * n_part)
        expert = info[:, EXPERT_LANE:EXPERT_LANE + TOP_K].astype(jnp.int32)
        rank = info[:, RANK_LANE:RANK_LANE + TOP_K].astype(jnp.int32)
        pos = (starts[expert] + rank).T.reshape(-1)
        sorted_rows.append((_sc_scatter(h, pos.reshape(-1, SC_CHUNK)), visits, pos))
    gathered = [_sc_gather(_grouped_experts(hs, visits, wg, wu, wd), pos)
                for hs, visits, pos in sorted_rows]
    out = None
    for p, (y12, (_, info, _)) in enumerate(zip(gathered, routed)):
        out = _combine(x2, y12, info, mod, g_post, out, layer=layer, ctx=ctx, seq_len=seq_len,
                       tile0=p * tiles_part)
    return out.reshape(bt, seq_len, D_MODEL)


def _rope_tables(seq_len):
    rows = jnp.repeat(jnp.arange(seq_len // GRID_W), GRID_W).astype(F32)
    cols = jnp.tile(jnp.arange(GRID_W), seq_len // GRID_W).astype(F32)
    inv = ROPE_BASE ** (-jnp.arange(ROPE_FREQS, dtype=F32) / ROPE_FREQS)
    ang = jnp.concatenate([rows[:, None] * inv, rows[:, None] * inv,
                           cols[:, None] * inv, cols[:, None] * inv], axis=1)
    ang = jnp.tile(ang, (1, LANES // HEAD_QK))
    cos, sin = jnp.cos(ang), jnp.sin(ang)
    low = (jnp.arange(LANES) % (2 * ROPE_FREQS)) < ROPE_FREQS
    return cos, jnp.where(low, -sin, 0.0), jnp.where(low, 0.0, sin)


def kernel(x, c, ctx, c_ctx, w_mod, b_mod, g_pre_mix, g_post_mix, g_pre_ffn, g_post_ffn,
           w_in, lambda_q1, lambda_k1, lambda_q2, lambda_k2, g_subln, conv_w, pool_w,
           pool_scale, w_branch, w_out, ffn_w_gate, ffn_w_up, ffn_w_down, router_w,
           moe_w_gate, moe_w_up, moe_w_down):
    depth = w_in.shape[0]
    bsz, seq, _ = x.shape
    ctx_len = ctx.shape[1]

    cvec = jnp.zeros((MOD_ROWS, D_MODEL), F32).at[:bsz].set(c).at[CTX_MOD_ROW].set(c_ctx)
    mod = _modulation(cvec, w_mod, b_mod).reshape(depth * MOD_ROWS, 1, 6 * D_MODEL)
    tables = _rope_tables(seq)
    no_tables = tuple(t[:ctx_len] for t in tables)

    y = ctx
    for i in range(depth):
        last = i == depth - 1
        lam_init = 0.8 - 0.6 * math.exp(-0.3 * i)
        split = w_in.shape[2] - GATES_W
        w_in_i = jnp.concatenate([0.5 * w_in[i, :, split:], w_in[i, :, :split]], axis=1).astype(BF16)
        lam_vecs = jnp.stack([lambda_q1[i], lambda_k1[i], lambda_q2[i], lambda_k2[i]]).astype(F32)
        g_sub = g_subln[i].reshape(1, HEAD_V)
        g_pm, g_qm = g_pre_mix[i].reshape(1, D_MODEL), g_post_mix[i].reshape(1, D_MODEL)
        g_pf, g_qf = g_pre_ffn[i].reshape(1, D_MODEL), g_post_ffn[i].reshape(1, D_MODEL)
        mix_w = (conv_w[i], pool_w[i].astype(BF16), pool_scale[i].reshape(1, BRANCH_W))
        out_w = (w_branch[i].astype(BF16), (0.5 * w_out[i]).astype(BF16))

        z, kn = _inproj(x, g_pm, mod, w_in_i, tables, mix_w, layer=i, ctx=False, rope=True,
                        col_start=0, tm=512)
        kv_full = (K_COL * HEAD_BLKS, V_COL * HEAD_BLKS)
        if last:
            w_kv = w_in_i[:, K_COL * COL_BLK:(V_COL + 1) * COL_BLK]
            zc, knc = _inproj(y, g_pm, mod, w_kv, no_tables, layer=i, ctx=True, rope=False,
                              col_start=K_COL, tm=ctx_len)
            segs = [(z,) + kv_full, (zc, 0, HEAD_BLKS)]
        else:
            zc, knc = _inproj(y, g_pm, mod, w_in_i, no_tables, mix_w, layer=i, ctx=True, rope=False,
                              col_start=0, tm=ctx_len)
            segs = [(z,) + kv_full, (zc,) + kv_full]
            attn_c = _attention(zc, [(zc,) + kv_full], [knc], lam_vecs, g_sub, lam_init=lam_init,
                                tq=ctx_len)
            y = _merge(attn_c, zc, y, mod, g_qm, *out_w, layer=i, ctx=True, tm=ctx_len)
        attn_l = _attention(z, segs, [kn, knc], lam_vecs, g_sub, lam_init=lam_init, tq=1024)
        x = _merge(attn_l, z, x, mod, g_qm, *out_w, layer=i, ctx=False, tm=1024)

        j = i // 2
        streams = [(x, False, 1024)] + ([] if last else [(y, True, ctx_len)])
        if i % 2 == 0:
            ffn_w = (ffn_w_gate[j:j + 1].astype(BF16), ffn_w_up[j:j + 1].astype(BF16),
                     ffn_w_down[j:j + 1].astype(BF16))
            outs = [_ffn(t, g_pf, g_qf, mod, *ffn_w, layer=i, ctx=is_ctx, tm=tm)
                    for t, is_ctx, tm in streams]
        else:
            ffn_w = (moe_w_gate[j].astype(BF16), moe_w_up[j].astype(BF16), moe_w_down[j].astype(BF16))
            wr_pad = jnp.zeros((D_MODEL, LANES), F32).at[:, :N_EXPERTS].set(router_w[j])
            outs = [_moe(t, g_pf, g_qf, mod, wr_pad, *ffn_w, layer=i, ctx=is_ctx)
                    for t, is_ctx, _ in streams]
        x = outs[0]
        if not last:
            y = outs[1]
    return x
```

```python
import functools
import itertools
import math

import jax
import jax.numpy as jnp
from jax import lax
from jax.experimental import pallas as pl
from jax.experimental.pallas import tpu as pltpu
from jax.experimental.pallas import tpu_sc as plsc

F32 = jnp.float32
BF16 = jnp.bfloat16

D_MODEL = 1024
GRID_W = 64
N_HEADS = 4
HEAD_QK = 64
HEAD_V = 128
ROPE_BASE = 10000.0
ROPE_FREQS = HEAD_QK // 4
POOL_WINDOWS = (2, 4, 8, 16)
POOL_GW = 128
N_BRANCH = 3
BRANCH_W = 512
GATES_W = N_BRANCH * D_MODEL
IN_W = GATES_W + 7 * BRANCH_W
D_FF = 2816
N_EXPERTS = 8
EPS = 1e-6

LANES = 128
COL_BLK = 512
Q_COL, K_COL, V_COL, CB_COL, CC_COL, CX_COL, PIN_COL = range(6, 13)
CONV_OUT, POOL_OUT = CB_COL, CB_COL + 1
HEAD_BLKS = COL_BLK // 128
MOD_ROWS = 16
CTX_MOD_ROW = 8
HALO = 8
FF_CHUNK = 256
KEY_CHUNK = 1024
LOG2_E = 1.4426950408889634
MIN_ROW_SUM = 2.0 ** -88
VMEM_LIMIT = 56 * 1024 * 1024
INPROJ_TM = 512
ATTN_TQ = 1024
MERGE_TM = 1024
FFN_TM = 1024


def _params(sem, vmem=VMEM_LIMIT):
    return pltpu.CompilerParams(dimension_semantics=sem, vmem_limit_bytes=vmem)


def _rms(t, g):
    return t * lax.rsqrt(jnp.mean(t * t, axis=-1, keepdims=True) + EPS) * g


def _mod_kernel(c_ref, w_ref, b_ref, o_ref):
    c = c_ref[...]
    s = c * jax.nn.sigmoid(c)
    o_ref[0] = jnp.dot(s, w_ref[0], preferred_element_type=F32,
                       precision=lax.Precision.HIGHEST) + b_ref[0]


def _modulation(cvec, w_mod, b_mod):
    depth = w_mod.shape[0]
    wcols = w_mod.shape[2]
    tn = 1536
    return pl.pallas_call(
        _mod_kernel,
        grid=(depth, wcols // tn),
        in_specs=[pl.BlockSpec((MOD_ROWS, D_MODEL), lambda l, j: (0, 0)),
                  pl.BlockSpec((1, D_MODEL, tn), lambda l, j: (l, 0, j)),
                  pl.BlockSpec((1, 1, tn), lambda l, j: (l, 0, j))],
        out_specs=pl.BlockSpec((1, MOD_ROWS, tn), lambda l, j: (l, 0, j)),
        out_shape=jax.ShapeDtypeStruct((depth, MOD_ROWS, wcols), F32),
        compiler_params=_params(("parallel", "parallel")),
        name="modulation",
    )(cvec, w_mod, b_mod.reshape(depth, 1, wcols))


def _mod_spec(layer, chunk, ctx):
    def idx(*g):
        row = CTX_MOD_ROW if ctx else g[0]
        return (layer * MOD_ROWS + row, 0, chunk)
    return pl.BlockSpec((1, 1, D_MODEL), idx)


def _inproj_kernel(*refs, rope, col_start, mix, seq_len):
    if mix:
        (x_ref, xp_ref, xn_ref, g_ref, sh_ref, sc_ref, w_ref, cos_ref, sa_ref, sb_ref,
         cw_ref, pw_ref, ps_ref, o_ref, kn_ref, u_scr, p_scr) = refs
        x = jnp.concatenate([x_ref[0], xp_ref[0], xn_ref[0]], axis=0)
    else:
        x_ref, g_ref, sh_ref, sc_ref, w_ref, cos_ref, sa_ref, sb_ref, o_ref, kn_ref = refs
        x = x_ref[0]
    tm = x_ref.shape[1]
    h = (_rms(x, g_ref[...]) * (1.0 + sc_ref[0]) + sh_ref[0]).astype(BF16)
    def mix_branches(cb, cc, cx, pin):
        i = pl.program_id(1)
        first = i == 0
        last = i == pl.num_programs(1) - 1

        def fill(scr, val):
            scr[0:HALO] = jnp.where(first, 0.0, val[tm:tm + HALO])
            scr[HALO:HALO + tm] = val[:tm]
            scr[HALO + tm:2 * HALO + tm] = jnp.where(last, 0.0, val[tm + HALO:])

        fill(u_scr, cc * cx)
        fill(p_scr, pin)
        yield
        cw = cw_ref[...]
        for cs in (slice(0, BRANCH_W // 2), slice(BRANCH_W // 2, BRANCH_W)):
            conv = (cw[0:1, cs] * u_scr[HALO - 1:HALO - 1 + tm, cs] + cw[1:2, cs] * u_scr[HALO:HALO + tm, cs]
                    + cw[2:3, cs] * u_scr[HALO + 1:HALO + 1 + tm, cs])
            o_ref[0, :, CONV_OUT * COL_BLK + cs.start:CONV_OUT * COL_BLK + cs.stop] = (
                cb[:, cs] * conv).astype(BF16)
            yield

        t = i * tm + lax.broadcasted_iota(jnp.int32, (tm, 1), 0)
        for g, w in enumerate(POOL_WINDOWS):
            cs = slice(g * POOL_GW, (g + 1) * POOL_GW)
            acc = p_scr[HALO - w // 2:HALO - w // 2 + tm, cs]
            for d in range(-w // 2 + 1, w // 2):
                acc = acc + p_scr[HALO + d:HALO + d + tm, cs]
            lo = jnp.maximum(t - w // 2, 0)
            hi = jnp.minimum(t - w // 2 + w, seq_len)
            p = acc / (hi - lo).astype(F32) - p_scr[HALO:HALO + tm, cs]
            y = jnp.dot(p.astype(BF16), pw_ref[g], preferred_element_type=F32) * ps_ref[:, cs]
            o_ref[0, :, POOL_OUT * COL_BLK + g * POOL_GW:POOL_OUT * COL_BLK + (g + 1) * POOL_GW] = (
                y.astype(BF16))
            yield

    kept = {}
    pieces = iter(())
    n_blk = w_ref.shape[1] // COL_BLK

    def key_norms(heads):
        lane = lax.broadcasted_iota(jnp.int32, (tm, LANES), 1)
        lane1 = lax.broadcasted_iota(jnp.int32, (1, LANES), 1)
        rec = jnp.zeros((1, LANES), F32)
        for c, t in enumerate(heads):
            tf = t.astype(F32)
            for m, n in enumerate(_map_sums(tf * tf, lane)):
                rec = jnp.where(lane1 == 2 * c + m, jnp.max(n, axis=0, keepdims=True), rec)
            yield
        kn_ref[0, 0] = rec
        yield

    def priority(j):
        col = j + col_start
        return 0 if mix and col >= CB_COL else 1 if col == K_COL else 2

    for j in sorted(range(n_blk), key=priority):
        col = j + col_start
        w_blk = w_ref[:, j * COL_BLK:(j + 1) * COL_BLK]
        if mix and col >= CB_COL:
            kept[col] = jnp.dot(h[:tm] if col == CB_COL else h, w_blk, preferred_element_type=F32)
            if len(kept) == 4:
                pieces = mix_branches(*[kept[c] for c in (CB_COL, CC_COL, CX_COL, PIN_COL)])
            continue
        z = jnp.dot(h[:tm], w_blk, preferred_element_type=F32)
        next(pieces, None)
        heads = [z[:, c * LANES:(c + 1) * LANES] for c in range(COL_BLK // LANES)]
        if rope and col in (Q_COL, K_COL):
            cos, sa, sb = cos_ref[...], sa_ref[...], sb_ref[...]
            heads = [t * cos + pltpu.roll(t, LANES - ROPE_FREQS, 1) * sa
                     + pltpu.roll(t, ROPE_FREQS, 1) * sb for t in heads]
        heads = [t.astype(BF16) for t in heads]
        for c, t in enumerate(heads):
            o_ref[0, :, j * COL_BLK + c * LANES:j * COL_BLK + (c + 1) * LANES] = t
        if col == K_COL:
            pieces = itertools.chain(pieces, key_norms(heads))
    for _ in pieces:
        pass


def _resident(shape):
    return pl.BlockSpec(shape, lambda *_: (0,) * len(shape), pipeline_mode=pl.Buffered(1))


def _inproj(x, g, mod, w, tables, mix_w=None, *, layer, ctx, rope, col_start, tm):
    bt, L, _ = x.shape
    cos, sa, sb = tables
    mix = mix_w is not None
    out_w = (POOL_OUT + 1) * COL_BLK if mix else w.shape[1]
    tab_spec = pl.BlockSpec((tm, LANES), lambda b, i: (i, 0))
    hb, nhb = tm // HALO, L // HALO
    in_specs = [pl.BlockSpec((1, tm, D_MODEL), lambda b, i: (b, i, 0))]
    args = [x]
    if mix:
        in_specs += [pl.BlockSpec((1, HALO, D_MODEL), lambda b, i: (b, jnp.maximum(i * hb - 1, 0), 0)),
                     pl.BlockSpec((1, HALO, D_MODEL),
                                  lambda b, i: (b, jnp.minimum((i + 1) * hb, nhb - 1), 0))]
        args += [x, x]
    in_specs += [_resident((1, D_MODEL)), _mod_spec(layer, 0, ctx), _mod_spec(layer, 1, ctx),
                 _resident(w.shape), tab_spec, tab_spec, tab_spec]
    args += [g, mod, mod, w, cos, sa, sb]
    scratch = []
    if mix:
        in_specs += [_resident(a.shape) for a in mix_w]
        args += list(mix_w)
        scratch = [pltpu.VMEM((tm + 2 * HALO, BRANCH_W), F32)] * 2
    return pl.pallas_call(
        functools.partial(_inproj_kernel, rope=rope, col_start=col_start, mix=mix, seq_len=L),
        grid=(bt, L // tm),
        in_specs=in_specs,
        out_specs=[pl.BlockSpec((1, tm, out_w), lambda b, i: (b, i, 0)),
                   pl.BlockSpec((1, 1, 1, LANES), lambda b, i: (b, i, 0, 0))],
        out_shape=[jax.ShapeDtypeStruct((bt, L, out_w), BF16),
                   jax.ShapeDtypeStruct((bt, L // tm, 1, LANES), F32)],
        scratch_shapes=scratch,
        compiler_params=_params(("parallel", "parallel")),
        name="inproj",
    )(*args)


def _map_sums(sq, lane):
    return [jnp.sum(jnp.where(lane < HEAD_QK, sq, 0.0), axis=-1, keepdims=True),
            jnp.sum(jnp.where(lane >= HEAD_QK, sq, 0.0), axis=-1, keepdims=True)]


def _attn_kernel(*refs, nseg, lam_init):
    q_ref, lam_ref, gs_ref, knorm_ref = refs[:4]
    kv_refs = refs[4:4 + 2 * nseg]
    o_ref = refs[4 + 2 * nseg]

    lv = lam_ref[...]
    lam = (jnp.exp(jnp.sum(lv[0:1] * lv[1:2], axis=-1, keepdims=True))
           - jnp.exp(jnp.sum(lv[2:3] * lv[3:4], axis=-1, keepdims=True)) + lam_init)

    q = q_ref[0]
    tq = q.shape[0]
    lane = lax.broadcasted_iota(jnp.int32, q.shape, 1)
    qs = (q.astype(F32) * (HEAD_QK ** -0.5 * LOG2_E)).astype(BF16)
    qmap = [jnp.where(lane < HEAD_QK, qs, jnp.zeros_like(qs)),
            jnp.where(lane >= HEAD_QK, qs, jnp.zeros_like(qs))]
    qf = qs.astype(F32)
    knorm = knorm_ref[0]
    bound =[jnp.sqrt(qn * knorm[:, j:j + 1]) for j, qn in enumerate(_map_sums(qf * qf, lane))]
    dn = (((1,), (1,)), ((), ()))

    def chunks():
        for s in range(nseg):
            k_ref, v_ref = kv_refs[2 * s], kv_refs[2 * s + 1]
            for c0 in range(0, k_ref.shape[1], KEY_CHUNK):
                ck = min(KEY_CHUNK, k_ref.shape[1] - c0)
                v = v_ref[0, c0:c0 + ck, :]
                yield k_ref[0, c0:c0 + ck, :], jnp.concatenate([v, jnp.ones_like(v)], axis=1)

    def finish(acc):
        o = (acc[0][:, :HEAD_V] / acc[0][:, HEAD_V:]
             - acc[1][:, :HEAD_V] * (lam / acc[1][:, HEAD_V:]))
        o_ref[0] = (_rms(o, gs_ref[...]) * (1.0 - lam_init)).astype(BF16)

    acc = [jnp.zeros((tq, 2 * HEAD_V), F32) for _ in range(2)]
    for k, v_aug in chunks():
        for j in range(2):
            sc = lax.dot_general(qmap[j], k, dn, preferred_element_type=F32)
            p = jnp.exp2(sc - bound[j]).astype(BF16)
            acc[j] = acc[j] + jnp.dot(p, v_aug, preferred_element_type=F32)
    finish(acc)

    smallest = jnp.min(jnp.minimum(acc[0][:, HEAD_V:], acc[1][:, HEAD_V:]))

    @pl.when(jnp.logical_not(smallest >= MIN_ROW_SUM))
    def _():
        m = [jnp.full((tq, 1), -jnp.inf, F32) for _ in range(2)]
        acc = [jnp.zeros((tq, 2 * HEAD_V), F32) for _ in range(2)]
        for k, v_aug in chunks():
            for j in range(2):
                sc = lax.dot_general(qmap[j], k, dn, preferred_element_type=F32)
                m_new = jnp.maximum(m[j], jnp.max(sc, axis=-1, keepdims=True))
                p = jnp.exp2(sc - m_new).astype(BF16)
                acc[j] = acc[j] * jnp.exp2(m[j] - m_new) + jnp.dot(p, v_aug,
                                                                  preferred_element_type=F32)
                m[j] = m_new
        finish(acc)


def _attention(zq, segs, key_norms, lam_vecs, g_sub, *, lam_init, tq):
    bt, lq, _ = zq.shape
    q0 = Q_COL * HEAD_BLKS
    best = functools.reduce(jnp.maximum, [jnp.max(kn, axis=(1, 2)) for kn in key_norms])
    knorm = jnp.pad(best[:, :2 * N_HEADS].reshape(bt * N_HEADS, 1, 2), ((0, 0), (0, 0), (0, LANES - 2)))
    in_specs = [pl.BlockSpec((1, tq, HEAD_V), lambda b, h, i: (b, i, q0 + h)),
                pl.BlockSpec((4, HEAD_QK), lambda b, h, i: (0, 0)),
                pl.BlockSpec((1, HEAD_V), lambda b, h, i: (0, 0)),
                pl.BlockSpec((1, 1, LANES), lambda b, h, i: (b * N_HEADS + h, 0, 0))]
    args = [zq, lam_vecs, g_sub, knorm]
    for arr, kc, vc in segs:
        t = arr.shape[1]
        in_specs.append(pl.BlockSpec((1, t, HEAD_V), lambda b, h, i, kc=kc: (b, 0, kc + h)))
        in_specs.append(pl.BlockSpec((1, t, HEAD_V), lambda b, h, i, vc=vc: (b, 0, vc + h)))
        args += [arr, arr]
    return pl.pallas_call(
        functools.partial(_attn_kernel, nseg=len(segs), lam_init=lam_init),
        grid=(bt, N_HEADS, lq // tq),
        in_specs=in_specs,
        out_specs=pl.BlockSpec((1, tq, HEAD_V), lambda b, h, i: (b, i, h)),
        out_shape=jax.ShapeDtypeStruct((bt, lq, N_HEADS * HEAD_V), BF16),
        compiler_params=_params(("parallel", "parallel", "parallel")),
        name="diff_attention",
    )(*args)


def _merge_kernel(attn_ref, conv_ref, pool_ref, g0_ref, g1_ref, g2_ref, x_ref, gate_ref, gpost_ref,
                  wb_ref, wo_ref, o_ref):
    def gate2(ref):
        return 1.0 + jnp.tanh(ref[0].astype(F32))

    merged2 = (gate2(g0_ref) * jnp.dot(attn_ref[0], wb_ref[0], preferred_element_type=F32)
               + gate2(g1_ref) * jnp.dot(conv_ref[0], wb_ref[1], preferred_element_type=F32)
               + gate2(g2_ref) * jnp.dot(pool_ref[0], wb_ref[2], preferred_element_type=F32))
    mix = jnp.dot(merged2.astype(BF16), wo_ref[...], preferred_element_type=F32)
    o_ref[0] = x_ref[0] + gate_ref[0] * _rms(mix, gpost_ref[...])


def _merge(attn, z, x, mod, g_post, w_branch, w_out, *, layer, ctx, tm):
    bt, L, _ = x.shape

    def col(c, width=COL_BLK):
        return pl.BlockSpec((1, tm, width), lambda b, i, c=c: (b, i, c))

    in_specs = [col(0), col(CONV_OUT), col(POOL_OUT)]
    in_specs += [col(k, D_MODEL) for k in range(N_BRANCH)]
    in_specs += [col(0, D_MODEL), _mod_spec(layer, 2, ctx),
                 _resident(g_post.shape), _resident(w_branch.shape), _resident(w_out.shape)]
    return pl.pallas_call(
        _merge_kernel,
        grid=(bt, L // tm),
        in_specs=in_specs,
        out_specs=col(0, D_MODEL),
        out_shape=jax.ShapeDtypeStruct((bt, L, D_MODEL), F32),
        compiler_params=_params(("parallel", "parallel")),
        name="mixer_merge",
    )(attn, *([z] * 5), x, mod, g_post, w_branch, w_out)


def _swiglu(h, wg_ref, wu_ref, wd_ref):
    f = None
    for c in range(D_FF // FF_CHUNK):
        cs = slice(c * FF_CHUNK, (c + 1) * FF_CHUNK)
        gt = jnp.dot(h, wg_ref[0, :, cs], preferred_element_type=F32)
        up = jnp.dot(h, wu_ref[0, :, cs], preferred_element_type=F32)
        a = (gt * jax.nn.sigmoid(gt) * up).astype(BF16)
        part = jnp.dot(a, wd_ref[0, cs, :], preferred_element_type=F32)
        f = part if f is None else f + part
    return f


def _ffn_kernel(x_ref, g_ref, sh_ref, sc_ref, gate_ref, gpost_ref, wg_ref, wu_ref, wd_ref, o_ref):
    h = (_rms(x_ref[0], g_ref[...]) * (1.0 + sc_ref[0]) + sh_ref[0]).astype(BF16)
    f = _swiglu(h, wg_ref, wu_ref, wd_ref)
    o_ref[0] = x_ref[0] + gate_ref[0] * _rms(f, gpost_ref[...])


def _ffn(x, g_pre, g_post, mod, wg, wu, wd, *, layer, ctx, tm):
    bt, L, _ = x.shape
    vec = pl.BlockSpec((1, D_MODEL), lambda b, i: (0, 0))
    return pl.pallas_call(
        _ffn_kernel,
        grid=(bt, L // tm),
        in_specs=[pl.BlockSpec((1, tm, D_MODEL), lambda b, i: (b, i, 0)),
                  vec,
                  _mod_spec(layer, 3, ctx),
                  _mod_spec(layer, 4, ctx),
                  _mod_spec(layer, 5, ctx),
                  vec,
                  _resident(wg.shape), _resident(wu.shape), _resident(wd.shape)],
        out_specs=pl.BlockSpec((1, tm, D_MODEL), lambda b, i: (b, i, 0)),
        out_shape=jax.ShapeDtypeStruct((bt, L, D_MODEL), F32),
        compiler_params=_params(("parallel", "parallel")),
        name="channel_mixer",
    )(x, g_pre, mod, mod, mod, g_post, wg, wu, wd)


ROUTE_TM = 512
GROUP_TM = 512
TOP_K = 2
MOE_PARTS = 2
ROW_TILE = (8, LANES)
SC_CORES, SC_SUBCORES = 2, 16
SC_WORKERS = SC_CORES * SC_SUBCORES
SC_CHUNK = 32
RANK_LANE, PROB_LANE, EXPERT_LANE = 0, 2, 4


def _flat_mod_spec(layer, chunk, ctx, tm, seq_len, tile0=0):
    def idx(i):
        row = CTX_MOD_ROW if ctx else ((i + tile0) * tm) // seq_len
        return (layer * MOD_ROWS + row, 0, chunk)
    return pl.BlockSpec((1, 1, D_MODEL), idx)


def _lane_pick(v, lane, k):
    return jnp.sum(jnp.where(lane == k, v, 0.0), axis=-1, keepdims=True)


def _route_kernel(x_ref, g_ref, sh_ref, sc_ref, wr_ref, h_ref, info_ref, cnt_ref, run_scr):
    @pl.when(pl.program_id(0) == 0)
    def _():
        run_scr[...] = jnp.zeros_like(run_scr)

    h = _rms(x_ref[...], g_ref[...]) * (1.0 + sc_ref[0]) + sh_ref[0]
    h_ref[...] = h.reshape(h_ref.shape)
    w = wr_ref[...]
    h_hi, w_hi = h.astype(BF16), w.astype(BF16)
    h_lo = (h - h_hi.astype(F32)).astype(BF16)
    w_lo = (w - w_hi.astype(F32)).astype(BF16)
    logits = (jnp.dot(h_hi, w_hi, preferred_element_type=F32)
              + jnp.dot(h_lo, w_hi, preferred_element_type=F32)
              + jnp.dot(h_hi, w_lo, preferred_element_type=F32))
    lane = lax.broadcasted_iota(jnp.int32, logits.shape, 1)
    neg = jnp.float32(-jnp.inf)
    l1 = jnp.where(lane < N_EXPERTS, logits, neg)
    m1 = jnp.max(l1, axis=-1, keepdims=True)
    i1 = jnp.min(jnp.where(l1 == m1, lane, LANES), axis=-1, keepdims=True)
    l2 = jnp.where(lane == i1, neg, l1)
    m2 = jnp.max(l2, axis=-1, keepdims=True)
    i2 = jnp.min(jnp.where(l2 == m2, lane, LANES), axis=-1, keepdims=True)
    e2 = jnp.exp(m2 - m1)
    den = 1.0 + e2

    chosen = jnp.where(jnp.logical_or(lane == i1, lane == i2), 1.0, 0.0)
    tm = chosen.shape[0]
    earlier = (lax.broadcasted_iota(jnp.int32, (tm, tm), 1)
               < lax.broadcasted_iota(jnp.int32, (tm, tm), 0))
    before = jnp.dot(jnp.where(earlier, 1.0, 0.0).astype(BF16), chosen.astype(BF16),
                     preferred_element_type=F32)
    rank = before + run_scr[...]
    rec = [_lane_pick(rank, lane, i1), _lane_pick(rank, lane, i2), 1.0 / den, e2 / den,
           i1.astype(F32), i2.astype(F32)]
    info = jnp.zeros_like(logits)
    for k, v in enumerate(rec):
        info = jnp.where(lane == k, v, info)
    info_ref[...] = info
    run_scr[...] += jnp.sum(chosen, axis=0, keepdims=True)
    cnt_ref[...] = run_scr[...]


def _route(x2, g, mod, wr_pad, *, layer, ctx, seq_len, tile0, n):
    tm = ROUTE_TM
    vec = pl.BlockSpec((1, D_MODEL), lambda i: (0, 0))
    return pl.pallas_call(
        _route_kernel,
        grid=(n // tm,),
        in_specs=[pl.BlockSpec((tm, D_MODEL), lambda i: (i + tile0, 0)),
                  vec,
                  _flat_mod_spec(layer, 3, ctx, tm, seq_len, tile0),
                  _flat_mod_spec(layer, 4, ctx, tm, seq_len, tile0),
                  pl.BlockSpec((D_MODEL, LANES), lambda i: (0, 0))],
        out_specs=[pl.BlockSpec((tm,) + ROW_TILE, lambda i: (i, 0, 0)),
                   pl.BlockSpec((tm, LANES), lambda i: (i, 0)),
                   pl.BlockSpec((1, LANES), lambda i: (0, 0))],
        out_shape=[jax.ShapeDtypeStruct((n,) + ROW_TILE, F32),
                   jax.ShapeDtypeStruct((n, LANES), F32),
                   jax.ShapeDtypeStruct((1, LANES), F32)],
        scratch_shapes=[pltpu.VMEM((1, LANES), F32)],
        compiler_params=_params(("arbitrary",)),
        name="route",
    )(x2, g, mod, mod, wr_pad)


def _sc_gather(table, idx):
    n_out = idx.shape[0]
    per_worker = n_out // SC_WORKERS
    n_chunks = per_worker // SC_CHUNK
    mesh = plsc.VectorSubcoreMesh(core_axis_name="c", subcore_axis_name="s")

    @functools.partial(
        pl.kernel, mesh=mesh,
        out_type=jax.ShapeDtypeStruct((n_out,) + table.shape[1:], table.dtype),
        scratch_types=[pltpu.VMEM((per_worker,), jnp.int32),
                       pltpu.VMEM((SC_CHUNK,) + table.shape[1:], table.dtype),
                       pltpu.SemaphoreType.DMA],
        name="sc_row_gather")
    def gather(table_hbm, idx_hbm, out_hbm, idx_v, rows_v, sem):
        wid = lax.axis_index("s") * SC_CORES + lax.axis_index("c")
        base = wid * per_worker
        pltpu.sync_copy(idx_hbm.at[pl.ds(base, per_worker)], idx_v)

        @pl.loop(0, n_chunks)
        def _(c):
            off = c * SC_CHUNK
            pltpu.async_copy(table_hbm.at[idx_v.at[pl.ds(off, SC_CHUNK)]], rows_v, sem).wait()
            pltpu.sync_copy(rows_v, out_hbm.at[pl.ds(base + off, SC_CHUNK)])

    return gather(table, idx)


def _sc_scatter(rows, dest):
    n = rows.shape[0]
    n_out = dest.size
    per_worker = n_out // SC_WORKERS
    n_chunks = per_worker // SC_CHUNK
    mesh = plsc.VectorSubcoreMesh(core_axis_name="c", subcore_axis_name="s")

    @functools.partial(
        pl.kernel, mesh=mesh,
        out_type=jax.ShapeDtypeStruct((n_out,) + rows.shape[1:], rows.dtype),
        scratch_types=[pltpu.VMEM((n_chunks, SC_CHUNK), jnp.int32),
                       pltpu.VMEM((SC_CHUNK,) + rows.shape[1:], rows.dtype),
                       pltpu.SemaphoreType.DMA],
        name="sc_row_scatter")
    def scatter(rows_hbm, dest_hbm, out_hbm, dest_v, rows_v, sem):
        wid = lax.axis_index("s") * SC_CORES + lax.axis_index("c")
        src_base = lax.rem(wid * per_worker, n)
        pltpu.sync_copy(dest_hbm.at[pl.ds(wid * n_chunks, n_chunks)], dest_v)

        @pl.loop(0, n_chunks)
        def _(c):
            pltpu.sync_copy(rows_hbm.at[pl.ds(src_base + c * SC_CHUNK, SC_CHUNK)], rows_v)
            pltpu.async_copy(rows_v, out_hbm.at[dest_v.at[c]], sem).wait()

    return scatter(rows, dest)


def _group_kernel(vblk_ref, vexp_ref, vlo_ref, vhi_ref, vfirst_ref, vvalid_ref,
                  h_ref, wg_ref, wu_ref, wd_ref, y_ref):
    del vblk_ref, vexp_ref
    v = pl.program_id(0)

    @pl.when(vvalid_ref[v] == 1)
    def _():
        tg = h_ref.shape[0]
        h = h_ref[...].reshape(tg, D_MODEL).astype(BF16)
        f = _swiglu(h, wg_ref, wu_ref, wd_ref).reshape(y_ref.shape)

        @pl.when(vfirst_ref[v] == 1)
        def _():
            y_ref[...] = f

        @pl.when(vfirst_ref[v] == 0)
        def _():
            row = lax.broadcasted_iota(jnp.int32, (tg, 1, 1), 0)
            mine = jnp.logical_and(row >= vlo_ref[v], row < vhi_ref[v])
            y_ref[...] = jnp.where(mine, f, y_ref[...])


def _grouped_experts(hs, visits, wg, wu, wd):
    n_rows = hs.shape[0]
    tg = GROUP_TM
    nv = visits[0].shape[0]

    def wspec(shape):
        return pl.BlockSpec((1,) + shape, lambda v, blk, exp, *_: (exp[v], 0, 0))

    row_spec = pl.BlockSpec((tg,) + ROW_TILE, lambda v, blk, *_: (blk[v], 0, 0))
    return pl.pallas_call(
        _group_kernel,
        grid_spec=pltpu.PrefetchScalarGridSpec(
            num_scalar_prefetch=len(visits),
            grid=(nv,),
            in_specs=[row_spec, wspec((D_MODEL, D_FF)), wspec((D_MODEL, D_FF)),
                      wspec((D_FF, D_MODEL))],
            out_specs=row_spec),
        out_shape=jax.ShapeDtypeStruct((n_rows,) + ROW_TILE, F32),
        compiler_params=_params(("arbitrary",)),
        name="grouped_experts",
    )(*visits, hs, wg, wu, wd)


def _visit_tables(counts, n_rows):
    tg = GROUP_TM
    nv = n_rows // tg + N_EXPERTS - 1
    ends = jnp.cumsum(counts)
    starts = ends - counts
    first_tile = starts // tg
    last_tile = jnp.maximum(ends - 1, 0) // tg
    nvis = jnp.where(counts > 0, last_tile - first_tile + 1, 0)
    vend = jnp.cumsum(nvis)
    total = vend[-1]
    v = jnp.minimum(jnp.arange(nv, dtype=jnp.int32), total - 1)
    mine = jnp.logical_and(v[:, None] >= (vend - nvis)[None, :], v[:, None] < vend[None, :])

    def pick(per_expert):
        return jnp.sum(jnp.where(mine, per_expert[None, :], 0), axis=1).astype(jnp.int32)

    exp = pick(jnp.arange(N_EXPERTS, dtype=jnp.int32))
    tile = v + pick(first_tile - (vend - nvis))
    lo = jnp.maximum(pick(starts), tile * tg) - tile * tg
    hi = jnp.minimum(pick(ends), (tile + 1) * tg) - tile * tg
    first = jnp.concatenate([jnp.ones((1,), jnp.int32), (tile[1:] != tile[:-1]).astype(jnp.int32)])
    valid = (jnp.arange(nv) < total).astype(jnp.int32)
    return tuple(a.astype(jnp.int32) for a in (tile, exp, lo, hi, first, valid)), starts


def _combine_kernel(x_ref, y1_ref, y2_ref, info_ref, gate_ref, gpost_ref, o_ref):
    info = info_ref[...]
    lane = lax.broadcasted_iota(jnp.int32, info.shape, 1)
    f = (_lane_pick(info, lane, PROB_LANE) * y1_ref[...].reshape(x_ref.shape)
         + _lane_pick(info, lane, PROB_LANE + 1) * y2_ref[...].reshape(x_ref.shape))
    o_ref[...] = x_ref[...] + gate_ref[0] * _rms(f, gpost_ref[...])


def _combine_into_kernel(x_ref, y1_ref, y2_ref, info_ref, gate_ref, gpost_ref, prev_ref, o_ref):
    del prev_ref
    _combine_kernel(x_ref, y1_ref, y2_ref, info_ref, gate_ref, gpost_ref, o_ref)


def _combine(x2, y12, info, mod, g_post, prev, *, layer, ctx, seq_len, tile0):
    n = x2.shape[0]
    tm = ROUTE_TM
    nt = info.shape[0] // tm
    row = pl.BlockSpec((tm, D_MODEL), lambda i: (i + tile0, 0))
    in_specs = [row, pl.BlockSpec((tm,) + ROW_TILE, lambda i: (i, 0, 0)),
                pl.BlockSpec((tm,) + ROW_TILE, lambda i: (nt + i, 0, 0)),
                pl.BlockSpec((tm, LANES), lambda i: (i, 0)),
                _flat_mod_spec(layer, 5, ctx, tm, seq_len, tile0),
                pl.BlockSpec((1, D_MODEL), lambda i: (0, 0))]
    args = [x2, y12, y12, info, mod, g_post]
    if prev is not None:
        in_specs.append(pl.BlockSpec(memory_space=pl.ANY))
        args.append(prev)
    return pl.pallas_call(
        _combine_kernel if prev is None else _combine_into_kernel,
        grid=(nt,),
        in_specs=in_specs,
        out_specs=row,
        out_shape=jax.ShapeDtypeStruct((n, D_MODEL), F32),
        input_output_aliases={} if prev is None else {len(args) - 1: 0},
        compiler_params=_params(("parallel",)),
        name="combine",
    )(*args)


def _moe(x, g_pre, g_post, mod, wr_pad, wg, wu, wd, *, layer, ctx):
    bt, seq_len, _ = x.shape
    n = bt * seq_len
    x2 = x.reshape(n, D_MODEL)
    parts = MOE_PARTS if n % (MOE_PARTS * SC_WORKERS * SC_CHUNK) == 0 else 1
    n_part = n // parts
    tiles_part = n_part // ROUTE_TM

    routed = [_route(x2, g_pre, mod, wr_pad, layer=layer, ctx=ctx, seq_len=seq_len,
                     tile0=p * tiles_part, n=n_part) for p in range(parts)]
    sorted_rows = []
    for h, info, cnt in routed:
        counts = cnt[0, :N_EXPERTS].astype(jnp.int32)
        visits, starts = _visit_tables(counts, TOP_K * n_part)
        expert = info[:, EXPERT_LANE:EXPERT_LANE + TOP_K].astype(jnp.int32)
        rank = info[:, RANK_LANE:RANK_LANE + TOP_K].astype(jnp.int32)
        pos = (starts[expert] + rank).T.reshape(-1)
        sorted_rows.append((_sc_scatter(h, pos.reshape(-1, SC_CHUNK)), visits, pos))
    gathered = [_sc_gather(_grouped_experts(hs, visits, wg, wu, wd), pos)
                for hs, visits, pos in sorted_rows]
    out = None
    for p, (y12, (_, info, _)) in enumerate(zip(gathered, routed)):
        out = _combine(x2, y12, info, mod, g_post, out, layer=layer, ctx=ctx, seq_len=seq_len,
                       tile0=p * tiles_part)
    return out.reshape(bt, seq_len, D_MODEL)


def _rope_tables(seq_len):
    rows = jnp.repeat(jnp.arange(seq_len // GRID_W), GRID_W).astype(F32)
    cols = jnp.tile(jnp.arange(GRID_W), seq_len // GRID_W).astype(F32)
    inv = ROPE_BASE ** (-jnp.arange(ROPE_FREQS, dtype=F32) / ROPE_FREQS)
    ang = jnp.concatenate([rows[:, None] * inv, rows[:, None] * inv,
                           cols[:, None] * inv, cols[:, None] * inv], axis=1)
    ang = jnp.tile(ang, (1, LANES // HEAD_QK))
    cos, sin = jnp.cos(ang), jnp.sin(ang)
    low = (jnp.arange(LANES) % (2 * ROPE_FREQS)) < ROPE_FREQS
    return cos, jnp.where(low, -sin, 0.0), jnp.where(low, 0.0, sin)


def kernel(x, c, ctx, c_ctx, w_mod, b_mod, g_pre_mix, g_post_mix, g_pre_ffn, g_post_ffn,
           w_in, lambda_q1, lambda_k1, lambda_q2, lambda_k2, g_subln, conv_w, pool_w,
           pool_scale, w_branch, w_out, ffn_w_gate, ffn_w_up, ffn_w_down, router_w,
           moe_w_gate, moe_w_up, moe_w_down):
    depth = w_in.shape[0]
    bsz, seq, _ = x.shape
    ctx_len = ctx.shape[1]

    cvec = jnp.zeros((MOD_ROWS, D_MODEL), F32).at[:bsz].set(c).at[CTX_MOD_ROW].set(c_ctx)
    mod = _modulation(cvec, w_mod, b_mod).reshape(depth * MOD_ROWS, 1, 6 * D_MODEL)
    tables = _rope_tables(seq)
    no_tables = tuple(t[:ctx_len] for t in tables)

    y = ctx
    for i in range(depth):
        last = i == depth - 1
        lam_init = 0.8 - 0.6 * math.exp(-0.3 * i)
        split = w_in.shape[2] - GATES_W
        w_in_i = jnp.concatenate([0.5 * w_in[i, :, split:], w_in[i, :, :split]], axis=1).astype(BF16)
        lam_vecs = jnp.stack([lambda_q1[i], lambda_k1[i], lambda_q2[i], lambda_k2[i]]).astype(F32)
        g_sub = g_subln[i].reshape(1, HEAD_V)
        g_pm, g_qm = g_pre_mix[i].reshape(1, D_MODEL), g_post_mix[i].reshape(1, D_MODEL)
        g_pf, g_qf = g_pre_ffn[i].reshape(1, D_MODEL), g_post_ffn[i].reshape(1, D_MODEL)
        mix_w = (conv_w[i], pool_w[i].astype(BF16), pool_scale[i].reshape(1, BRANCH_W))
        out_w = (w_branch[i].astype(BF16), (0.5 * w_out[i]).astype(BF16))

        z, kn = _inproj(x, g_pm, mod, w_in_i, tables, mix_w, layer=i, ctx=False, rope=True,
                        col_start=0, tm=INPROJ_TM)
        kv_full = (K_COL * HEAD_BLKS, V_COL * HEAD_BLKS)
        if last:
            w_kv = w_in_i[:, K_COL * COL_BLK:(V_COL + 1) * COL_BLK]
            zc, knc = _inproj(y, g_pm, mod, w_kv, no_tables, layer=i, ctx=True, rope=False,
                              col_start=K_COL, tm=ctx_len)
            segs = [(z,) + kv_full, (zc, 0, HEAD_BLKS)]
        else:
            zc, knc = _inproj(y, g_pm, mod, w_in_i, no_tables, mix_w, layer=i, ctx=True, rope=False,
                              col_start=0, tm=ctx_len)
            segs = [(z,) + kv_full, (zc,) + kv_full]
            attn_c = _attention(zc, [(zc,) + kv_full], [knc], lam_vecs, g_sub, lam_init=lam_init,
                                tq=ctx_len)
            y = _merge(attn_c, zc, y, mod, g_qm, *out_w, layer=i, ctx=True, tm=ctx_len)
        attn_l = _attention(z, segs, [kn, knc], lam_vecs, g_sub, lam_init=lam_init, tq=ATTN_TQ)
        x = _merge(attn_l, z, x, mod, g_qm, *out_w, layer=i, ctx=False, tm=MERGE_TM)

        j = i // 2
        streams = [(x, False, FFN_TM)] + ([] if last else [(y, True, ctx_len)])
        if i % 2 == 0:
            ffn_w = (ffn_w_gate[j:j + 1].astype(BF16), ffn_w_up[j:j + 1].astype(BF16),
                     ffn_w_down[j:j + 1].astype(BF16))
            outs = [_ffn(t, g_pf, g_qf, mod, *ffn_w, layer=i, ctx=is_ctx, tm=tm)
                    for t, is_ctx, tm in streams]
        else:
            ffn_w = (moe_w_gate[j].astype(BF16), moe_w_up[j].astype(BF16), moe_w_down[j].astype(BF16))
            wr_pad = jnp.zeros((D_MODEL, LANES), F32).at[:, :N_EXPERTS].set(router_w[j])
            outs = [_moe(t, g_pf, g_qf, mod, wr_pad, *ffn_w, layer=i, ctx=is_ctx)
                    for t, is_ctx, _ in streams]
        x = outs[0]
        if not last:
            y = outs[1]
    return x
```

```python
import functools
import itertools
import math

import jax
import jax.numpy as jnp
from jax import lax
from jax.experimental import pallas as pl
from jax.experimental.pallas import tpu as pltpu
from jax.experimental.pallas import tpu_sc as plsc

F32 = jnp.float32
BF16 = jnp.bfloat16

D_MODEL = 1024
GRID_W = 64
N_HEADS = 4
HEAD_QK = 64
HEAD_V = 128
ROPE_BASE = 10000.0
ROPE_FREQS = HEAD_QK // 4
POOL_WINDOWS = (2, 4, 8, 16)
POOL_GW = 128
N_BRANCH = 3
BRANCH_W = 512
GATES_W = N_BRANCH * D_MODEL
IN_W = GATES_W + 7 * BRANCH_W
D_FF = 2816
N_EXPERTS = 8
EPS = 1e-6

LANES = 128
COL_BLK = 512
Q_COL, K_COL, V_COL, CB_COL, CC_COL, CX_COL, PIN_COL = range(6, 13)
CONV_OUT, POOL_OUT = CB_COL, CB_COL + 1
HEAD_BLKS = COL_BLK // 128
MOD_ROWS = 16
CTX_MOD_ROW = 8
HALO = 8
FF_CHUNK = 256
KEY_CHUNK = 1024
LOG2_E = 1.4426950408889634
MIN_ROW_SUM = 2.0 ** -88
VMEM_LIMIT = 56 * 1024 * 1024
INPROJ_TM = 512
ATTN_TQ = 1024
MERGE_TM = 1024
FFN_TM = 1024


def _params(sem, vmem=VMEM_LIMIT):
    return pltpu.CompilerParams(dimension_semantics=sem, vmem_limit_bytes=vmem)


def _rms(t, g):
    return t * lax.rsqrt(jnp.mean(t * t, axis=-1, keepdims=True) + EPS) * g


def _mod_kernel(c_ref, w_ref, b_ref, o_ref):
    c = c_ref[...]
    s = c * jax.nn.sigmoid(c)
    o_ref[0] = jnp.dot(s, w_ref[0], preferred_element_type=F32,
                       precision=lax.Precision.HIGHEST) + b_ref[0]


def _modulation(cvec, w_mod, b_mod):
    depth = w_mod.shape[0]
    wcols = w_mod.shape[2]
    tn = 1536
    return pl.pallas_call(
        _mod_kernel,
        grid=(depth, wcols // tn),
        in_specs=[pl.BlockSpec((MOD_ROWS, D_MODEL), lambda l, j: (0, 0)),
                  pl.BlockSpec((1, D_MODEL, tn), lambda l, j: (l, 0, j)),
                  pl.BlockSpec((1, 1, tn), lambda l, j: (l, 0, j))],
        out_specs=pl.BlockSpec((1, MOD_ROWS, tn), lambda l, j: (l, 0, j)),
        out_shape=jax.ShapeDtypeStruct((depth, MOD_ROWS, wcols), F32),
        compiler_params=_params(("parallel", "parallel")),
        name="modulation",
    )(cvec, w_mod, b_mod.reshape(depth, 1, wcols))


def _mod_spec(layer, chunk, ctx):
    def idx(*g):
        row = CTX_MOD_ROW if ctx else g[0]
        return (layer * MOD_ROWS + row, 0, chunk)
    return pl.BlockSpec((1, 1, D_MODEL), idx)


def _inproj_kernel(*refs, rope, col_start, mix, seq_len):
    if mix:
        (x_ref, xp_ref, xn_ref, g_ref, sh_ref, sc_ref, w_ref, cos_ref, sa_ref, sb_ref,
         cw_ref, pw_ref, ps_ref, o_ref, kn_ref, u_scr, p_scr) = refs
        x = jnp.concatenate([x_ref[0], xp_ref[0], xn_ref[0]], axis=0)
    else:
        x_ref, g_ref, sh_ref, sc_ref, w_ref, cos_ref, sa_ref, sb_ref, o_ref, kn_ref = refs
        x = x_ref[0]
    tm = x_ref.shape[1]
    h = (_rms(x, g_ref[...]) * (1.0 + sc_ref[0]) + sh_ref[0]).astype(BF16)
    def mix_branches(cb, cc, cx, pin):
        i = pl.program_id(1)
        first = i == 0
        last = i == pl.num_programs(1) - 1

        def fill(scr, val):
            scr[0:HALO] = jnp.where(first, 0.0, val[tm:tm + HALO])
            scr[HALO:HALO + tm] = val[:tm]
            scr[HALO + tm:2 * HALO + tm] = jnp.where(last, 0.0, val[tm + HALO:])

        fill(u_scr, cc * cx)
        fill(p_scr, pin)
        yield
        cw = cw_ref[...]
        for cs in (slice(0, BRANCH_W // 2), slice(BRANCH_W // 2, BRANCH_W)):
            conv = (cw[0:1, cs] * u_scr[HALO - 1:HALO - 1 + tm, cs] + cw[1:2, cs] * u_scr[HALO:HALO + tm, cs]
                    + cw[2:3, cs] * u_scr[HALO + 1:HALO + 1 + tm, cs])
            o_ref[0, :, CONV_OUT * COL_BLK + cs.start:CONV_OUT * COL_BLK + cs.stop] = (
                cb[:, cs] * conv).astype(BF16)
            yield

        t = i * tm + lax.broadcasted_iota(jnp.int32, (tm, 1), 0)
        for g, w in enumerate(POOL_WINDOWS):
            cs = slice(g * POOL_GW, (g + 1) * POOL_GW)
            acc = p_scr[HALO - w // 2:HALO - w // 2 + tm, cs]
            for d in range(-w // 2 + 1, w // 2):
                acc = acc + p_scr[HALO + d:HALO + d + tm, cs]
            lo = jnp.maximum(t - w // 2, 0)
            hi = jnp.minimum(t - w // 2 + w, seq_len)
            p = acc / (hi - lo).astype(F32) - p_scr[HALO:HALO + tm, cs]
            y = jnp.dot(p.astype(BF16), pw_ref[g], preferred_element_type=F32) * ps_ref[:, cs]
            o_ref[0, :, POOL_OUT * COL_BLK + g * POOL_GW:POOL_OUT * COL_BLK + (g + 1) * POOL_GW] = (
                y.astype(BF16))
            yield

    kept = {}
    pieces = iter(())
    n_blk = w_ref.shape[1] // COL_BLK

    def key_norms(heads):
        lane = lax.broadcasted_iota(jnp.int32, (tm, LANES), 1)
        lane1 = lax.broadcasted_iota(jnp.int32, (1, LANES), 1)
        rec = jnp.zeros((1, LANES), F32)
        for c, t in enumerate(heads):
            tf = t.astype(F32)
            for m, n in enumerate(_map_sums(tf * tf, lane)):
                rec = jnp.where(lane1 == 2 * c + m, jnp.max(n, axis=0, keepdims=True), rec)
            yield
        kn_ref[0, 0] = rec
        yield

    def priority(j):
        col = j + col_start
        return 0 if mix and col >= CB_COL else 1 if col == K_COL else 2

    for j in sorted(range(n_blk), key=priority):
        col = j + col_start
        w_blk = w_ref[:, j * COL_BLK:(j + 1) * COL_BLK]
        if mix and col >= CB_COL:
            kept[col] = jnp.dot(h[:tm] if col == CB_COL else h, w_blk, preferred_element_type=F32)
            if len(kept) == 4:
                pieces = mix_branches(*[kept[c] for c in (CB_COL, CC_COL, CX_COL, PIN_COL)])
            continue
        z = jnp.dot(h[:tm], w_blk, preferred_element_type=F32)
        next(pieces, None)
        heads = [z[:, c * LANES:(c + 1) * LANES] for c in range(COL_BLK // LANES)]
        if rope and col in (Q_COL, K_COL):
            cos, sa, sb = cos_ref[...], sa_ref[...], sb_ref[...]
            heads = [t * cos + pltpu.roll(t, LANES - ROPE_FREQS, 1) * sa
                     + pltpu.roll(t, ROPE_FREQS, 1) * sb for t in heads]
        heads = [t.astype(BF16) for t in heads]
        for c, t in enumerate(heads):
            o_ref[0, :, j * COL_BLK + c * LANES:j * COL_BLK + (c + 1) * LANES] = t
        if col == K_COL:
            pieces = itertools.chain(pieces, key_norms(heads))
    for _ in pieces:
        pass


def _resident(shape):
    return pl.BlockSpec(shape, lambda *_: (0,) * len(shape), pipeline_mode=pl.Buffered(1))


def _inproj(x, g, mod, w, tables, mix_w=None, *, layer, ctx, rope, col_start, tm):
    bt, L, _ = x.shape
    cos, sa, sb = tables
    mix = mix_w is not None
    out_w = (POOL_OUT + 1) * COL_BLK if mix else w.shape[1]
    tab_spec = pl.BlockSpec((tm, LANES), lambda b, i: (i, 0))
    hb, nhb = tm // HALO, L // HALO
    in_specs = [pl.BlockSpec((1, tm, D_MODEL), lambda b, i: (b, i, 0))]
    args = [x]
    if mix:
        in_specs += [pl.BlockSpec((1, HALO, D_MODEL), lambda b, i: (b, jnp.maximum(i * hb - 1, 0), 0)),
                     pl.BlockSpec((1, HALO, D_MODEL),
                                  lambda b, i: (b, jnp.minimum((i + 1) * hb, nhb - 1), 0))]
        args += [x, x]
    in_specs += [_resident((1, D_MODEL)), _mod_spec(layer, 0, ctx), _mod_spec(layer, 1, ctx),
                 _resident(w.shape), tab_spec, tab_spec, tab_spec]
    args += [g, mod, mod, w, cos, sa, sb]
    scratch = []
    if mix:
        in_specs += [_resident(a.shape) for a in mix_w]
        args += list(mix_w)
        scratch = [pltpu.VMEM((tm + 2 * HALO, BRANCH_W), F32)] * 2
    return pl.pallas_call(
        functools.partial(_inproj_kernel, rope=rope, col_start=col_start, mix=mix, seq_len=L),
        grid=(bt, L // tm),
        in_specs=in_specs,
        out_specs=[pl.BlockSpec((1, tm, out_w), lambda b, i: (b, i, 0)),
                   pl.BlockSpec((1, 1, 1, LANES), lambda b, i: (b, i, 0, 0))],
        out_shape=[jax.ShapeDtypeStruct((bt, L, out_w), BF16),
                   jax.ShapeDtypeStruct((bt, L // tm, 1, LANES), F32)],
        scratch_shapes=scratch,
        compiler_params=_params(("parallel", "parallel")),
        name="inproj",
    )(*args)


def _map_sums(sq, lane):
    return [jnp.sum(jnp.where(lane < HEAD_QK, sq, 0.0), axis=-1, keepdims=True),
            jnp.sum(jnp.where(lane >= HEAD_QK, sq, 0.0), axis=-1, keepdims=True)]


def _attn_kernel(*refs, nseg, lam_init):
    q_ref, lam_ref, gs_ref, knorm_ref = refs[:4]
    kv_refs = refs[4:4 + 2 * nseg]
    o_ref = refs[4 + 2 * nseg]

    lv = lam_ref[...]
    lam = (jnp.exp(jnp.sum(lv[0:1] * lv[1:2], axis=-1, keepdims=True))
           - jnp.exp(jnp.sum(lv[2:3] * lv[3:4], axis=-1, keepdims=True)) + lam_init)

    q = q_ref[0]
    tq = q.shape[0]
    lane = lax.broadcasted_iota(jnp.int32, q.shape, 1)
    qs = (q.astype(F32) * (HEAD_QK ** -0.5 * LOG2_E)).astype(BF16)
    qmap = [jnp.where(lane < HEAD_QK, qs, jnp.zeros_like(qs)),
            jnp.where(lane >= HEAD_QK, qs, jnp.zeros_like(qs))]
    qf = qs.astype(F32)
    knorm = knorm_ref[0]
    bound =[jnp.sqrt(qn * knorm[:, j:j + 1]) for j, qn in enumerate(_map_sums(qf * qf, lane))]
    dn = (((1,), (1,)), ((), ()))

    def chunks():
        for s in range(nseg):
            k_ref, v_ref = kv_refs[2 * s], kv_refs[2 * s + 1]
            for c0 in range(0, k_ref.shape[1], KEY_CHUNK):
                ck = min(KEY_CHUNK, k_ref.shape[1] - c0)
                v = v_ref[0, c0:c0 + ck, :]
                yield k_ref[0, c0:c0 + ck, :], jnp.concatenate([v, jnp.ones_like(v)], axis=1)

    def finish(acc):
        o = (acc[0][:, :HEAD_V] / acc[0][:, HEAD_V:]
             - acc[1][:, :HEAD_V] * (lam / acc[1][:, HEAD_V:]))
        o_ref[0] = (_rms(o, gs_ref[...]) * (1.0 - lam_init)).astype(BF16)

    acc = [jnp.zeros((tq, 2 * HEAD_V), F32) for _ in range(2)]
    for k, v_aug in chunks():
        for j in range(2):
            sc = lax.dot_general(qmap[j], k, dn, preferred_element_type=F32)
            p = jnp.exp2(sc - bound[j]).astype(BF16)
            acc[j] = acc[j] + jnp.dot(p, v_aug, preferred_element_type=F32)
    finish(acc)

    smallest = jnp.min(jnp.minimum(acc[0][:, HEAD_V:], acc[1][:, HEAD_V:]))

    @pl.when(jnp.logical_not(smallest >= MIN_ROW_SUM))
    def _():
        m = [jnp.full((tq, 1), -jnp.inf, F32) for _ in range(2)]
        acc = [jnp.zeros((tq, 2 * HEAD_V), F32) for _ in range(2)]
        for k, v_aug in chunks():
            for j in range(2):
                sc = lax.dot_general(qmap[j], k, dn, preferred_element_type=F32)
                m_new = jnp.maximum(m[j], jnp.max(sc, axis=-1, keepdims=True))
                p = jnp.exp2(sc - m_new).astype(BF16)
                acc[j] = acc[j] * jnp.exp2(m[j] - m_new) + jnp.dot(p, v_aug,
                                                                  preferred_element_type=F32)
                m[j] = m_new
        finish(acc)


def _attention(zq, segs, key_norms, lam_vecs, g_sub, *, lam_init, tq):
    bt, lq, _ = zq.shape
    q0 = Q_COL * HEAD_BLKS
    best = functools.reduce(jnp.maximum, [jnp.max(kn, axis=(1, 2)) for kn in key_norms])
    knorm = jnp.pad(best[:, :2 * N_HEADS].reshape(bt * N_HEADS, 1, 2), ((0, 0), (0, 0), (0, LANES - 2)))
    in_specs = [pl.BlockSpec((1, tq, HEAD_V), lambda b, h, i: (b, i, q0 + h)),
                pl.BlockSpec((4, HEAD_QK), lambda b, h, i: (0, 0)),
                pl.BlockSpec((1, HEAD_V), lambda b, h, i: (0, 0)),
                pl.BlockSpec((1, 1, LANES), lambda b, h, i: (b * N_HEADS + h, 0, 0))]
    args = [zq, lam_vecs, g_sub, knorm]
    for arr, kc, vc in segs:
        t = arr.shape[1]
        in_specs.append(pl.BlockSpec((1, t, HEAD_V), lambda b, h, i, kc=kc: (b, 0, kc + h)))
        in_specs.append(pl.BlockSpec((1, t, HEAD_V), lambda b, h, i, vc=vc: (b, 0, vc + h)))
        args += [arr, arr]
    return pl.pallas_call(
        functools.partial(_attn_kernel, nseg=len(segs), lam_init=lam_init),
        grid=(bt, N_HEADS, lq // tq),
        in_specs=in_specs,
        out_specs=pl.BlockSpec((1, tq, HEAD_V), lambda b, h, i: (b, i, h)),
        out_shape=jax.ShapeDtypeStruct((bt, lq, N_HEADS * HEAD_V), BF16),
        compiler_params=_params(("parallel", "parallel", "parallel")),
        name="diff_attention",
    )(*args)


def _merge_kernel(attn_ref, conv_ref, pool_ref, g0_ref, g1_ref, g2_ref, x_ref, gate_ref, gpost_ref,
                  wb_ref, wo_ref, o_ref):
    def gate2(ref):
        return 1.0 + jnp.tanh(ref[0].astype(F32))

    merged2 = (gate2(g0_ref) * jnp.dot(attn_ref[0], wb_ref[0], preferred_element_type=F32)
               + gate2(g1_ref) * jnp.dot(conv_ref[0], wb_ref[1], preferred_element_type=F32)
               + gate2(g2_ref) * jnp.dot(pool_ref[0], wb_ref[2], preferred_element_type=F32))
    mix = jnp.dot(merged2.astype(BF16), wo_ref[...], preferred_element_type=F32)
    o_ref[0] = x_ref[0] + gate_ref[0] * _rms(mix, gpost_ref[...])


def _merge(attn, z, x, mod, g_post, w_branch, w_out, *, layer, ctx, tm):
    bt, L, _ = x.shape

    def col(c, width=COL_BLK):
        return pl.BlockSpec((1, tm, width), lambda b, i, c=c: (b, i, c))

    in_specs = [col(0), col(CONV_OUT), col(POOL_OUT)]
    in_specs += [col(k, D_MODEL) for k in range(N_BRANCH)]
    in_specs += [col(0, D_MODEL), _mod_spec(layer, 2, ctx),
                 _resident(g_post.shape), _resident(w_branch.shape), _resident(w_out.shape)]
    return pl.pallas_call(
        _merge_kernel,
        grid=(bt, L // tm),
        in_specs=in_specs,
        out_specs=col(0, D_MODEL),
        out_shape=jax.ShapeDtypeStruct((bt, L, D_MODEL), F32),
        compiler_params=_params(("parallel", "parallel")),
        name="mixer_merge",
    )(attn, *([z] * 5), x, mod, g_post, w_branch, w_out)


def _swiglu(h, wg_ref, wu_ref, wd_ref):
    f = None
    for c in range(D_FF // FF_CHUNK):
        cs = slice(c * FF_CHUNK, (c + 1) * FF_CHUNK)
        gt = jnp.dot(h, wg_ref[0, :, cs], preferred_element_type=F32)
        up = jnp.dot(h, wu_ref[0, :, cs], preferred_element_type=F32)
        a = (gt * jax.nn.sigmoid(gt) * up).astype(BF16)
        part = jnp.dot(a, wd_ref[0, cs, :], preferred_element_type=F32)
        f = part if f is None else f + part
    return f


def _ffn_kernel(x_ref, g_ref, sh_ref, sc_ref, gate_ref, gpost_ref, wg_ref, wu_ref, wd_ref, o_ref):
    h = (_rms(x_ref[0], g_ref[...]) * (1.0 + sc_ref[0]) + sh_ref[0]).astype(BF16)
    f = _swiglu(h, wg_ref, wu_ref, wd_ref)
    o_ref[0] = x_ref[0] + gate_ref[0] * _rms(f, gpost_ref[...])


def _ffn(x, g_pre, g_post, mod, wg, wu, wd, *, layer, ctx, tm):
    bt, L, _ = x.shape
    vec = pl.BlockSpec((1, D_MODEL), lambda b, i: (0, 0))
    return pl.pallas_call(
        _ffn_kernel,
        grid=(bt, L // tm),
        in_specs=[pl.BlockSpec((1, tm, D_MODEL), lambda b, i: (b, i, 0)),
                  vec,
                  _mod_spec(layer, 3, ctx),
                  _mod_spec(layer, 4, ctx),
                  _mod_spec(layer, 5, ctx),
                  vec,
                  _resident(wg.shape), _resident(wu.shape), _resident(wd.shape)],
        out_specs=pl.BlockSpec((1, tm, D_MODEL), lambda b, i: (b, i, 0)),
        out_shape=jax.ShapeDtypeStruct((bt, L, D_MODEL), F32),
        compiler_params=_params(("parallel", "parallel")),
        name="channel_mixer",
    )(x, g_pre, mod, mod, mod, g_post, wg, wu, wd)


ROUTE_TM = 512
GROUP_TM = 512
TOP_K = 2
MOE_PARTS = 2
ROW_TILE = (8, LANES)
SC_CORES, SC_SUBCORES = 2, 16
SC_WORKERS = SC_CORES * SC_SUBCORES
SC_CHUNK = 32
RANK_LANE, PROB_LANE, EXPERT_LANE = 0, 2, 4


def _flat_mod_spec(layer, chunk, ctx, tm, seq_len, tile0=0):
    def idx(i):
        row = CTX_MOD_ROW if ctx else ((i + tile0) * tm) // seq_len
        return (layer * MOD_ROWS + row, 0, chunk)
    return pl.BlockSpec((1, 1, D_MODEL), idx)


def _lane_pick(v, lane, k):
    return jnp.sum(jnp.where(lane == k, v, 0.0), axis=-1, keepdims=True)


def _route_kernel(x_ref, g_ref, sh_ref, sc_ref, wr_ref, h_ref, info_ref, cnt_ref, run_scr):
    @pl.when(pl.program_id(0) == 0)
    def _():
        run_scr[...] = jnp.zeros_like(run_scr)

    h = _rms(x_ref[...], g_ref[...]) * (1.0 + sc_ref[0]) + sh_ref[0]
    h_ref[...] = h.reshape(h_ref.shape)
    w = wr_ref[...]
    h_hi, w_hi = h.astype(BF16), w.astype(BF16)
    h_lo = (h - h_hi.astype(F32)).astype(BF16)
    w_lo = (w - w_hi.astype(F32)).astype(BF16)
    logits = (jnp.dot(h_hi, w_hi, preferred_element_type=F32)
              + jnp.dot(h_lo, w_hi, preferred_element_type=F32)
              + jnp.dot(h_hi, w_lo, preferred_element_type=F32))
    lane = lax.broadcasted_iota(jnp.int32, logits.shape, 1)
    neg = jnp.float32(-jnp.inf)
    l1 = jnp.where(lane < N_EXPERTS, logits, neg)
    m1 = jnp.max(l1, axis=-1, keepdims=True)
    i1 = jnp.min(jnp.where(l1 == m1, lane, LANES), axis=-1, keepdims=True)
    l2 = jnp.where(lane == i1, neg, l1)
    m2 = jnp.max(l2, axis=-1, keepdims=True)
    i2 = jnp.min(jnp.where(l2 == m2, lane, LANES), axis=-1, keepdims=True)
    e2 = jnp.exp(m2 - m1)
    den = 1.0 + e2

    chosen = jnp.where(jnp.logical_or(lane == i1, lane == i2), 1.0, 0.0)
    tm = chosen.shape[0]
    earlier = (lax.broadcasted_iota(jnp.int32, (tm, tm), 1)
               < lax.broadcasted_iota(jnp.int32, (tm, tm), 0))
    before = jnp.dot(jnp.where(earlier, 1.0, 0.0).astype(BF16), chosen.astype(BF16),
                     preferred_element_type=F32)
    rank = before + run_scr[...]
    rec = [_lane_pick(rank, lane, i1), _lane_pick(rank, lane, i2), 1.0 / den, e2 / den,
           i1.astype(F32), i2.astype(F32)]
    info = jnp.zeros_like(logits)
    for k, v in enumerate(rec):
        info = jnp.where(lane == k, v, info)
    info_ref[...] = info
    run_scr[...] += jnp.sum(chosen, axis=0, keepdims=True)
    cnt_ref[...] = run_scr[...]


def _route(x2, g, mod, wr_pad, *, layer, ctx, seq_len, tile0, n):
    tm = ROUTE_TM
    vec = pl.BlockSpec((1, D_MODEL), lambda i: (0, 0))
    return pl.pallas_call(
        _route_kernel,
        grid=(n // tm,),
        in_specs=[pl.BlockSpec((tm, D_MODEL), lambda i: (i + tile0, 0)),
                  vec,
                  _flat_mod_spec(layer, 3, ctx, tm, seq_len, tile0),
                  _flat_mod_spec(layer, 4, ctx, tm, seq_len, tile0),
                  pl.BlockSpec((D_MODEL, LANES), lambda i: (0, 0))],
        out_specs=[pl.BlockSpec((tm,) + ROW_TILE, lambda i: (i, 0, 0)),
                   pl.BlockSpec((tm, LANES), lambda i: (i, 0)),
                   pl.BlockSpec((1, LANES), lambda i: (0, 0))],
        out_shape=[jax.ShapeDtypeStruct((n,) + ROW_TILE, F32),
                   jax.ShapeDtypeStruct((n, LANES), F32),
                   jax.ShapeDtypeStruct((1, LANES), F32)],
        scratch_shapes=[pltpu.VMEM((1, LANES), F32)],
        compiler_params=_params(("arbitrary",)),
        name="route",
    )(x2, g, mod, mod, wr_pad)


def _sc_gather(table, idx):
    n_out = idx.shape[0]
    per_worker = n_out // SC_WORKERS
    n_chunks = per_worker // SC_CHUNK
    mesh = plsc.VectorSubcoreMesh(core_axis_name="c", subcore_axis_name="s")

    @functools.partial(
        pl.kernel, mesh=mesh,
        out_type=jax.ShapeDtypeStruct((n_out,) + table.shape[1:], table.dtype),
        scratch_types=[pltpu.VMEM((per_worker,), jnp.int32),
                       pltpu.VMEM((SC_CHUNK,) + table.shape[1:], table.dtype),
                       pltpu.VMEM((SC_CHUNK,) + table.shape[1:], table.dtype),
                       pltpu.SemaphoreType.DMA, pltpu.SemaphoreType.DMA],
        name="sc_row_gather")
    def gather(table_hbm, idx_hbm, out_hbm, idx_v, rows0, rows1, sem0, sem1):
        wid = lax.axis_index("s") * SC_CORES + lax.axis_index("c")
        base = wid * per_worker
        pltpu.sync_copy(idx_hbm.at[pl.ds(base, per_worker)], idx_v)

        def fetch(c, rows_v, sem):
            return pltpu.make_async_copy(
                table_hbm.at[idx_v.at[pl.ds(c * SC_CHUNK, SC_CHUNK)]], rows_v, sem)

        def flush(c, rows_v):
            pltpu.sync_copy(rows_v, out_hbm.at[pl.ds(base + c * SC_CHUNK, SC_CHUNK)])

        fetch(0, rows0, sem0).start()

        @pl.loop(0, n_chunks, step=2)
        def _(c):
            fetch(c + 1, rows1, sem1).start()
            fetch(c, rows0, sem0).wait()
            flush(c, rows0)

            @pl.when(c + 2 < n_chunks)
            def _():
                fetch(c + 2, rows0, sem0).start()

            fetch(c + 1, rows1, sem1).wait()
            flush(c + 1, rows1)

    return gather(table, idx)


def _sc_scatter(rows, dest):
    n = rows.shape[0]
    n_out = dest.size
    per_worker = n_out // SC_WORKERS
    n_chunks = per_worker // SC_CHUNK
    mesh = plsc.VectorSubcoreMesh(core_axis_name="c", subcore_axis_name="s")

    @functools.partial(
        pl.kernel, mesh=mesh,
        out_type=jax.ShapeDtypeStruct((n_out,) + rows.shape[1:], rows.dtype),
        scratch_types=[pltpu.VMEM((n_chunks, SC_CHUNK), jnp.int32),
                       pltpu.VMEM((SC_CHUNK,) + rows.shape[1:], rows.dtype),
                       pltpu.VMEM((SC_CHUNK,) + rows.shape[1:], rows.dtype),
                       pltpu.SemaphoreType.DMA, pltpu.SemaphoreType.DMA, pltpu.SemaphoreType.DMA],
        name="sc_row_scatter")
    def scatter(rows_hbm, dest_hbm, out_hbm, dest_v, rows0, rows1, sem0, sem1, wsem):
        wid = lax.axis_index("s") * SC_CORES + lax.axis_index("c")
        src_base = lax.rem(wid * per_worker, n)
        pltpu.sync_copy(dest_hbm.at[pl.ds(wid * n_chunks, n_chunks)], dest_v)

        def fetch(c, rows_v, sem):
            return pltpu.make_async_copy(
                rows_hbm.at[pl.ds(src_base + c * SC_CHUNK, SC_CHUNK)], rows_v, sem)

        def place(c, rows_v):
            pltpu.async_copy(rows_v, out_hbm.at[dest_v.at[c]], wsem).wait()

        fetch(0, rows0, sem0).start()

        @pl.loop(0, n_chunks, step=2)
        def _(c):
            fetch(c + 1, rows1, sem1).start()
            fetch(c, rows0, sem0).wait()
            place(c, rows0)

            @pl.when(c + 2 < n_chunks)
            def _():
                fetch(c + 2, rows0, sem0).start()

            fetch(c + 1, rows1, sem1).wait()
            place(c + 1, rows1)

    return scatter(rows, dest)


def _group_kernel(vblk_ref, vexp_ref, vlo_ref, vhi_ref, vfirst_ref, vvalid_ref,
                  h_ref, wg_ref, wu_ref, wd_ref, y_ref):
    del vblk_ref, vexp_ref
    v = pl.program_id(0)

    @pl.when(vvalid_ref[v] == 1)
    def _():
        tg = h_ref.shape[0]
        h = h_ref[...].reshape(tg, D_MODEL).astype(BF16)
        f = _swiglu(h, wg_ref, wu_ref, wd_ref).reshape(y_ref.shape)

        @pl.when(vfirst_ref[v] == 1)
        def _():
            y_ref[...] = f

        @pl.when(vfirst_ref[v] == 0)
        def _():
            row = lax.broadcasted_iota(jnp.int32, (tg, 1, 1), 0)
            mine = jnp.logical_and(row >= vlo_ref[v], row < vhi_ref[v])
            y_ref[...] = jnp.where(mine, f, y_ref[...])


def _grouped_experts(hs, visits, wg, wu, wd):
    n_rows = hs.shape[0]
    tg = GROUP_TM
    nv = visits[0].shape[0]

    def wspec(shape):
        return pl.BlockSpec((1,) + shape, lambda v, blk, exp, *_: (exp[v], 0, 0))

    row_spec = pl.BlockSpec((tg,) + ROW_TILE, lambda v, blk, *_: (blk[v], 0, 0))
    return pl.pallas_call(
        _group_kernel,
        grid_spec=pltpu.PrefetchScalarGridSpec(
            num_scalar_prefetch=len(visits),
            grid=(nv,),
            in_specs=[row_spec, wspec((D_MODEL, D_FF)), wspec((D_MODEL, D_FF)),
                      wspec((D_FF, D_MODEL))],
            out_specs=row_spec),
        out_shape=jax.ShapeDtypeStruct((n_rows,) + ROW_TILE, F32),
        compiler_params=_params(("arbitrary",)),
        name="grouped_experts",
    )(*visits, hs, wg, wu, wd)


def _visit_tables(counts, n_rows):
    tg = GROUP_TM
    nv = n_rows // tg + N_EXPERTS - 1
    ends = jnp.cumsum(counts)
    starts = ends - counts
    first_tile = starts // tg
    last_tile = jnp.maximum(ends - 1, 0) // tg
    nvis = jnp.where(counts > 0, last_tile - first_tile + 1, 0)
    vend = jnp.cumsum(nvis)
    total = vend[-1]
    v = jnp.minimum(jnp.arange(nv, dtype=jnp.int32), total - 1)
    mine = jnp.logical_and(v[:, None] >= (vend - nvis)[None, :], v[:, None] < vend[None, :])

    def pick(per_expert):
        return jnp.sum(jnp.where(mine, per_expert[None, :], 0), axis=1).astype(jnp.int32)

    exp = pick(jnp.arange(N_EXPERTS, dtype=jnp.int32))
    tile = v + pick(first_tile - (vend - nvis))
    lo = jnp.maximum(pick(starts), tile * tg) - tile * tg
    hi = jnp.minimum(pick(ends), (tile + 1) * tg) - tile * tg
    first = jnp.concatenate([jnp.ones((1,), jnp.int32), (tile[1:] != tile[:-1]).astype(jnp.int32)])
    valid = (jnp.arange(nv) < total).astype(jnp.int32)
    return tuple(a.astype(jnp.int32) for a in (tile, exp, lo, hi, first, valid)), starts


def _combine_kernel(x_ref, y1_ref, y2_ref, info_ref, gate_ref, gpost_ref, o_ref):
    info = info_ref[...]
    lane = lax.broadcasted_iota(jnp.int32, info.shape, 1)
    f = (_lane_pick(info, lane, PROB_LANE) * y1_ref[...].reshape(x_ref.shape)
         + _lane_pick(info, lane, PROB_LANE + 1) * y2_ref[...].reshape(x_ref.shape))
    o_ref[...] = x_ref[...] + gate_ref[0] * _rms(f, gpost_ref[...])


def _combine_into_kernel(x_ref, y1_ref, y2_ref, info_ref, gate_ref, gpost_ref, prev_ref, o_ref):
    del prev_ref
    _combine_kernel(x_ref, y1_ref, y2_ref, info_ref, gate_ref, gpost_ref, o_ref)


def _combine(x2, y12, info, mod, g_post, prev, *, layer, ctx, seq_len, tile0):
    n = x2.shape[0]
    tm = ROUTE_TM
    nt = info.shape[0] // tm
    row = pl.BlockSpec((tm, D_MODEL), lambda i: (i + tile0, 0))
    in_specs = [row, pl.BlockSpec((tm,) + ROW_TILE, lambda i: (i, 0, 0)),
                pl.BlockSpec((tm,) + ROW_TILE, lambda i: (nt + i, 0, 0)),
                pl.BlockSpec((tm, LANES), lambda i: (i, 0)),
                _flat_mod_spec(layer, 5, ctx, tm, seq_len, tile0),
                pl.BlockSpec((1, D_MODEL), lambda i: (0, 0))]
    args = [x2, y12, y12, info, mod, g_post]
    if prev is not None:
        in_specs.append(pl.BlockSpec(memory_space=pl.ANY))
        args.append(prev)
    return pl.pallas_call(
        _combine_kernel if prev is None else _combine_into_kernel,
        grid=(nt,),
        in_specs=in_specs,
        out_specs=row,
        out_shape=jax.ShapeDtypeStruct((n, D_MODEL), F32),
        input_output_aliases={} if prev is None else {len(args) - 1: 0},
        compiler_params=_params(("parallel",)),
        name="combine",
    )(*args)


def _moe(x, g_pre, g_post, mod, wr_pad, wg, wu, wd, *, layer, ctx):
    bt, seq_len, _ = x.shape
    n = bt * seq_len
    x2 = x.reshape(n, D_MODEL)
    parts = MOE_PARTS if n % (MOE_PARTS * SC_WORKERS * SC_CHUNK) == 0 else 1
    n_part = n // parts
    tiles_part = n_part // ROUTE_TM

    routed = [_route(x2, g_pre, mod, wr_pad, layer=layer, ctx=ctx, seq_len=seq_len,
                     tile0=p * tiles_part, n=n_part) for p in range(parts)]
    sorted_rows = []
    for h, info, cnt in routed:
        counts = cnt[0, :N_EXPERTS].astype(jnp.int32)
        visits, starts = _visit_tables(counts, TOP_K * n_part)
        expert = info[:, EXPERT_LANE:EXPERT_LANE + TOP_K].astype(jnp.int32)
        rank = info[:, RANK_LANE:RANK_LANE + TOP_K].astype(jnp.int32)
        pos = (starts[expert] + rank).T.reshape(-1)
        sorted_rows.append((_sc_scatter(h, pos.reshape(-1, SC_CHUNK)), visits, pos))
    gathered = [_sc_gather(_grouped_experts(hs, visits, wg, wu, wd), pos)
                for hs, visits, pos in sorted_rows]
    out = None
    for p, (y12, (_, info, _)) in enumerate(zip(gathered, routed)):
        out = _combine(x2, y12, info, mod, g_post, out, layer=layer, ctx=ctx, seq_len=seq_len,
                       tile0=p * tiles_part)
    return out.reshape(bt, seq_len, D_MODEL)


def _rope_tables(seq_len):
    rows = jnp.repeat(jnp.arange(seq_len // GRID_W), GRID_W).astype(F32)
    cols = jnp.tile(jnp.arange(GRID_W), seq_len // GRID_W).astype(F32)
    inv = ROPE_BASE ** (-jnp.arange(ROPE_FREQS, dtype=F32) / ROPE_FREQS)
    ang = jnp.concatenate([rows[:, None] * inv, rows[:, None] * inv,
                           cols[:, None] * inv, cols[:, None] * inv], axis=1)
    ang = jnp.tile(ang, (1, LANES // HEAD_QK))
    cos, sin = jnp.cos(ang), jnp.sin(ang)
    low = (jnp.arange(LANES) % (2 * ROPE_FREQS)) < ROPE_FREQS
    return cos, jnp.where(low, -sin, 0.0), jnp.where(low, 0.0, sin)


def kernel(x, c, ctx, c_ctx, w_mod, b_mod, g_pre_mix, g_post_mix, g_pre_ffn, g_post_ffn,
           w_in, lambda_q1, lambda_k1, lambda_q2, lambda_k2, g_subln, conv_w, pool_w,
           pool_scale, w_branch, w_out, ffn_w_gate, ffn_w_up, ffn_w_down, router_w,
           moe_w_gate, moe_w_up, moe_w_down):
    depth = w_in.shape[0]
    bsz, seq, _ = x.shape
    ctx_len = ctx.shape[1]

    cvec = jnp.zeros((MOD_ROWS, D_MODEL), F32).at[:bsz].set(c).at[CTX_MOD_ROW].set(c_ctx)
    mod = _modulation(cvec, w_mod, b_mod).reshape(depth * MOD_ROWS, 1, 6 * D_MODEL)
    tables = _rope_tables(seq)
    no_tables = tuple(t[:ctx_len] for t in tables)

    y = ctx
    for i in range(depth):
        last = i == depth - 1
        lam_init = 0.8 - 0.6 * math.exp(-0.3 * i)
        split = w_in.shape[2] - GATES_W
        w_in_i = jnp.concatenate([0.5 * w_in[i, :, split:], w_in[i, :, :split]], axis=1).astype(BF16)
        lam_vecs = jnp.stack([lambda_q1[i], lambda_k1[i], lambda_q2[i], lambda_k2[i]]).astype(F32)
        g_sub = g_subln[i].reshape(1, HEAD_V)
        g_pm, g_qm = g_pre_mix[i].reshape(1, D_MODEL), g_post_mix[i].reshape(1, D_MODEL)
        g_pf, g_qf = g_pre_ffn[i].reshape(1, D_MODEL), g_post_ffn[i].reshape(1, D_MODEL)
        mix_w = (conv_w[i], pool_w[i].astype(BF16), pool_scale[i].reshape(1, BRANCH_W))
        out_w = (w_branch[i].astype(BF16), (0.5 * w_out[i]).astype(BF16))

        z, kn = _inproj(x, g_pm, mod, w_in_i, tables, mix_w, layer=i, ctx=False, rope=True,
                        col_start=0, tm=INPROJ_TM)
        kv_full = (K_COL * HEAD_BLKS, V_COL * HEAD_BLKS)
        if last:
            w_kv = w_in_i[:, K_COL * COL_BLK:(V_COL + 1) * COL_BLK]
            zc, knc = _inproj(y, g_pm, mod, w_kv, no_tables, layer=i, ctx=True, rope=False,
                              col_start=K_COL, tm=ctx_len)
            segs = [(z,) + kv_full, (zc, 0, HEAD_BLKS)]
        else:
            zc, knc = _inproj(y, g_pm, mod, w_in_i, no_tables, mix_w, layer=i, ctx=True, rope=False,
                              col_start=0, tm=ctx_len)
            segs = [(z,) + kv_full, (zc,) + kv_full]
            attn_c = _attention(zc, [(zc,) + kv_full], [knc], lam_vecs, g_sub, lam_init=lam_init,
                                tq=ctx_len)
            y = _merge(attn_c, zc, y, mod, g_qm, *out_w, layer=i, ctx=True, tm=ctx_len)
        attn_l = _attention(z, segs, [kn, knc], lam_vecs, g_sub, lam_init=lam_init, tq=ATTN_TQ)
        x = _merge(attn_l, z, x, mod, g_qm, *out_w, layer=i, ctx=False, tm=MERGE_TM)

        j = i // 2
        streams = [(x, False, FFN_TM)] + ([] if last else [(y, True, ctx_len)])
        if i % 2 == 0:
            ffn_w = (ffn_w_gate[j:j + 1].astype(BF16), ffn_w_up[j:j + 1].astype(BF16),
                     ffn_w_down[j:j + 1].astype(BF16))
            outs = [_ffn(t, g_pf, g_qf, mod, *ffn_w, layer=i, ctx=is_ctx, tm=tm)
                    for t, is_ctx, tm in streams]
        else:
            ffn_w = (moe_w_gate[j].astype(BF16), moe_w_up[j].astype(BF16), moe_w_down[j].astype(BF16))
            wr_pad = jnp.zeros((D_MODEL, LANES), F32).at[:, :N_EXPERTS].set(router_w[j])
            outs = [_moe(t, g_pf, g_qf, mod, wr_pad, *ffn_w, layer=i, ctx=is_ctx)
                    for t, is_ctx, _ in streams]
        x = outs[0]
        if not last:
            y = outs[1]
    return x
```
